```python
import jax, jax.numpy as jnp
from jax import lax
import numpy as np

D_MODEL = 1024
BATCH = 8
SEQ = 2048
DEPTH = 2
DEC_BATCH = 128
DEC_SEQ = 4
PAST_LEN = 16384
PAGE_SIZE = 128

MIX_WIDTH = D_MODEL
N_MIXERS = 4
GROUP_WIDTH = MIX_WIDTH // N_MIXERS
N_SUB = 4
SUB_DIM = GROUP_WIDTH // N_SUB
POOL_WINDOWS = (2, 4, 8, 16)
POOL_BUF = max(POOL_WINDOWS) - 1
CONV_WIDTH = 31
SHORT_WIDTH = 3
CHUNK = 128
D_FF = 4 * D_MODEL
EPS = 1e-6
IN_WIDTH = GROUP_WIDTH * (1 + 2 + 2 + 3)

kernel_name = "hybrid_parallel_pool_conv_sgu_shortconv_step"


def rms_norm(x, g):
    xf = x.astype(jnp.float32)
    y = xf * lax.rsqrt(jnp.mean(xf * xf, axis=-1, keepdims=True) + EPS)
    return (y * g.astype(jnp.float32)).astype(x.dtype)


def layer_norm(x, g, b):
    xf = x.astype(jnp.float32)
    mu = jnp.mean(xf, axis=-1, keepdims=True)
    xc = xf - mu
    var = jnp.mean(xc * xc, axis=-1, keepdims=True)
    y = xc * lax.rsqrt(var + EPS) * g.astype(jnp.float32) + b.astype(jnp.float32)
    return y.astype(x.dtype)


def causal_depthwise_conv(x, buf, w):
    ext = jnp.concatenate([buf.astype(x.dtype), x], axis=1)
    k = w.shape[0]
    y = lax.conv_general_dilated(ext, w.astype(ext.dtype)[:, None, :], window_strides=(1,),
                                 padding="VALID", dimension_numbers=("NWC", "WIO", "NWC"),
                                 feature_group_count=x.shape[-1])
    return y, ext[:, ext.shape[1] - (k - 1):]


def pool_mixer(a, buf, pos0, w_pool, scale):
    bn, t, _ = a.shape
    p = POOL_BUF
    ext = jnp.concatenate([buf.astype(jnp.float32), a.astype(jnp.float32)], axis=1)
    cs = jnp.concatenate([jnp.zeros((bn, 1, GROUP_WIDTH), jnp.float32),
                          jnp.cumsum(ext, axis=1)], axis=1)
    pos = pos0 + jnp.arange(t)
    cur = ext[:, p:]
    outs = []
    for gi, w in enumerate(POOL_WINDOWS):
        sl = slice(gi * SUB_DIM, (gi + 1) * SUB_DIM)
        win_sum = cs[:, p + 1:p + 1 + t, sl] - cs[:, p + 1 - w:p + 1 - w + t, sl]
        cnt = jnp.minimum(w, pos + 1).astype(jnp.float32)[None, :, None]
        outs.append(win_sum / cnt - cur[..., sl])
    d = jnp.stack(outs, axis=2)
    y = jnp.einsum("btgc,gcd->btgd", d, w_pool.astype(jnp.float32)).reshape(bn, t, GROUP_WIDTH)
    y = y * scale.astype(jnp.float32)
    return y.astype(a.dtype), ext[:, ext.shape[1] - p:].astype(a.dtype)


def conformer_conv(z, buf, conv_w, conv_b, ln_g, ln_b):
    p, gate = jnp.split(z, 2, axis=-1)
    g = p * jax.nn.sigmoid(gate)
    c, new_buf = causal_depthwise_conv(g, buf, conv_w)
    c = c + conv_b
    bn, t, _ = c.shape
    c = layer_norm(c.reshape(bn, t, N_SUB, SUB_DIM), ln_g.reshape(N_SUB, SUB_DIM),
                   ln_b.reshape(N_SUB, SUB_DIM))
    return jax.nn.silu(c).reshape(bn, t, GROUP_WIDTH), new_buf


def chunk_spatial_gating(z, ln_g, ln_b, w_s, b_s):
    bn, t, _ = z.shape
    u, v = jnp.split(z, 2, axis=-1)
    v = layer_norm(v.reshape(bn, t, N_SUB, SUB_DIM), ln_g.reshape(N_SUB, SUB_DIM),
                   ln_b.reshape(N_SUB, SUB_DIM))
    n_chunks = -(-t // CHUNK)
    pad = n_chunks * CHUNK - t
    vp = jnp.pad(v, ((0, 0), (0, pad), (0, 0), (0, 0))).reshape(bn, n_chunks, CHUNK, N_SUB, SUB_DIM)
    mask = jnp.tril(jnp.ones((CHUNK, CHUNK), dtype=bool))
    w_causal = jnp.where(mask[None], w_s, jnp.zeros((), w_s.dtype))
    s = jnp.einsum("hts,bnshd->bnthd", w_causal, vp) + jnp.transpose(b_s)[None, None, :, :, None]
    s = s.reshape(bn, n_chunks * CHUNK, N_SUB, SUB_DIM)[:, :t].reshape(bn, t, GROUP_WIDTH)
    return u * s, v.reshape(bn, t, GROUP_WIDTH)


def short_gated_conv(z, buf, w):
    bg, cg, h = jnp.split(z, 3, axis=-1)
    c, new_buf = causal_depthwise_conv(cg * h, buf, w)
    return bg * c, new_buf


def decoder_layer(x, pos0, pool_buf, conv_buf, short_buf, g_mix_pre, g_mix_post, g_ffn_pre,
                  g_ffn_post, w_in, w_out, w_pool, pool_scale, conv_w, conv_b, conv_ln_g,
                  conv_ln_b, sgu_ln_g, sgu_ln_b, sgu_w, sgu_b, short_w, w_up, w_down):
    gw = GROUP_WIDTH
    h = rms_norm(x, g_mix_pre)
    z = h @ w_in
    y_a, nb_pool = pool_mixer(z[..., :gw], pool_buf, pos0, w_pool, pool_scale)
    y_b, nb_conv = conformer_conv(z[..., gw:3 * gw], conv_buf, conv_w, conv_b, conv_ln_g, conv_ln_b)
    y_c, v_rows = chunk_spatial_gating(z[..., 3 * gw:5 * gw], sgu_ln_g, sgu_ln_b, sgu_w, sgu_b)
    y_d, nb_short = short_gated_conv(z[..., 5 * gw:], short_buf, short_w)
    o = jnp.concatenate([y_a, y_b, y_c, y_d], axis=-1) @ w_out
    x = x + rms_norm(o, g_mix_post)
    f = rms_norm(x, g_ffn_pre)
    f = jnp.square(jax.nn.relu(f @ w_up)) @ w_down
    x = x + rms_norm(f, g_ffn_post)
    return x, nb_pool, nb_conv, nb_short, v_rows


def setup_inputs(seed: int = 0) -> dict:
    key = jax.random.key(seed)
    ks = jax.random.split(key, 24)

    def nrm(k, shape, scale):
        return jax.random.normal(k, shape, jnp.float32) * scale

    return {
        "x_prompt": nrm(ks[0], (BATCH, SEQ, D_MODEL), 1.0),
        "x_sample": nrm(ks[1], (DEC_BATCH, DEC_SEQ, D_MODEL), 1.0),
        "state_pool": nrm(ks[2], (DEPTH, DEC_BATCH, POOL_BUF, GROUP_WIDTH), 1.0),
        "state_conv": nrm(ks[3], (DEPTH, DEC_BATCH, CONV_WIDTH - 1, GROUP_WIDTH), 0.5),
        "state_short": nrm(ks[4], (DEPTH, DEC_BATCH, SHORT_WIDTH - 1, GROUP_WIDTH), 1.0),
        "norm_mix_pre": 1.0 + nrm(ks[5], (DEPTH, D_MODEL), 0.1),
        "norm_mix_post": 1.0 + nrm(ks[6], (DEPTH, D_MODEL), 0.1),
        "norm_ffn_pre": 1.0 + nrm(ks[7], (DEPTH, D_MODEL), 0.1),
        "norm_ffn_post": 1.0 + nrm(ks[8], (DEPTH, D_MODEL), 0.1),
        "w_in": nrm(ks[9], (DEPTH, D_MODEL, IN_WIDTH), D_MODEL ** -0.5),
        "w_out": nrm(ks[10], (DEPTH, MIX_WIDTH, D_MODEL), MIX_WIDTH ** -0.5),
        "w_pool": nrm(ks[11], (DEPTH, N_SUB, SUB_DIM, SUB_DIM), SUB_DIM ** -0.5),
        "pool_scale": 1.0 + nrm(ks[12], (DEPTH, GROUP_WIDTH), 0.1),
        "conv_w": nrm(ks[13], (DEPTH, CONV_WIDTH, GROUP_WIDTH), CONV_WIDTH ** -0.5),
        "conv_b": nrm(ks[14], (DEPTH, GROUP_WIDTH), 0.01),
        "conv_ln_g": 1.0 + nrm(ks[15], (DEPTH, GROUP_WIDTH), 0.1),
        "conv_ln_b": nrm(ks[16], (DEPTH, GROUP_WIDTH), 0.01),
        "sgu_ln_g": 1.0 + nrm(ks[17], (DEPTH, GROUP_WIDTH), 0.1),
        "sgu_ln_b": nrm(ks[18], (DEPTH, GROUP_WIDTH), 0.01),
        "sgu_w": nrm(ks[19], (DEPTH, N_SUB, CHUNK, CHUNK), CHUNK ** -0.5),
        "sgu_b": 1.0 + nrm(ks[20], (DEPTH, N_SUB, CHUNK), 0.1),
        "short_w": nrm(ks[21], (DEPTH, SHORT_WIDTH, GROUP_WIDTH), SHORT_WIDTH ** -0.5),
        "w_ffn_up": nrm(ks[22], (DEPTH, D_MODEL, D_FF), D_MODEL ** -0.5),
        "w_ffn_down": nrm(ks[23], (DEPTH, D_FF, D_MODEL), D_FF ** -0.5),
    }


def reference(x_prompt, x_sample, state_pool, state_conv, state_short, norm_mix_pre,
              norm_mix_post, norm_ffn_pre, norm_ffn_post, w_in, w_out, w_pool, pool_scale,
              conv_w, conv_b, conv_ln_g, conv_ln_b, sgu_ln_g, sgu_ln_b, sgu_w, sgu_b, short_w,
              w_ffn_up, w_ffn_down):
    bp = x_prompt.shape[0]
    yp = x_prompt
    ys = x_sample
    pool_p, pool_s, conv_p, conv_s, short_p, short_s, v_s = [], [], [], [], [], [], []
    for l in range(DEPTH):
        params = (norm_mix_pre[l], norm_mix_post[l], norm_ffn_pre[l], norm_ffn_post[l], w_in[l],
                  w_out[l], w_pool[l], pool_scale[l], conv_w[l], conv_b[l], conv_ln_g[l],
                  conv_ln_b[l], sgu_ln_g[l], sgu_ln_b[l], sgu_w[l], sgu_b[l], short_w[l],
                  w_ffn_up[l], w_ffn_down[l])
        yp, pa, pb, pd, _ = decoder_layer(
            yp, 0,
            jnp.zeros((bp, POOL_BUF, GROUP_WIDTH), yp.dtype),
            jnp.zeros((bp, CONV_WIDTH - 1, GROUP_WIDTH), yp.dtype),
            jnp.zeros((bp, SHORT_WIDTH - 1, GROUP_WIDTH), yp.dtype),
            *params)
        ys, sa, sb, sd, sv = decoder_layer(ys, PAST_LEN, state_pool[l], state_conv[l],
                                           state_short[l], *params)
        pool_p.append(pa); pool_s.append(sa)
        conv_p.append(pb); conv_s.append(sb)
        short_p.append(pd); short_s.append(sd)
        v_s.append(sv)
    return (yp, ys, jnp.stack(pool_p), jnp.stack(pool_s), jnp.stack(conv_p), jnp.stack(conv_s),
            jnp.stack(short_p), jnp.stack(short_s), jnp.stack(v_s))
```

```python
import functools

import jax
import jax.numpy as jnp
from jax import lax
from jax.experimental import pallas as pl
from jax.experimental.pallas import tpu as pltpu

D_MODEL = 1024
GROUP_WIDTH = 256
N_SUB = 4
SUB_DIM = 64
POOL_WINDOWS = (2, 4, 8, 16)
POOL_BUF = 15
CONV_WIDTH = 31
SHORT_WIDTH = 3
CHUNK = 128
D_FF = 4096
EPS = 1e-6
PAST_LEN = 16384
IN_WIDTH = 8 * GROUP_WIDTH

SUBLANES = 8
POOL_HIST = 16
CONV_HIST = 32
SHORT_HIST = 8

ROW_BLOCK = CHUNK
PROMPT_TILE = 512
FFN_TILE = 512
FF_CHUNK = 512
VMEM_LIMIT_BYTES = 56 * 1024 * 1024

F32 = jnp.float32
BF16 = jnp.bfloat16


def _rms_norm(x, g):
    ms = jnp.mean(x * x, axis=-1, keepdims=True)
    return x * lax.rsqrt(ms + EPS) * g


def _dot(a, b):
    return jnp.dot(a, b, preferred_element_type=F32)


def _dot_split(x, m):
    hi = x.astype(BF16)
    lo = (x - hi.astype(F32)).astype(BF16)
    return _dot(hi, m) + _dot(lo, m)


def _head_layer_norm(x, avg, g, b):
    mu = _dot_split(x, avg)
    xc = x - mu
    var = _dot_split(xc * xc, avg)
    return xc * lax.rsqrt(var + EPS) * g + b


def _shift_up(x, r):
    n = x.shape[0]
    r = r % n
    if r == 0:
        return x
    return pltpu.roll(x, n - r, axis=0)


def _shift_down(x, r):
    if r % x.shape[0] == 0:
        return x
    return pltpu.roll(x, r % x.shape[0], axis=0)


def _lane_group(shape):
    return jnp.right_shift(lax.broadcasted_iota(jnp.int32, shape, 1), SUB_DIM.bit_length() - 1)


def _pool_window(shape):
    grp = _lane_group(shape)
    w = jnp.full(shape, POOL_WINDOWS[0], jnp.int32)
    for gi in range(1, N_SUB):
        w = jnp.where(grp == gi, POOL_WINDOWS[gi], w)
    return grp, w


def _pool_delta(ext, pos0):
    r = ext.shape[0] - POOL_HIST
    s2 = ext + _shift_down(ext, 1)
    s4 = s2 + _shift_down(s2, 2)
    s8 = s4 + _shift_down(s4, 4)
    s16 = s8 + _shift_down(s8, 8)
    sums = (s2[POOL_HIST:], s4[POOL_HIST:], s8[POOL_HIST:], s16[POOL_HIST:])
    cur = ext[POOL_HIST:]
    grp, w = _pool_window((r, GROUP_WIDTH))
    win = sums[0]
    for gi in range(1, N_SUB):
        win = jnp.where(grp == gi, sums[gi], win)
    pos = pos0 + lax.broadcasted_iota(jnp.int32, (r, GROUP_WIDTH), 0)
    cnt = jnp.minimum(w, pos + 1).astype(F32)
    return win / cnt - cur


def _causal_taps(ext, w_ref, hist, n_taps, rows):
    first = hist - (n_taps - 1)
    shifted = {}
    acc = None
    for k in range(n_taps):
        off = first + k
        res = off % SUBLANES
        if res not in shifted:
            shifted[res] = _shift_up(ext, res)
        base = off - res
        term = shifted[res][base:base + rows] * w_ref[k:k + 1, :]
        acc = term if acc is None else acc + term
    return acc


def _gating_rhs(vn):
    grp = _lane_group(vn.shape)
    return jnp.concatenate([jnp.where(grp == h, vn, 0.0).astype(BF16) for h in range(N_SUB)], axis=0)


def _prompt_mixer_kernel(x_ref, gpre_ref, gpost_ref, win_ref, wout_ref, wpool_ref, pscale_ref,
                         convw_ref, convb_ref, clng_ref, clnb_ref, slng_ref, slnb_ref,
                         swcat_ref, sbias_ref, shortw_ref, avg_ref,
                         out_ref, pool_out_ref, conv_out_ref, short_out_ref,
                         z_ref, y_ref, pool_ext, conv_ext, short_ext):
    j = pl.program_id(1)
    tile = x_ref.shape[0]
    n_blocks = tile // ROW_BLOCK

    @pl.when(j == 0)
    def _():
        pool_ext[0:POOL_HIST, :] = jnp.zeros((POOL_HIST, GROUP_WIDTH), F32)
        conv_ext[0:CONV_HIST, :] = jnp.zeros((CONV_HIST, GROUP_WIDTH), F32)
        short_ext[0:SHORT_HIST, :] = jnp.zeros((SHORT_HIST, GROUP_WIDTH), F32)

    x = x_ref[...]
    h = _rms_norm(x, gpre_ref[...]).astype(BF16)
    z_ref[...] = _dot(h, win_ref[...])
    gw = GROUP_WIDTH
    avg = avg_ref[...]

    def block(i, carry):
        r0 = pl.multiple_of(i * ROW_BLOCK, ROW_BLOCK)
        rows = pl.ds(r0, ROW_BLOCK)

        a = z_ref[rows, 0:gw]
        pool_ext[pl.ds(POOL_HIST + r0, ROW_BLOCK), :] = a
        d = _pool_delta(pool_ext[pl.ds(r0, POOL_HIST + ROW_BLOCK), :], j * tile + r0)
        y_ref[rows, 0:gw] = (_dot(d.astype(BF16), wpool_ref[...]) * pscale_ref[...]).astype(BF16)

        g = z_ref[rows, gw:2 * gw] * jax.nn.sigmoid(z_ref[rows, 2 * gw:3 * gw])
        conv_ext[pl.ds(CONV_HIST + r0, ROW_BLOCK), :] = g
        c = _causal_taps(conv_ext[pl.ds(r0, CONV_HIST + ROW_BLOCK), :], convw_ref,
                         CONV_HIST, CONV_WIDTH, ROW_BLOCK) + convb_ref[...]
        c = _head_layer_norm(c, avg, clng_ref[...], clnb_ref[...])
        y_ref[rows, gw:2 * gw] = (c * jax.nn.sigmoid(c)).astype(BF16)

        vn = _head_layer_norm(z_ref[rows, 4 * gw:5 * gw], avg, slng_ref[...], slnb_ref[...])
        s = _dot(swcat_ref[...], _gating_rhs(vn)) + sbias_ref[...]
        y_ref[rows, 2 * gw:3 * gw] = (z_ref[rows, 3 * gw:4 * gw] * s).astype(BF16)

        ch = z_ref[rows, 6 * gw:7 * gw] * z_ref[rows, 7 * gw:8 * gw]
        short_ext[pl.ds(SHORT_HIST + r0, ROW_BLOCK), :] = ch
        sc = _causal_taps(short_ext[pl.ds(r0, SHORT_HIST + ROW_BLOCK), :], shortw_ref,
                          SHORT_HIST, SHORT_WIDTH, ROW_BLOCK)
        y_ref[rows, 3 * gw:4 * gw] = (z_ref[rows, 5 * gw:6 * gw] * sc).astype(BF16)
        return carry

    lax.fori_loop(0, n_blocks, block, 0)

    o = _dot(y_ref[...], wout_ref[...])
    out_ref[...] = x + _rms_norm(o, gpost_ref[...])

    pool_tail = pool_ext[tile:tile + POOL_HIST, :]
    conv_tail = conv_ext[tile:tile + CONV_HIST, :]
    short_tail = short_ext[tile:tile + SHORT_HIST, :]
    pool_ext[0:POOL_HIST, :] = pool_tail
    conv_ext[0:CONV_HIST, :] = conv_tail
    short_ext[0:SHORT_HIST, :] = short_tail

    @pl.when(j == pl.num_programs(1) - 1)
    def _():
        pool_out_ref[...] = _shift_up(pool_tail, POOL_HIST - POOL_BUF)[0:POOL_BUF]
        conv_out_ref[...] = _shift_up(conv_tail, CONV_HIST - (CONV_WIDTH - 1))[0:CONV_WIDTH - 1]
        short_out_ref[...] = _shift_up(short_tail, SHORT_HIST - (SHORT_WIDTH - 1))[0:SHORT_WIDTH - 1]


def _sample_mixer_kernel(x_ref, pool_in_ref, conv_in_ref, short_in_ref,
                         gpre_ref, gpost_ref, win_ref, wout_ref, wpool_ref, pscale_ref,
                         convw_ref, convb_ref, clng_ref, clnb_ref, slng_ref, slnb_ref,
                         sgw_ref, sgb_ref, shortw_ref, avg_ref,
                         out_ref, pool_out_ref, conv_out_ref, short_out_ref, v_out_ref,
                         z_ref, y_ref):
    n_steps, n_seq = v_out_ref.shape[0], v_out_ref.shape[1]
    gw = GROUP_WIDTH
    x = x_ref[...]
    h = _rms_norm(x, gpre_ref[...]).astype(BF16)
    z_ref[...] = _dot(h, win_ref[...])
    avg = avg_ref[...]
    grp, w = _pool_window((n_seq, gw))
    cnt = jnp.minimum(w, PAST_LEN + 1).astype(F32)

    def slab(t):
        return slice(t * n_seq, (t + 1) * n_seq)

    a_new = [z_ref[slab(t), 0:gw] for t in range(n_steps)]
    g_new = [z_ref[slab(t), gw:2 * gw] * jax.nn.sigmoid(z_ref[slab(t), 2 * gw:3 * gw])
             for t in range(n_steps)]
    ch_new = [z_ref[slab(t), 6 * gw:7 * gw] * z_ref[slab(t), 7 * gw:8 * gw]
              for t in range(n_steps)]

    def pool_row(i):
        return pool_in_ref[i] if i < POOL_BUF else a_new[i - POOL_BUF]

    def conv_row(i):
        return conv_in_ref[i] if i < CONV_WIDTH - 1 else g_new[i - (CONV_WIDTH - 1)]

    def short_row(i):
        return short_in_ref[i] if i < SHORT_WIDTH - 1 else ch_new[i - (SHORT_WIDTH - 1)]

    vn = []
    for t in range(n_steps):
        rows = slab(t)
        end = POOL_BUF + t
        acc = pool_row(end)
        sums = []
        for k in range(1, max(POOL_WINDOWS)):
            acc = acc + pool_row(end - k)
            if k + 1 in POOL_WINDOWS:
                sums.append(acc)
        win = sums[0]
        for gi in range(1, N_SUB):
            win = jnp.where(grp == gi, sums[gi], win)
        d = win / cnt - a_new[t]
        y_ref[rows, 0:gw] = (_dot(d.astype(BF16), wpool_ref[...]) * pscale_ref[...]).astype(BF16)

        c = None
        for k in range(CONV_WIDTH):
            term = conv_row(t + k) * convw_ref[k:k + 1, :]
            c = term if c is None else c + term
        c = _head_layer_norm(c + convb_ref[...], avg, clng_ref[...], clnb_ref[...])
        y_ref[rows, gw:2 * gw] = (c * jax.nn.sigmoid(c)).astype(BF16)

        vn.append(_head_layer_norm(z_ref[rows, 4 * gw:5 * gw], avg, slng_ref[...], slnb_ref[...]))
        v_out_ref[t] = vn[t]
        s = sgb_ref[t:t + 1, :]
        for u in range(t + 1):
            s = s + sgw_ref[t * n_steps + u:t * n_steps + u + 1, :] * vn[u]
        y_ref[rows, 2 * gw:3 * gw] = (z_ref[rows, 3 * gw:4 * gw] * s).astype(BF16)

        sc = None
        for k in range(SHORT_WIDTH):
            term = short_row(t + k) * shortw_ref[k:k + 1, :]
            sc = term if sc is None else sc + term
        y_ref[rows, 3 * gw:4 * gw] = (z_ref[rows, 5 * gw:6 * gw] * sc).astype(BF16)

    o = _dot(y_ref[...], wout_ref[...])
    out_ref[...] = x + _rms_norm(o, gpost_ref[...])

    for i in range(POOL_BUF):
        pool_out_ref[i] = pool_row(i + n_steps)
    for i in range(CONV_WIDTH - 1):
        conv_out_ref[i] = conv_row(i + n_steps)
    for i in range(SHORT_WIDTH - 1):
        short_out_ref[i] = short_row(i + n_steps)


def _ffn_kernel(x_ref, gpre_ref, gpost_ref, wup_ref, wdown_ref, out_ref, acc_ref):
    x = x_ref[...]
    f = _rms_norm(x, gpre_ref[...]).astype(BF16)
    for c in range(D_FF // FF_CHUNK):
        cols = slice(c * FF_CHUNK, (c + 1) * FF_CHUNK)
        u = jnp.maximum(_dot(f, wup_ref[:, cols]), 0.0)
        part = _dot((u * u).astype(BF16), wdown_ref[cols, :])
        if c == 0:
            acc_ref[...] = part
        else:
            acc_ref[...] += part
    out_ref[...] = x + _rms_norm(acc_ref[...], gpost_ref[...])


def _const_spec(shape):
    nd = len(shape)
    return pl.BlockSpec(shape, lambda *_: (0,) * nd)


def _compiler_params(semantics):
    return pltpu.CompilerParams(dimension_semantics=semantics, vmem_limit_bytes=VMEM_LIMIT_BYTES)


def _prompt_mixer(x, p):
    b, s, d = x.shape
    tile = PROMPT_TILE
    consts = (p["gpre"], p["gpost"], p["w_in"], p["w_out"], p["wpool"], p["pscale"],
              p["conv_w"], p["conv_b"], p["cln_g"], p["cln_b"], p["sln_g"], p["sln_b"],
              p["sgu_wcat"], p["sgu_bias"], p["short_w"], p["avg"])
    gw = GROUP_WIDTH
    state_spec = lambda n: pl.BlockSpec((None, n, gw), lambda bi, j: (bi, 0, 0))
    return pl.pallas_call(
        _prompt_mixer_kernel,
        grid=(b, s // tile),
        in_specs=[pl.BlockSpec((None, tile, d), lambda bi, j: (bi, j, 0))]
        + [_const_spec(c.shape) for c in consts],
        out_specs=[pl.BlockSpec((None, tile, d), lambda bi, j: (bi, j, 0)),
                   state_spec(POOL_BUF), state_spec(CONV_WIDTH - 1), state_spec(SHORT_WIDTH - 1)],
        out_shape=[jax.ShapeDtypeStruct((b, s, d), F32),
                   jax.ShapeDtypeStruct((b, POOL_BUF, gw), F32),
                   jax.ShapeDtypeStruct((b, CONV_WIDTH - 1, gw), F32),
                   jax.ShapeDtypeStruct((b, SHORT_WIDTH - 1, gw), F32)],
        scratch_shapes=[pltpu.VMEM((tile, IN_WIDTH), F32),
                        pltpu.VMEM((tile, d), BF16),
                        pltpu.VMEM((POOL_HIST + tile, gw), F32),
                        pltpu.VMEM((CONV_HIST + tile, gw), F32),
                        pltpu.VMEM((SHORT_HIST + tile, gw), F32)],
        compiler_params=_compiler_params(("arbitrary", "arbitrary")),
        name="prompt_mixer",
    )(x, *consts)


def _sample_mixer(x, pool_st, conv_st, short_st, p):
    n, d = x.shape
    n_seq = pool_st.shape[1]
    n_steps = n // n_seq
    gw = GROUP_WIDTH
    args = (x, pool_st, conv_st, short_st,
            p["gpre"], p["gpost"], p["w_in"], p["w_out"], p["wpool"], p["pscale"],
            p["conv_w"], p["conv_b"], p["cln_g"], p["cln_b"], p["sln_g"], p["sln_b"],
            p["sgu_w4"], p["sgu_b4"], p["short_w"], p["avg"])
    out_shape = [jax.ShapeDtypeStruct((n, d), F32),
                 jax.ShapeDtypeStruct(pool_st.shape, F32),
                 jax.ShapeDtypeStruct(conv_st.shape, F32),
                 jax.ShapeDtypeStruct(short_st.shape, F32),
                 jax.ShapeDtypeStruct((n_steps, n_seq, gw), F32)]
    return pl.pallas_call(
        _sample_mixer_kernel,
        grid=(1,),
        in_specs=[_const_spec(a.shape) for a in args],
        out_specs=[_const_spec(o.shape) for o in out_shape],
        out_shape=out_shape,
        scratch_shapes=[pltpu.VMEM((n, IN_WIDTH), F32), pltpu.VMEM((n, d), BF16)],
        compiler_params=_compiler_params(("arbitrary",)),
        name="sample_mixer",
    )(*args)


def _ffn(x, p, name):
    n, d = x.shape
    tile = min(FFN_TILE, n)
    consts = (p["fpre"], p["fpost"], p["w_up"], p["w_down"])
    return pl.pallas_call(
        _ffn_kernel,
        grid=(n // tile,),
        in_specs=[pl.BlockSpec((tile, d), lambda i: (i, 0))] + [_const_spec(c.shape) for c in consts],
        out_specs=pl.BlockSpec((tile, d), lambda i: (i, 0)),
        out_shape=jax.ShapeDtypeStruct((n, d), F32),
        scratch_shapes=[pltpu.VMEM((tile, d), F32)],
        compiler_params=_compiler_params(("arbitrary",)),
        name=name,
    )(x, *consts)


def _layer_params(l, n_steps, norm_mix_pre, norm_mix_post, norm_ffn_pre, norm_ffn_post, w_in, w_out,
                  w_pool, pool_scale, conv_w, conv_b, conv_ln_g, conv_ln_b, sgu_ln_g, sgu_ln_b,
                  sgu_w, sgu_b, short_w, w_ffn_up, w_ffn_down):
    row = lambda v: v[l].reshape(1, -1)
    gw = GROUP_WIDTH
    head_of_lane = jnp.arange(gw) // SUB_DIM
    same_head = head_of_lane[:, None] == head_of_lane[None, :]
    wp = jnp.tile(w_pool[l].reshape(gw, SUB_DIM), (1, N_SUB))
    wpool = jnp.where(same_head, wp, 0.0).astype(BF16)
    avg = jnp.where(same_head, 1.0 / SUB_DIM, 0.0).astype(BF16)
    tril = jnp.tril(jnp.ones((CHUNK, CHUNK), dtype=bool))
    w_causal = jnp.where(tril[None], sgu_w[l], 0.0)
    wcat = jnp.transpose(w_causal, (1, 0, 2)).reshape(CHUNK, N_SUB * CHUNK).astype(BF16)
    bias = jnp.repeat(jnp.transpose(sgu_b[l]), SUB_DIM, axis=1)
    w4 = jnp.repeat(jnp.transpose(w_causal[:, :n_steps, :n_steps], (1, 2, 0)), SUB_DIM, axis=2)
    w4 = w4.reshape(n_steps * n_steps, gw)
    return dict(
        gpre=row(norm_mix_pre), gpost=row(norm_mix_post), fpre=row(norm_ffn_pre),
        fpost=row(norm_ffn_post), w_in=w_in[l].astype(BF16), w_out=w_out[l].astype(BF16),
        wpool=wpool, pscale=row(pool_scale), conv_w=conv_w[l], conv_b=row(conv_b),
        cln_g=row(conv_ln_g), cln_b=row(conv_ln_b), sln_g=row(sgu_ln_g), sln_b=row(sgu_ln_b),
        sgu_wcat=wcat, sgu_bias=bias, sgu_w4=w4, sgu_b4=bias[:n_steps], short_w=short_w[l],
        avg=avg, w_up=w_ffn_up[l].astype(BF16), w_down=w_ffn_down[l].astype(BF16))


def kernel(x_prompt, x_sample, state_pool, state_conv, state_short, norm_mix_pre, norm_mix_post, norm_ffn_pre, norm_ffn_post, w_in, w_out, w_pool, pool_scale, conv_w, conv_b, conv_ln_g, conv_ln_b, sgu_ln_g, sgu_ln_b, sgu_w, sgu_b, short_w, w_ffn_up, w_ffn_down):
    depth = w_in.shape[0]
    bp, seq, d = x_prompt.shape
    n_seq, n_steps, _ = x_sample.shape
    assert seq % PROMPT_TILE == 0 and PROMPT_TILE % ROW_BLOCK == 0
    assert n_steps <= CHUNK and PAST_LEN % CHUNK == 0

    yp = x_prompt
    ys = jnp.transpose(x_sample, (1, 0, 2)).reshape(n_steps * n_seq, d)
    outs = [[] for _ in range(7)]
    for l in range(depth):
        p = _layer_params(l, n_steps, norm_mix_pre, norm_mix_post, norm_ffn_pre, norm_ffn_post,
                          w_in, w_out, w_pool, pool_scale, conv_w, conv_b, conv_ln_g, conv_ln_b,
                          sgu_ln_g, sgu_ln_b, sgu_w, sgu_b, short_w, w_ffn_up, w_ffn_down)
        yp, pool_p, conv_p, short_p = _prompt_mixer(yp, p)
        yp = _ffn(yp.reshape(bp * seq, d), p, "prompt_ffn").reshape(bp, seq, d)

        ys, pool_s, conv_s, short_s, v_s = _sample_mixer(
            ys, jnp.transpose(state_pool[l], (1, 0, 2)), jnp.transpose(state_conv[l], (1, 0, 2)),
            jnp.transpose(state_short[l], (1, 0, 2)), p)
        ys = _ffn(ys, p, "sample_ffn")

        to_seq_major = lambda a: jnp.transpose(a, (1, 0, 2))
        for lst, val in zip(outs, (pool_p, to_seq_major(pool_s), conv_p, to_seq_major(conv_s),
                                   short_p, to_seq_major(short_s), to_seq_major(v_s))):
            lst.append(val)

    ys = jnp.transpose(ys.reshape(n_steps, n_seq, d), (1, 0, 2))
    return (yp, ys) + tuple(jnp.stack(o) for o in outs)
```

```python
import jax
import jax.numpy as jnp
from jax import lax
from jax.experimental import pallas as pl
from jax.experimental.pallas import tpu as pltpu

D_MODEL = 1024
GROUP_WIDTH = 256
N_SUB = 4
SUB_DIM = 64
POOL_WINDOWS = (2, 4, 8, 16)
POOL_BUF = 15
CONV_WIDTH = 31
SHORT_WIDTH = 3
CHUNK = 128
D_FF = 4096
EPS = 1e-6
PAST_LEN = 16384
IN_WIDTH = 8 * GROUP_WIDTH

SUBLANES = 8
LANES = 128
N_SLABS = GROUP_WIDTH // LANES
POOL_HIST = 16
CONV_HIST = 32
SHORT_HIST = 8

ROW_BLOCK = CHUNK
MATMUL_ROWS = 256
PROMPT_TILE = 1024
FFN_TILE = 512
FF_CHUNK = 512
VMEM_LIMIT_BYTES = 56 * 1024 * 1024

F32 = jnp.float32
BF16 = jnp.bfloat16


def _rms_norm(x, g):
    ms = jnp.mean(x * x, axis=-1, keepdims=True)
    return x * lax.rsqrt(ms + EPS) * g


def _dot(a, b):
    return jnp.dot(a, b, preferred_element_type=F32)


def _dot_split(x, m):
    hi = x.astype(BF16)
    lo = (x - hi.astype(F32)).astype(BF16)
    return _dot(hi, m) + _dot(lo, m)


def _head_layer_norm(x, avg, g, b):
    mu = _dot_split(x, avg)
    xc = x - mu
    var = _dot_split(xc * xc, avg)
    return xc * lax.rsqrt(var + EPS) * g + b


def _lane_group(shape):
    return jnp.right_shift(lax.broadcasted_iota(jnp.int32, shape, 1), SUB_DIM.bit_length() - 1)


def _pool_window(shape):
    grp = _lane_group(shape)
    w = jnp.full(shape, POOL_WINDOWS[0], jnp.int32)
    for gi in range(1, N_SUB):
        w = jnp.where(grp == gi, POOL_WINDOWS[gi], w)
    return grp, w


def _gating_rhs(vn):
    grp = _lane_group(vn.shape)
    return jnp.concatenate([jnp.where(grp == h, vn, 0.0).astype(BF16) for h in range(N_SUB)], axis=0)


def _cols(group, slab):
    lo = group * GROUP_WIDTH + slab * LANES
    return slice(lo, lo + LANES)


def _trailing_sum(ext_ref, slab, start, n_rows, width):
    acc = ext_ref[slab, start:start + n_rows, :]
    for k in range(1, width):
        acc = acc + ext_ref[slab, start - k:start - k + n_rows, :]
    return acc


def _causal_taps(ext_ref, slab, w_ref, hist, n_taps, r):
    first = hist + r - (n_taps - 1)
    lanes = slice(slab * LANES, (slab + 1) * LANES)
    acc = None
    for k in range(n_taps):
        term = ext_ref[slab, first + k:first + k + ROW_BLOCK, :] * w_ref[k:k + 1, lanes]
        acc = term if acc is None else acc + term
    return acc


def _mix_rows(r, j, tile, z_ref, y_ref, pool_ext, conv_ext, short_ext, wpool_ref, pscale_ref,
              convw_ref, convb_ref, clng_ref, clnb_ref, slng_ref, slnb_ref, swcat_ref, sbias_ref,
              shortw_ref, avg):
    rows = slice(r, r + ROW_BLOCK)
    gw = GROUP_WIDTH
    shape = (ROW_BLOCK, LANES)
    low_head = lax.broadcasted_iota(jnp.int32, shape, 1) < SUB_DIM

    d = []
    for s in range(N_SLABS):
        a = z_ref[rows, _cols(0, s)]
        base = POOL_HIST + r
        pool_ext[s, base:base + ROW_BLOCK, :] = a
        small, large = POOL_WINDOWS[2 * s], POOL_WINDOWS[2 * s + 1]
        assert large == 2 * small
        if small % SUBLANES == 0:
            run = _trailing_sum(pool_ext, s, base - small, ROW_BLOCK + small, small)
            s_small = run[small:]
            s_large = s_small + run[:ROW_BLOCK]
        else:
            s_small = _trailing_sum(pool_ext, s, base, ROW_BLOCK, small)
            s_large = s_small + _trailing_sum(pool_ext, s, base - small, ROW_BLOCK, small)
        win = jnp.where(low_head, s_small, s_large)
        if r < max(POOL_WINDOWS):
            pos1 = j * tile + r + 1 + lax.broadcasted_iota(jnp.int32, shape, 0)
            cnt = jnp.minimum(jnp.where(low_head, small, large), pos1).astype(F32)
            inv = 1.0 / cnt
        else:
            inv = jnp.where(low_head, 1.0 / small, 1.0 / large)
        d.append(win * inv - a)
    d = jnp.concatenate(d, axis=1).astype(BF16)
    y_ref[rows, 0:gw] = (_dot(d, wpool_ref[...]) * pscale_ref[...]).astype(BF16)

    c = []
    for s in range(N_SLABS):
        g = z_ref[rows, _cols(1, s)] * jax.nn.sigmoid(z_ref[rows, _cols(2, s)])
        conv_ext[s, CONV_HIST + r:CONV_HIST + r + ROW_BLOCK, :] = g
        c.append(_causal_taps(conv_ext, s, convw_ref, CONV_HIST, CONV_WIDTH, r))
    c = jnp.concatenate(c, axis=1) + convb_ref[...]
    c = _head_layer_norm(c, avg, clng_ref[...], clnb_ref[...])
    y_ref[rows, gw:2 * gw] = (c * jax.nn.sigmoid(c)).astype(BF16)

    vn = _head_layer_norm(z_ref[rows, 4 * gw:5 * gw], avg, slng_ref[...], slnb_ref[...])
    sg = _dot(swcat_ref[...], _gating_rhs(vn)) + sbias_ref[...]
    y_ref[rows, 2 * gw:3 * gw] = (z_ref[rows, 3 * gw:4 * gw] * sg).astype(BF16)

    for s in range(N_SLABS):
        ch = z_ref[rows, _cols(6, s)] * z_ref[rows, _cols(7, s)]
        short_ext[s, SHORT_HIST + r:SHORT_HIST + r + ROW_BLOCK, :] = ch
        sc = _causal_taps(short_ext, s, shortw_ref, SHORT_HIST, SHORT_WIDTH, r)
        y_ref[rows, _cols(3, s)] = (z_ref[rows, _cols(5, s)] * sc).astype(BF16)


def _prompt_mixer_kernel(x_ref, gpre_ref, gpost_ref, win_ref, wout_ref, wpool_ref, pscale_ref,
                         convw_ref, convb_ref, clng_ref, clnb_ref, slng_ref, slnb_ref,
                         swcat_ref, sbias_ref, shortw_ref, avg_ref,
                         out_ref, pool_out_ref, conv_out_ref, short_out_ref,
                         z_ref, y_ref, pool_ext, conv_ext, short_ext):
    j = pl.program_id(1)
    tile = x_ref.shape[0]

    @pl.when(j == 0)
    def _():
        pool_ext[:, 0:POOL_HIST, :] = jnp.zeros((N_SLABS, POOL_HIST, LANES), F32)
        conv_ext[:, 0:CONV_HIST, :] = jnp.zeros((N_SLABS, CONV_HIST, LANES), F32)
        short_ext[:, 0:SHORT_HIST, :] = jnp.zeros((N_SLABS, SHORT_HIST, LANES), F32)

    avg = avg_ref[...]
    for b in range(tile // MATMUL_ROWS):
        r0 = b * MATMUL_ROWS
        rows = slice(r0, r0 + MATMUL_ROWS)
        h = _rms_norm(x_ref[rows, :], gpre_ref[...]).astype(BF16)
        z_ref[rows, :] = _dot(h, win_ref[...])
        for r in range(r0, r0 + MATMUL_ROWS, ROW_BLOCK):
            _mix_rows(r, j, tile, z_ref, y_ref, pool_ext, conv_ext, short_ext, wpool_ref,
                      pscale_ref, convw_ref, convb_ref, clng_ref, clnb_ref, slng_ref, slnb_ref,
                      swcat_ref, sbias_ref, shortw_ref, avg)
        o = _dot(y_ref[rows, :], wout_ref[...])
        out_ref[rows, :] = x_ref[rows, :] + _rms_norm(o, gpost_ref[...])

    pool_ext[:, 0:POOL_HIST, :] = pool_ext[:, tile:tile + POOL_HIST, :]
    conv_ext[:, 0:CONV_HIST, :] = conv_ext[:, tile:tile + CONV_HIST, :]
    short_ext[:, 0:SHORT_HIST, :] = short_ext[:, tile:tile + SHORT_HIST, :]

    @pl.when(j == pl.num_programs(1) - 1)
    def _():
        for s in range(N_SLABS):
            lanes = slice(s * LANES, (s + 1) * LANES)
            pool_out_ref[:, lanes] = pool_ext[s, POOL_HIST - POOL_BUF:POOL_HIST, :]
            conv_out_ref[:, lanes] = conv_ext[s, CONV_HIST - (CONV_WIDTH - 1):CONV_HIST, :]
            short_out_ref[:, lanes] = short_ext[s, SHORT_HIST - (SHORT_WIDTH - 1):SHORT_HIST, :]


def _sample_mixer_kernel(x_ref, pool_in_ref, conv_in_ref, short_in_ref,
                         gpre_ref, gpost_ref, win_ref, wout_ref, wpool_ref, pscale_ref,
                         convw_ref, convb_ref, clng_ref, clnb_ref, slng_ref, slnb_ref,
                         sgw_ref, sgb_ref, shortw_ref, avg_ref,
                         out_ref, pool_out_ref, conv_out_ref, short_out_ref, v_out_ref,
                         z_ref, y_ref):
    n_steps, n_seq = v_out_ref.shape[0], v_out_ref.shape[1]
    gw = GROUP_WIDTH
    x = x_ref[...]
    h = _rms_norm(x, gpre_ref[...]).astype(BF16)
    z_ref[...] = _dot(h, win_ref[...])
    avg = avg_ref[...]
    grp, w = _pool_window((n_seq, gw))
    cnt = jnp.minimum(w, PAST_LEN + 1).astype(F32)

    def slab(t):
        return slice(t * n_seq, (t + 1) * n_seq)

    a_new = [z_ref[slab(t), 0:gw] for t in range(n_steps)]
    g_new = [z_ref[slab(t), gw:2 * gw] * jax.nn.sigmoid(z_ref[slab(t), 2 * gw:3 * gw])
             for t in range(n_steps)]
    ch_new = [z_ref[slab(t), 6 * gw:7 * gw] * z_ref[slab(t), 7 * gw:8 * gw]
              for t in range(n_steps)]

    def pool_row(i):
        return pool_in_ref[i] if i < POOL_BUF else a_new[i - POOL_BUF]

    def conv_row(i):
        return conv_in_ref[i] if i < CONV_WIDTH - 1 else g_new[i - (CONV_WIDTH - 1)]

    def short_row(i):
        return short_in_ref[i] if i < SHORT_WIDTH - 1 else ch_new[i - (SHORT_WIDTH - 1)]

    vn = []
    for t in range(n_steps):
        rows = slab(t)
        end = POOL_BUF + t
        acc = pool_row(end)
        sums = []
        for k in range(1, max(POOL_WINDOWS)):
            acc = acc + pool_row(end - k)
            if k + 1 in POOL_WINDOWS:
                sums.append(acc)
        win = sums[0]
        for gi in range(1, N_SUB):
            win = jnp.where(grp == gi, sums[gi], win)
        d = win / cnt - a_new[t]
        y_ref[rows, 0:gw] = (_dot(d.astype(BF16), wpool_ref[...]) * pscale_ref[...]).astype(BF16)

        c = None
        for k in range(CONV_WIDTH):
            term = conv_row(t + k) * convw_ref[k:k + 1, :]
            c = term if c is None else c + term
        c = _head_layer_norm(c + convb_ref[...], avg, clng_ref[...], clnb_ref[...])
        y_ref[rows, gw:2 * gw] = (c * jax.nn.sigmoid(c)).astype(BF16)

        vn.append(_head_layer_norm(z_ref[rows, 4 * gw:5 * gw], avg, slng_ref[...], slnb_ref[...]))
        v_out_ref[t] = vn[t]
        s = sgb_ref[t:t + 1, :]
        for u in range(t + 1):
            s = s + sgw_ref[t * n_steps + u:t * n_steps + u + 1, :] * vn[u]
        y_ref[rows, 2 * gw:3 * gw] = (z_ref[rows, 3 * gw:4 * gw] * s).astype(BF16)

        sc = None
        for k in range(SHORT_WIDTH):
            term = short_row(t + k) * shortw_ref[k:k + 1, :]
            sc = term if sc is None else sc + term
        y_ref[rows, 3 * gw:4 * gw] = (z_ref[rows, 5 * gw:6 * gw] * sc).astype(BF16)

    o = _dot(y_ref[...], wout_ref[...])
    out_ref[...] = x + _rms_norm(o, gpost_ref[...])

    for i in range(POOL_BUF):
        pool_out_ref[i] = pool_row(i + n_steps)
    for i in range(CONV_WIDTH - 1):
        conv_out_ref[i] = conv_row(i + n_steps)
    for i in range(SHORT_WIDTH - 1):
        short_out_ref[i] = short_row(i + n_steps)


def _ffn_kernel(x_ref, gpre_ref, gpost_ref, wup_ref, wdown_ref, out_ref, acc_ref):
    x = x_ref[...]
    f = _rms_norm(x, gpre_ref[...]).astype(BF16)
    for c in range(D_FF // FF_CHUNK):
        cols = slice(c * FF_CHUNK, (c + 1) * FF_CHUNK)
        u = jnp.maximum(_dot(f, wup_ref[:, cols]), 0.0)
        part = _dot((u * u).astype(BF16), wdown_ref[cols, :])
        if c == 0:
            acc_ref[...] = part
        else:
            acc_ref[...] += part
    out_ref[...] = x + _rms_norm(acc_ref[...], gpost_ref[...])


def _const_spec(shape):
    nd = len(shape)
    return pl.BlockSpec(shape, lambda *_: (0,) * nd, pipeline_mode=pl.Buffered(1))


def _whole_spec(shape):
    nd = len(shape)
    return pl.BlockSpec(shape, lambda *_: (0,) * nd)


def _compiler_params(semantics):
    return pltpu.CompilerParams(dimension_semantics=semantics, vmem_limit_bytes=VMEM_LIMIT_BYTES)


def _prompt_mixer(x, p):
    b, s, d = x.shape
    tile = PROMPT_TILE
    consts = (p["gpre"], p["gpost"], p["w_in"], p["w_out"], p["wpool"], p["pscale"],
              p["conv_w"], p["conv_b"], p["cln_g"], p["cln_b"], p["sln_g"], p["sln_b"],
              p["sgu_wcat"], p["sgu_bias"], p["short_w"], p["avg"])
    gw = GROUP_WIDTH
    state_spec = lambda n: pl.BlockSpec((None, n, gw), lambda bi, j: (bi, 0, 0))
    return pl.pallas_call(
        _prompt_mixer_kernel,
        grid=(b, s // tile),
        in_specs=[pl.BlockSpec((None, tile, d), lambda bi, j: (bi, j, 0))]
        + [_const_spec(c.shape) for c in consts],
        out_specs=[pl.BlockSpec((None, tile, d), lambda bi, j: (bi, j, 0)),
                   state_spec(POOL_BUF), state_spec(CONV_WIDTH - 1), state_spec(SHORT_WIDTH - 1)],
        out_shape=[jax.ShapeDtypeStruct((b, s, d), F32),
                   jax.ShapeDtypeStruct((b, POOL_BUF, gw), F32),
                   jax.ShapeDtypeStruct((b, CONV_WIDTH - 1, gw), F32),
                   jax.ShapeDtypeStruct((b, SHORT_WIDTH - 1, gw), F32)],
        scratch_shapes=[pltpu.VMEM((tile, IN_WIDTH), F32),
                        pltpu.VMEM((tile, d), BF16),
                        pltpu.VMEM((N_SLABS, POOL_HIST + tile, LANES), F32),
                        pltpu.VMEM((N_SLABS, CONV_HIST + tile, LANES), F32),
                        pltpu.VMEM((N_SLABS, SHORT_HIST + tile, LANES), F32)],
        compiler_params=_compiler_params(("arbitrary", "arbitrary")),
        name="prompt_mixer",
    )(x, *consts)


def _sample_mixer(x, pool_st, conv_st, short_st, p):
    n, d = x.shape
    n_seq = pool_st.shape[1]
    n_steps = n // n_seq
    gw = GROUP_WIDTH
    args = (x, pool_st, conv_st, short_st,
            p["gpre"], p["gpost"], p["w_in"], p["w_out"], p["wpool"], p["pscale"],
            p["conv_w"], p["conv_b"], p["cln_g"], p["cln_b"], p["sln_g"], p["sln_b"],
            p["sgu_w4"], p["sgu_b4"], p["short_w"], p["avg"])
    out_shape = [jax.ShapeDtypeStruct((n, d), F32),
                 jax.ShapeDtypeStruct(pool_st.shape, F32),
                 jax.ShapeDtypeStruct(conv_st.shape, F32),
                 jax.ShapeDtypeStruct(short_st.shape, F32),
                 jax.ShapeDtypeStruct((n_steps, n_seq, gw), F32)]
    return pl.pallas_call(
        _sample_mixer_kernel,
        grid=(1,),
        in_specs=[_whole_spec(a.shape) for a in args],
        out_specs=[_whole_spec(o.shape) for o in out_shape],
        out_shape=out_shape,
        scratch_shapes=[pltpu.VMEM((n, IN_WIDTH), F32), pltpu.VMEM((n, d), BF16)],
        compiler_params=_compiler_params(("arbitrary",)),
        name="sample_mixer",
    )(*args)


def _ffn(x, p, name):
    n, d = x.shape
    tile = min(FFN_TILE, n)
    consts = (p["fpre"], p["fpost"], p["w_up"], p["w_down"])
    return pl.pallas_call(
        _ffn_kernel,
        grid=(n // tile,),
        in_specs=[pl.BlockSpec((tile, d), lambda i: (i, 0))] + [_const_spec(c.shape) for c in consts],
        out_specs=pl.BlockSpec((tile, d), lambda i: (i, 0)),
        out_shape=jax.ShapeDtypeStruct((n, d), F32),
        scratch_shapes=[pltpu.VMEM((tile, d), F32)],
        compiler_params=_compiler_params(("arbitrary",)),
        name=name,
    )(x, *consts)


def _layer_params(l, n_steps, norm_mix_pre, norm_mix_post, norm_ffn_pre, norm_ffn_post, w_in, w_out,
                  w_pool, pool_scale, conv_w, conv_b, conv_ln_g, conv_ln_b, sgu_ln_g, sgu_ln_b,
                  sgu_w, sgu_b, short_w, w_ffn_up, w_ffn_down):
    row = lambda v: v[l].reshape(1, -1)
    gw = GROUP_WIDTH
    head_of_lane = jnp.arange(gw) // SUB_DIM
    same_head = head_of_lane[:, None] == head_of_lane[None, :]
    wp = jnp.tile(w_pool[l].reshape(gw, SUB_DIM), (1, N_SUB))
    wpool = jnp.where(same_head, wp, 0.0).astype(BF16)
    avg = jnp.where(same_head, 1.0 / SUB_DIM, 0.0).astype(BF16)
    tril = jnp.tril(jnp.ones((CHUNK, CHUNK), dtype=bool))
    w_causal = jnp.where(tril[None], sgu_w[l], 0.0)
    wcat = jnp.transpose(w_causal, (1, 0, 2)).reshape(CHUNK, N_SUB * CHUNK).astype(BF16)
    bias = jnp.repeat(jnp.transpose(sgu_b[l]), SUB_DIM, axis=1)
    w4 = jnp.repeat(jnp.transpose(w_causal[:, :n_steps, :n_steps], (1, 2, 0)), SUB_DIM, axis=2)
    w4 = w4.reshape(n_steps * n_steps, gw)
    return dict(
        gpre=row(norm_mix_pre), gpost=row(norm_mix_post), fpre=row(norm_ffn_pre),
        fpost=row(norm_ffn_post), w_in=w_in[l].astype(BF16), w_out=w_out[l].astype(BF16),
        wpool=wpool, pscale=row(pool_scale), conv_w=conv_w[l], conv_b=row(conv_b),
        cln_g=row(conv_ln_g), cln_b=row(conv_ln_b), sln_g=row(sgu_ln_g), sln_b=row(sgu_ln_b),
        sgu_wcat=wcat, sgu_bias=bias, sgu_w4=w4, sgu_b4=bias[:n_steps], short_w=short_w[l],
        avg=avg, w_up=w_ffn_up[l].astype(BF16), w_down=w_ffn_down[l].astype(BF16))


def kernel(x_prompt, x_sample, state_pool, state_conv, state_short, norm_mix_pre, norm_mix_post, norm_ffn_pre, norm_ffn_post, w_in, w_out, w_pool, pool_scale, conv_w, conv_b, conv_ln_g, conv_ln_b, sgu_ln_g, sgu_ln_b, sgu_w, sgu_b, short_w, w_ffn_up, w_ffn_down):
    depth = w_in.shape[0]
    bp, seq, d = x_prompt.shape
    n_seq, n_steps, _ = x_sample.shape
    assert seq % PROMPT_TILE == 0 and PROMPT_TILE % MATMUL_ROWS == 0 and MATMUL_ROWS % ROW_BLOCK == 0
    assert n_steps <= CHUNK and PAST_LEN % CHUNK == 0

    yp = x_prompt
    ys = jnp.transpose(x_sample, (1, 0, 2)).reshape(n_steps * n_seq, d)
    outs = [[] for _ in range(7)]
    for l in range(depth):
        p = _layer_params(l, n_steps, norm_mix_pre, norm_mix_post, norm_ffn_pre, norm_ffn_post,
                          w_in, w_out, w_pool, pool_scale, conv_w, conv_b, conv_ln_g, conv_ln_b,
                          sgu_ln_g, sgu_ln_b, sgu_w, sgu_b, short_w, w_ffn_up, w_ffn_down)
        yp, pool_p, conv_p, short_p = _prompt_mixer(yp, p)
        yp = _ffn(yp.reshape(bp * seq, d), p, "prompt_ffn").reshape(bp, seq, d)

        ys, pool_s, conv_s, short_s, v_s = _sample_mixer(
            ys, jnp.transpose(state_pool[l], (1, 0, 2)), jnp.transpose(state_conv[l], (1, 0, 2)),
            jnp.transpose(state_short[l], (1, 0, 2)), p)
        ys = _ffn(ys, p, "sample_ffn")

        to_seq_major = lambda a: jnp.transpose(a, (1, 0, 2))
        for lst, val in zip(outs, (pool_p, to_seq_major(pool_s), conv_p, to_seq_major(conv_s),
                                   short_p, to_seq_major(short_s), to_seq_major(v_s))):
            lst.append(val)

    ys = jnp.transpose(ys.reshape(n_steps, n_seq, d), (1, 0, 2))
    return (yp, ys) + tuple(jnp.stack(o) for o in outs)
```

```python
import jax
import jax.numpy as jnp
from jax import lax
from jax.experimental import pallas as pl
from jax.experimental.pallas import tpu as pltpu

D_MODEL = 1024
GROUP_WIDTH = 256
N_SUB = 4
SUB_DIM = 64
POOL_WINDOWS = (2, 4, 8, 16)
POOL_BUF = 15
CONV_WIDTH = 31
SHORT_WIDTH = 3
CHUNK = 128
D_FF = 4096
EPS = 1e-6
PAST_LEN = 16384
IN_WIDTH = 8 * GROUP_WIDTH

SUBLANES = 8
LANES = 128
N_SLABS = GROUP_WIDTH // LANES
POOL_HIST = 16
CONV_HIST = 32
SHORT_HIST = 8

ROW_BLOCK = CHUNK
MATMUL_ROWS = 256
PROMPT_TILE = 1024
FFN_TILE = 512
FF_CHUNK = 512
VMEM_LIMIT_BYTES = 56 * 1024 * 1024

F32 = jnp.float32
BF16 = jnp.bfloat16


def _rms_norm(x, g):
    ms = jnp.mean(x * x, axis=-1, keepdims=True)
    return x * lax.rsqrt(ms + EPS) * g


def _dot(a, b):
    return jnp.dot(a, b, preferred_element_type=F32)


def _head_mean(x, low_head):
    s_low = jnp.sum(jnp.where(low_head, x, 0.0), axis=-1, keepdims=True)
    s_high = jnp.sum(jnp.where(low_head, 0.0, x), axis=-1, keepdims=True)
    return jnp.where(low_head, s_low, s_high) * (1.0 / SUB_DIM)


def _head_layer_norm(x, g, b):
    low_head = lax.broadcasted_iota(jnp.int32, (x.shape[0], LANES), 1) < SUB_DIM
    out = []
    for s in range(N_SLABS):
        lanes = slice(s * LANES, (s + 1) * LANES)
        xs = x[:, lanes]
        xc = xs - _head_mean(xs, low_head)
        var = _head_mean(xc * xc, low_head)
        out.append(xc * lax.rsqrt(var + EPS) * g[:, lanes] + b[:, lanes])
    return jnp.concatenate(out, axis=1)


def _lane_group(shape):
    return jnp.right_shift(lax.broadcasted_iota(jnp.int32, shape, 1), SUB_DIM.bit_length() - 1)


def _pool_window(shape):
    grp = _lane_group(shape)
    w = jnp.full(shape, POOL_WINDOWS[0], jnp.int32)
    for gi in range(1, N_SUB):
        w = jnp.where(grp == gi, POOL_WINDOWS[gi], w)
    return grp, w


def _gating_rhs(vn):
    grp = _lane_group(vn.shape)
    return jnp.concatenate([jnp.where(grp == h, vn, 0.0).astype(BF16) for h in range(N_SUB)], axis=0)


def _cols(group, slab):
    lo = group * GROUP_WIDTH + slab * LANES
    return slice(lo, lo + LANES)


def _trailing_sum(ext_ref, slab, start, n_rows, width):
    acc = ext_ref[slab, start:start + n_rows, :]
    for k in range(1, width):
        acc = acc + ext_ref[slab, start - k:start - k + n_rows, :]
    return acc


def _causal_taps(ext_ref, slab, w_ref, hist, n_taps, r):
    first = hist + r - (n_taps - 1)
    lanes = slice(slab * LANES, (slab + 1) * LANES)
    acc = None
    for k in range(n_taps):
        term = ext_ref[slab, first + k:first + k + ROW_BLOCK, :] * w_ref[k:k + 1, lanes]
        acc = term if acc is None else acc + term
    return acc


def _mix_rows(r, seq_start, z_ref, y_ref, pool_ext, conv_ext, short_ext, wpool_ref, pscale_ref,
              pinv_first_ref, pinv_rest_ref, convw_ref, convb_ref, clng_ref, clnb_ref, slng_ref,
              slnb_ref, swcat_ref, sbias_ref, shortw_ref):
    rows = slice(r, r + ROW_BLOCK)
    gw = GROUP_WIDTH
    low_head = lax.broadcasted_iota(jnp.int32, (ROW_BLOCK, LANES), 1) < SUB_DIM

    d = []
    for s in range(N_SLABS):
        lanes = slice(s * LANES, (s + 1) * LANES)
        a = z_ref[rows, _cols(0, s)]
        base = POOL_HIST + r
        pool_ext[s, base:base + ROW_BLOCK, :] = a
        small, large = POOL_WINDOWS[2 * s], POOL_WINDOWS[2 * s + 1]
        assert large == 2 * small
        if small % SUBLANES == 0:
            run = _trailing_sum(pool_ext, s, base - small, ROW_BLOCK + small, small)
            s_small = run[small:]
            s_large = s_small + run[:ROW_BLOCK]
        else:
            s_small = _trailing_sum(pool_ext, s, base, ROW_BLOCK, small)
            s_large = s_small + _trailing_sum(pool_ext, s, base - small, ROW_BLOCK, small)
        win = jnp.where(low_head, s_small, s_large)
        inv = pinv_rest_ref[:, lanes]
        if seq_start is not None:
            inv = jnp.where(seq_start, pinv_first_ref[:, lanes], inv)
        d.append(win * inv - a)
    d = jnp.concatenate(d, axis=1).astype(BF16)
    y_ref[rows, 0:gw] = (_dot(d, wpool_ref[...]) * pscale_ref[...]).astype(BF16)

    c = []
    for s in range(N_SLABS):
        g = z_ref[rows, _cols(1, s)] * jax.nn.sigmoid(z_ref[rows, _cols(2, s)])
        conv_ext[s, CONV_HIST + r:CONV_HIST + r + ROW_BLOCK, :] = g
        c.append(_causal_taps(conv_ext, s, convw_ref, CONV_HIST, CONV_WIDTH, r))
    c = jnp.concatenate(c, axis=1) + convb_ref[...]
    c = _head_layer_norm(c, clng_ref[...], clnb_ref[...])
    y_ref[rows, gw:2 * gw] = (c * jax.nn.sigmoid(c)).astype(BF16)

    vn = _head_layer_norm(z_ref[rows, 4 * gw:5 * gw], slng_ref[...], slnb_ref[...])
    sg = _dot(swcat_ref[...], _gating_rhs(vn)) + sbias_ref[...]
    y_ref[rows, 2 * gw:3 * gw] = (z_ref[rows, 3 * gw:4 * gw] * sg).astype(BF16)

    for s in range(N_SLABS):
        ch = z_ref[rows, _cols(6, s)] * z_ref[rows, _cols(7, s)]
        short_ext[s, SHORT_HIST + r:SHORT_HIST + r + ROW_BLOCK, :] = ch
        sc = _causal_taps(short_ext, s, shortw_ref, SHORT_HIST, SHORT_WIDTH, r)
        y_ref[rows, _cols(3, s)] = (z_ref[rows, _cols(5, s)] * sc).astype(BF16)


def _prompt_mixer_kernel(x_ref, gpre_ref, gpost_ref, win_ref, wout_ref, wpool_ref, pscale_ref,
                         pinv_first_ref, pinv_rest_ref, convw_ref, convb_ref, clng_ref, clnb_ref,
                         slng_ref, slnb_ref, swcat_ref, sbias_ref, shortw_ref,
                         out_ref, pool_out_ref, conv_out_ref, short_out_ref,
                         z_ref, y_ref, pool_ext, conv_ext, short_ext):
    j = pl.program_id(1)
    tile = x_ref.shape[0]

    @pl.when(j == 0)
    def _():
        pool_ext[:, 0:POOL_HIST, :] = jnp.zeros((N_SLABS, POOL_HIST, LANES), F32)
        conv_ext[:, 0:CONV_HIST, :] = jnp.zeros((N_SLABS, CONV_HIST, LANES), F32)
        short_ext[:, 0:SHORT_HIST, :] = jnp.zeros((N_SLABS, SHORT_HIST, LANES), F32)

    for r0 in range(0, tile, MATMUL_ROWS):
        rows = slice(r0, r0 + MATMUL_ROWS)
        h = _rms_norm(x_ref[rows, :], gpre_ref[...]).astype(BF16)
        z_ref[rows, :] = _dot(h, win_ref[...])
        for r in range(r0, r0 + MATMUL_ROWS, ROW_BLOCK):
            _mix_rows(r, (j == 0) if r == 0 else None, z_ref, y_ref, pool_ext, conv_ext,
                      short_ext, wpool_ref, pscale_ref, pinv_first_ref, pinv_rest_ref, convw_ref,
                      convb_ref, clng_ref, clnb_ref, slng_ref, slnb_ref, swcat_ref, sbias_ref,
                      shortw_ref)
        o = _dot(y_ref[rows, :], wout_ref[...])
        out_ref[rows, :] = x_ref[rows, :] + _rms_norm(o, gpost_ref[...])

    pool_ext[:, 0:POOL_HIST, :] = pool_ext[:, tile:tile + POOL_HIST, :]
    conv_ext[:, 0:CONV_HIST, :] = conv_ext[:, tile:tile + CONV_HIST, :]
    short_ext[:, 0:SHORT_HIST, :] = short_ext[:, tile:tile + SHORT_HIST, :]

    @pl.when(j == pl.num_programs(1) - 1)
    def _():
        for s in range(N_SLABS):
            lanes = slice(s * LANES, (s + 1) * LANES)
            pool_out_ref[:, lanes] = pool_ext[s, POOL_HIST - POOL_BUF:POOL_HIST, :]
            conv_out_ref[:, lanes] = conv_ext[s, CONV_HIST - (CONV_WIDTH - 1):CONV_HIST, :]
            short_out_ref[:, lanes] = short_ext[s, SHORT_HIST - (SHORT_WIDTH - 1):SHORT_HIST, :]


def _sample_mixer_kernel(x_ref, pool_in_ref, conv_in_ref, short_in_ref,
                         gpre_ref, gpost_ref, win_ref, wout_ref, wpool_ref, pscale_ref,
                         convw_ref, convb_ref, clng_ref, clnb_ref, slng_ref, slnb_ref,
                         sgw_ref, sgb_ref, shortw_ref,
                         out_ref, pool_out_ref, conv_out_ref, short_out_ref, v_out_ref,
                         z_ref, y_ref):
    n_steps, n_seq = v_out_ref.shape[0], v_out_ref.shape[1]
    gw = GROUP_WIDTH
    x = x_ref[...]
    h = _rms_norm(x, gpre_ref[...]).astype(BF16)
    z_ref[...] = _dot(h, win_ref[...])
    grp, w = _pool_window((n_seq, gw))
    cnt = jnp.minimum(w, PAST_LEN + 1).astype(F32)

    def slab(t):
        return slice(t * n_seq, (t + 1) * n_seq)

    a_new = [z_ref[slab(t), 0:gw] for t in range(n_steps)]
    g_new = [z_ref[slab(t), gw:2 * gw] * jax.nn.sigmoid(z_ref[slab(t), 2 * gw:3 * gw])
             for t in range(n_steps)]
    ch_new = [z_ref[slab(t), 6 * gw:7 * gw] * z_ref[slab(t), 7 * gw:8 * gw]
              for t in range(n_steps)]

    def pool_row(i):
        return pool_in_ref[i] if i < POOL_BUF else a_new[i - POOL_BUF]

    def conv_row(i):
        return conv_in_ref[i] if i < CONV_WIDTH - 1 else g_new[i - (CONV_WIDTH - 1)]

    def short_row(i):
        return short_in_ref[i] if i < SHORT_WIDTH - 1 else ch_new[i - (SHORT_WIDTH - 1)]

    vn = []
    for t in range(n_steps):
        rows = slab(t)
        end = POOL_BUF + t
        acc = pool_row(end)
        sums = []
        for k in range(1, max(POOL_WINDOWS)):
            acc = acc + pool_row(end - k)
            if k + 1 in POOL_WINDOWS:
                sums.append(acc)
        win = sums[0]
        for gi in range(1, N_SUB):
            win = jnp.where(grp == gi, sums[gi], win)
        d = win / cnt - a_new[t]
        y_ref[rows, 0:gw] = (_dot(d.astype(BF16), wpool_ref[...]) * pscale_ref[...]).astype(BF16)

        c = None
        for k in range(CONV_WIDTH):
            term = conv_row(t + k) * convw_ref[k:k + 1, :]
            c = term if c is None else c + term
        c = _head_layer_norm(c + convb_ref[...], clng_ref[...], clnb_ref[...])
        y_ref[rows, gw:2 * gw] = (c * jax.nn.sigmoid(c)).astype(BF16)

        vn.append(_head_layer_norm(z_ref[rows, 4 * gw:5 * gw], slng_ref[...], slnb_ref[...]))
        v_out_ref[t] = vn[t]
        s = sgb_ref[t:t + 1, :]
        for u in range(t + 1):
            s = s + sgw_ref[t * n_steps + u:t * n_steps + u + 1, :] * vn[u]
        y_ref[rows, 2 * gw:3 * gw] = (z_ref[rows, 3 * gw:4 * gw] * s).astype(BF16)

        sc = None
        for k in range(SHORT_WIDTH):
            term = short_row(t + k) * shortw_ref[k:k + 1, :]
            sc = term if sc is None else sc + term
        y_ref[rows, 3 * gw:4 * gw] = (z_ref[rows, 5 * gw:6 * gw] * sc).astype(BF16)

    o = _dot(y_ref[...], wout_ref[...])
    out_ref[...] = x + _rms_norm(o, gpost_ref[...])

    for i in range(POOL_BUF):
        pool_out_ref[i] = pool_row(i + n_steps)
    for i in range(CONV_WIDTH - 1):
        conv_out_ref[i] = conv_row(i + n_steps)
    for i in range(SHORT_WIDTH - 1):
        short_out_ref[i] = short_row(i + n_steps)


def _ffn_kernel(x_ref, gpre_ref, gpost_ref, wup_ref, wdown_ref, out_ref, acc_ref):
    x = x_ref[...]
    f = _rms_norm(x, gpre_ref[...]).astype(BF16)
    for c in range(D_FF // FF_CHUNK):
        cols = slice(c * FF_CHUNK, (c + 1) * FF_CHUNK)
        u = jnp.maximum(_dot(f, wup_ref[:, cols]), 0.0)
        part = _dot((u * u).astype(BF16), wdown_ref[cols, :])
        if c == 0:
            acc_ref[...] = part
        else:
            acc_ref[...] += part
    out_ref[...] = x + _rms_norm(acc_ref[...], gpost_ref[...])


def _const_spec(shape):
    nd = len(shape)
    return pl.BlockSpec(shape, lambda *_: (0,) * nd, pipeline_mode=pl.Buffered(1))


def _whole_spec(shape):
    nd = len(shape)
    return pl.BlockSpec(shape, lambda *_: (0,) * nd)


def _compiler_params(semantics):
    return pltpu.CompilerParams(dimension_semantics=semantics, vmem_limit_bytes=VMEM_LIMIT_BYTES)


def _prompt_mixer(x, p):
    b, s, d = x.shape
    tile = PROMPT_TILE
    consts = (p["gpre"], p["gpost"], p["w_in"], p["w_out"], p["wpool"], p["pscale"],
              p["pinv_first"], p["pinv_rest"], p["conv_w"], p["conv_b"], p["cln_g"], p["cln_b"],
              p["sln_g"], p["sln_b"], p["sgu_wcat"], p["sgu_bias"], p["short_w"])
    gw = GROUP_WIDTH
    state_spec = lambda n: pl.BlockSpec((None, n, gw), lambda bi, j: (bi, 0, 0))
    return pl.pallas_call(
        _prompt_mixer_kernel,
        grid=(b, s // tile),
        in_specs=[pl.BlockSpec((None, tile, d), lambda bi, j: (bi, j, 0))]
        + [_const_spec(c.shape) for c in consts],
        out_specs=[pl.BlockSpec((None, tile, d), lambda bi, j: (bi, j, 0)),
                   state_spec(POOL_BUF), state_spec(CONV_WIDTH - 1), state_spec(SHORT_WIDTH - 1)],
        out_shape=[jax.ShapeDtypeStruct((b, s, d), F32),
                   jax.ShapeDtypeStruct((b, POOL_BUF, gw), F32),
                   jax.ShapeDtypeStruct((b, CONV_WIDTH - 1, gw), F32),
                   jax.ShapeDtypeStruct((b, SHORT_WIDTH - 1, gw), F32)],
        scratch_shapes=[pltpu.VMEM((tile, IN_WIDTH), F32),
                        pltpu.VMEM((tile, d), BF16),
                        pltpu.VMEM((N_SLABS, POOL_HIST + tile, LANES), F32),
                        pltpu.VMEM((N_SLABS, CONV_HIST + tile, LANES), F32),
                        pltpu.VMEM((N_SLABS, SHORT_HIST + tile, LANES), F32)],
        compiler_params=_compiler_params(("arbitrary", "arbitrary")),
        name="prompt_mixer",
    )(x, *consts)


def _sample_mixer(x, pool_st, conv_st, short_st, p):
    n, d = x.shape
    n_seq = pool_st.shape[1]
    n_steps = n // n_seq
    gw = GROUP_WIDTH
    args = (x, pool_st, conv_st, short_st,
            p["gpre"], p["gpost"], p["w_in"], p["w_out"], p["wpool"], p["pscale"],
            p["conv_w"], p["conv_b"], p["cln_g"], p["cln_b"], p["sln_g"], p["sln_b"],
            p["sgu_w4"], p["sgu_b4"], p["short_w"])
    out_shape = [jax.ShapeDtypeStruct((n, d), F32),
                 jax.ShapeDtypeStruct(pool_st.shape, F32),
                 jax.ShapeDtypeStruct(conv_st.shape, F32),
                 jax.ShapeDtypeStruct(short_st.shape, F32),
                 jax.ShapeDtypeStruct((n_steps, n_seq, gw), F32)]
    return pl.pallas_call(
        _sample_mixer_kernel,
        grid=(1,),
        in_specs=[_whole_spec(a.shape) for a in args],
        out_specs=[_whole_spec(o.shape) for o in out_shape],
        out_shape=out_shape,
        scratch_shapes=[pltpu.VMEM((n, IN_WIDTH), F32), pltpu.VMEM((n, d), BF16)],
        compiler_params=_compiler_params(("arbitrary",)),
        name="sample_mixer",
    )(*args)


def _ffn(x, p, name):
    n, d = x.shape
    tile = min(FFN_TILE, n)
    consts = (p["fpre"], p["fpost"], p["w_up"], p["w_down"])
    return pl.pallas_call(
        _ffn_kernel,
        grid=(n // tile,),
        in_specs=[pl.BlockSpec((tile, d), lambda i: (i, 0))] + [_const_spec(c.shape) for c in consts],
        out_specs=pl.BlockSpec((tile, d), lambda i: (i, 0)),
        out_shape=jax.ShapeDtypeStruct((n, d), F32),
        scratch_shapes=[pltpu.VMEM((tile, d), F32)],
        compiler_params=_compiler_params(("arbitrary",)),
        name=name,
    )(x, *consts)


def _layer_params(l, n_steps, norm_mix_pre, norm_mix_post, norm_ffn_pre, norm_ffn_post, w_in, w_out,
                  w_pool, pool_scale, conv_w, conv_b, conv_ln_g, conv_ln_b, sgu_ln_g, sgu_ln_b,
                  sgu_w, sgu_b, short_w, w_ffn_up, w_ffn_down):
    row = lambda v: v[l].reshape(1, -1)
    gw = GROUP_WIDTH
    head_of_lane = jnp.arange(gw) // SUB_DIM
    same_head = head_of_lane[:, None] == head_of_lane[None, :]
    wp = jnp.tile(w_pool[l].reshape(gw, SUB_DIM), (1, N_SUB))
    wpool = jnp.where(same_head, wp, 0.0).astype(BF16)
    tril = jnp.tril(jnp.ones((CHUNK, CHUNK), dtype=bool))
    w_causal = jnp.where(tril[None], sgu_w[l], 0.0)
    wcat = jnp.transpose(w_causal, (1, 0, 2)).reshape(CHUNK, N_SUB * CHUNK).astype(BF16)
    bias = jnp.repeat(jnp.transpose(sgu_b[l]), SUB_DIM, axis=1)
    w4 = jnp.repeat(jnp.transpose(w_causal[:, :n_steps, :n_steps], (1, 2, 0)), SUB_DIM, axis=2)
    w4 = w4.reshape(n_steps * n_steps, gw)
    window = jnp.repeat(jnp.asarray(POOL_WINDOWS, F32), SUB_DIM)[None, :]
    pinv_first = 1.0 / jnp.minimum(window, jnp.arange(1, ROW_BLOCK + 1, dtype=F32)[:, None])
    return dict(
        pinv_first=pinv_first, pinv_rest=1.0 / window,
        gpre=row(norm_mix_pre), gpost=row(norm_mix_post), fpre=row(norm_ffn_pre),
        fpost=row(norm_ffn_post), w_in=w_in[l].astype(BF16), w_out=w_out[l].astype(BF16),
        wpool=wpool, pscale=row(pool_scale), conv_w=conv_w[l], conv_b=row(conv_b),
        cln_g=row(conv_ln_g), cln_b=row(conv_ln_b), sln_g=row(sgu_ln_g), sln_b=row(sgu_ln_b),
        sgu_wcat=wcat, sgu_bias=bias, sgu_w4=w4, sgu_b4=bias[:n_steps], short_w=short_w[l],
        w_up=w_ffn_up[l].astype(BF16), w_down=w_ffn_down[l].astype(BF16))


def kernel(x_prompt, x_sample, state_pool, state_conv, state_short, norm_mix_pre, norm_mix_post, norm_ffn_pre, norm_ffn_post, w_in, w_out, w_pool, pool_scale, conv_w, conv_b, conv_ln_g, conv_ln_b, sgu_ln_g, sgu_ln_b, sgu_w, sgu_b, short_w, w_ffn_up, w_ffn_down):
    depth = w_in.shape[0]
    bp, seq, d = x_prompt.shape
    n_seq, n_steps, _ = x_sample.shape
    assert seq % PROMPT_TILE == 0 and PROMPT_TILE % MATMUL_ROWS == 0 and MATMUL_ROWS % ROW_BLOCK == 0
    assert ROW_BLOCK >= max(POOL_WINDOWS) and n_steps <= CHUNK and PAST_LEN % CHUNK == 0

    yp = x_prompt
    ys = jnp.transpose(x_sample, (1, 0, 2)).reshape(n_steps * n_seq, d)
    outs = [[] for _ in range(7)]
    for l in range(depth):
        p = _layer_params(l, n_steps, norm_mix_pre, norm_mix_post, norm_ffn_pre, norm_ffn_post,
                          w_in, w_out, w_pool, pool_scale, conv_w, conv_b, conv_ln_g, conv_ln_b,
                          sgu_ln_g, sgu_ln_b, sgu_w, sgu_b, short_w, w_ffn_up, w_ffn_down)
        yp, pool_p, conv_p, short_p = _prompt_mixer(yp, p)
        yp = _ffn(yp.reshape(bp * seq, d), p, "prompt_ffn").reshape(bp, seq, d)

        ys, pool_s, conv_s, short_s, v_s = _sample_mixer(
            ys, jnp.transpose(state_pool[l], (1, 0, 2)), jnp.transpose(state_conv[l], (1, 0, 2)),
            jnp.transpose(state_short[l], (1, 0, 2)), p)
        ys = _ffn(ys, p, "sample_ffn")

        to_seq_major = lambda a: jnp.transpose(a, (1, 0, 2))
        for lst, val in zip(outs, (pool_p, to_seq_major(pool_s), conv_p, to_seq_major(conv_s),
                                   short_p, to_seq_major(short_s), to_seq_major(v_s))):
            lst.append(val)

    ys = jnp.transpose(ys.reshape(n_steps, n_seq, d), (1, 0, 2))
    return (yp, ys) + tuple(jnp.stack(o) for o in outs)
```

```python
import functools

import jax
import jax.numpy as jnp
from jax import lax
from jax.experimental import pallas as pl
from jax.experimental.pallas import tpu as pltpu

D_MODEL = 1024
GROUP_WIDTH = 256
N_SUB = 4
SUB_DIM = 64
POOL_WINDOWS = (2, 4, 8, 16)
POOL_BUF = 15
CONV_WIDTH = 31
SHORT_WIDTH = 3
CHUNK = 128
D_FF = 4096
EPS = 1e-6
PAST_LEN = 16384
IN_WIDTH = 8 * GROUP_WIDTH

SUBLANES = 8
LANES = 128
N_SLABS = GROUP_WIDTH // LANES
POOL_HIST = 16
CONV_HIST = 32
SHORT_HIST = 8

ROW_BLOCK = CHUNK
MATMUL_ROWS = 256
PROMPT_TILE = 1024
FFN_TILE = 1024
FFN_ROWS = 512
FF_CHUNK = 512
VMEM_LIMIT_BYTES = 56 * 1024 * 1024

F32 = jnp.float32
BF16 = jnp.bfloat16


def _rms_norm(x, g):
    ms = jnp.mean(x * x, axis=-1, keepdims=True)
    return x * lax.rsqrt(ms + EPS) * g


def _dot(a, b):
    return jnp.dot(a, b, preferred_element_type=F32)


def _head_mean(x, low_head):
    s_low = jnp.sum(jnp.where(low_head, x, 0.0), axis=-1, keepdims=True)
    s_high = jnp.sum(jnp.where(low_head, 0.0, x), axis=-1, keepdims=True)
    return jnp.where(low_head, s_low, s_high) * (1.0 / SUB_DIM)


def _head_layer_norm(x, g, b):
    low_head = lax.broadcasted_iota(jnp.int32, (x.shape[0], LANES), 1) < SUB_DIM
    out = []
    for s in range(N_SLABS):
        lanes = slice(s * LANES, (s + 1) * LANES)
        xs = x[:, lanes]
        xc = xs - _head_mean(xs, low_head)
        var = _head_mean(xc * xc, low_head)
        out.append(xc * lax.rsqrt(var + EPS) * g[:, lanes] + b[:, lanes])
    return jnp.concatenate(out, axis=1)


def _lane_group(shape):
    return jnp.right_shift(lax.broadcasted_iota(jnp.int32, shape, 1), SUB_DIM.bit_length() - 1)


def _pool_window(shape):
    grp = _lane_group(shape)
    w = jnp.full(shape, POOL_WINDOWS[0], jnp.int32)
    for gi in range(1, N_SUB):
        w = jnp.where(grp == gi, POOL_WINDOWS[gi], w)
    return grp, w


def _gating_rhs(vn):
    grp = _lane_group(vn.shape)
    return jnp.concatenate([jnp.where(grp == h, vn, 0.0).astype(BF16) for h in range(N_SUB)], axis=0)


def _cols(group, slab):
    lo = group * GROUP_WIDTH + slab * LANES
    return slice(lo, lo + LANES)


def _trailing_sum(ext_ref, slab, start, n_rows, width):
    acc = ext_ref[slab, start:start + n_rows, :]
    for k in range(1, width):
        acc = acc + ext_ref[slab, start - k:start - k + n_rows, :]
    return acc


def _causal_taps(ext_ref, slab, w_ref, hist, n_taps, r):
    first = hist + r - (n_taps - 1)
    lanes = slice(slab * LANES, (slab + 1) * LANES)
    acc = None
    for k in range(n_taps):
        term = ext_ref[slab, first + k:first + k + ROW_BLOCK, :] * w_ref[k:k + 1, lanes]
        acc = term if acc is None else acc + term
    return acc


def _gate_rows(r, z_ref, y_ref, vn_ref, swcat_ref, sbias_ref):
    rows = slice(r, r + ROW_BLOCK)
    gw = GROUP_WIDTH
    sg = _dot(swcat_ref[...], _gating_rhs(vn_ref[rows, :])) + sbias_ref[...]
    y_ref[rows, 2 * gw:3 * gw] = (z_ref[rows, 3 * gw:4 * gw] * sg).astype(BF16)


def _mix_rows(r, seq_start, z_ref, y_ref, vn_ref, pool_ext, conv_ext, short_ext, wpool_ref,
              pscale_ref, pinv_first_ref, pinv_rest_ref, convw_ref, convb_ref, clng_ref, clnb_ref,
              slng_ref, slnb_ref, shortw_ref):
    rows = slice(r, r + ROW_BLOCK)
    gw = GROUP_WIDTH
    low_head = lax.broadcasted_iota(jnp.int32, (ROW_BLOCK, LANES), 1) < SUB_DIM

    d = []
    for s in range(N_SLABS):
        lanes = slice(s * LANES, (s + 1) * LANES)
        a = z_ref[rows, _cols(0, s)]
        base = POOL_HIST + r
        pool_ext[s, base:base + ROW_BLOCK, :] = a
        small, large = POOL_WINDOWS[2 * s], POOL_WINDOWS[2 * s + 1]
        assert large == 2 * small
        if small % SUBLANES == 0:
            run = _trailing_sum(pool_ext, s, base - small, ROW_BLOCK + small, small)
            s_small = run[small:]
            s_large = s_small + run[:ROW_BLOCK]
        else:
            s_small = _trailing_sum(pool_ext, s, base, ROW_BLOCK, small)
            s_large = s_small + _trailing_sum(pool_ext, s, base - small, ROW_BLOCK, small)
        win = jnp.where(low_head, s_small, s_large)
        inv = pinv_rest_ref[:, lanes]
        if seq_start is not None:
            inv = jnp.where(seq_start, pinv_first_ref[:, lanes], inv)
        d.append(win * inv - a)
    d = jnp.concatenate(d, axis=1).astype(BF16)
    y_ref[rows, 0:gw] = (_dot(d, wpool_ref[...]) * pscale_ref[...]).astype(BF16)

    c = []
    for s in range(N_SLABS):
        g = z_ref[rows, _cols(1, s)] * jax.nn.sigmoid(z_ref[rows, _cols(2, s)])
        conv_ext[s, CONV_HIST + r:CONV_HIST + r + ROW_BLOCK, :] = g
        c.append(_causal_taps(conv_ext, s, convw_ref, CONV_HIST, CONV_WIDTH, r))
    c = jnp.concatenate(c, axis=1) + convb_ref[...]
    c = _head_layer_norm(c, clng_ref[...], clnb_ref[...])
    y_ref[rows, gw:2 * gw] = (c * jax.nn.sigmoid(c)).astype(BF16)

    vn_ref[rows, :] = _head_layer_norm(z_ref[rows, 4 * gw:5 * gw], slng_ref[...], slnb_ref[...])

    for s in range(N_SLABS):
        ch = z_ref[rows, _cols(6, s)] * z_ref[rows, _cols(7, s)]
        short_ext[s, SHORT_HIST + r:SHORT_HIST + r + ROW_BLOCK, :] = ch
        sc = _causal_taps(short_ext, s, shortw_ref, SHORT_HIST, SHORT_WIDTH, r)
        y_ref[rows, _cols(3, s)] = (z_ref[rows, _cols(5, s)] * sc).astype(BF16)


def _prompt_mixer_kernel(x_ref, gpre_ref, gpost_ref, win_ref, wout_ref, wpool_ref, pscale_ref,
                         pinv_first_ref, pinv_rest_ref, convw_ref, convb_ref, clng_ref, clnb_ref,
                         slng_ref, slnb_ref, swcat_ref, sbias_ref, shortw_ref, wup_f32_ref, wdown_f32_ref,
                         out_ref, pool_out_ref, conv_out_ref, short_out_ref, wup_ref, wdown_ref,
                         z_ref, y_ref, vn_ref, pool_ext, conv_ext, short_ext):
    j = pl.program_id(1)
    tile = x_ref.shape[0]
    wup_ref[...] = wup_f32_ref[...].astype(BF16)
    wdown_ref[...] = wdown_f32_ref[...].astype(BF16)

    @pl.when(j == 0)
    def _():
        pool_ext[:, 0:POOL_HIST, :] = jnp.zeros((N_SLABS, POOL_HIST, LANES), F32)
        conv_ext[:, 0:CONV_HIST, :] = jnp.zeros((N_SLABS, CONV_HIST, LANES), F32)
        short_ext[:, 0:SHORT_HIST, :] = jnp.zeros((N_SLABS, SHORT_HIST, LANES), F32)

    def project_out(r0):
        rows = slice(r0, r0 + MATMUL_ROWS)
        o = _dot(y_ref[rows, :], wout_ref[...])
        out_ref[rows, :] = x_ref[rows, :] + _rms_norm(o, gpost_ref[...])

    for r0 in range(0, tile, MATMUL_ROWS):
        rows = slice(r0, r0 + MATMUL_ROWS)
        h = _rms_norm(x_ref[rows, :], gpre_ref[...]).astype(BF16)
        z_ref[rows, :] = _dot(h, win_ref[...])
        for r in range(r0, r0 + MATMUL_ROWS, ROW_BLOCK):
            _mix_rows(r, (j == 0) if r == 0 else None, z_ref, y_ref, vn_ref, pool_ext, conv_ext,
                      short_ext, wpool_ref, pscale_ref, pinv_first_ref, pinv_rest_ref, convw_ref,
                      convb_ref, clng_ref, clnb_ref, slng_ref, slnb_ref, shortw_ref)
            if r > 0:
                _gate_rows(r - ROW_BLOCK, z_ref, y_ref, vn_ref, swcat_ref, sbias_ref)
            if r == r0 and r0 > 0:
                project_out(r0 - MATMUL_ROWS)
    _gate_rows(tile - ROW_BLOCK, z_ref, y_ref, vn_ref, swcat_ref, sbias_ref)
    project_out(tile - MATMUL_ROWS)

    pool_ext[:, 0:POOL_HIST, :] = pool_ext[:, tile:tile + POOL_HIST, :]
    conv_ext[:, 0:CONV_HIST, :] = conv_ext[:, tile:tile + CONV_HIST, :]
    short_ext[:, 0:SHORT_HIST, :] = short_ext[:, tile:tile + SHORT_HIST, :]

    @pl.when(j == pl.num_programs(1) - 1)
    def _():
        for s in range(N_SLABS):
            lanes = slice(s * LANES, (s + 1) * LANES)
            pool_out_ref[:, lanes] = pool_ext[s, POOL_HIST - POOL_BUF:POOL_HIST, :]
            conv_out_ref[:, lanes] = conv_ext[s, CONV_HIST - (CONV_WIDTH - 1):CONV_HIST, :]
            short_out_ref[:, lanes] = short_ext[s, SHORT_HIST - (SHORT_WIDTH - 1):SHORT_HIST, :]


def _sample_mixer_kernel(x_ref, pool_in_ref, conv_in_ref, short_in_ref,
                         gpre_ref, gpost_ref, win_ref, wout_ref, wpool_ref, pscale_ref,
                         convw_ref, convb_ref, clng_ref, clnb_ref, slng_ref, slnb_ref,
                         sgw_ref, sgb_ref, shortw_ref,
                         out_ref, pool_out_ref, conv_out_ref, short_out_ref, v_out_ref,
                         z_ref, y_ref):
    n_steps, n_seq = v_out_ref.shape[0], v_out_ref.shape[1]
    gw = GROUP_WIDTH
    x = x_ref[...]
    h = _rms_norm(x, gpre_ref[...]).astype(BF16)
    z_ref[...] = _dot(h, win_ref[...])
    grp, w = _pool_window((n_seq, gw))
    cnt = jnp.minimum(w, PAST_LEN + 1).astype(F32)

    def slab(t):
        return slice(t * n_seq, (t + 1) * n_seq)

    a_new = [z_ref[slab(t), 0:gw] for t in range(n_steps)]
    g_new = [z_ref[slab(t), gw:2 * gw] * jax.nn.sigmoid(z_ref[slab(t), 2 * gw:3 * gw])
             for t in range(n_steps)]
    ch_new = [z_ref[slab(t), 6 * gw:7 * gw] * z_ref[slab(t), 7 * gw:8 * gw]
              for t in range(n_steps)]

    def pool_row(i):
        return pool_in_ref[i] if i < POOL_BUF else a_new[i - POOL_BUF]

    def conv_row(i):
        return conv_in_ref[i] if i < CONV_WIDTH - 1 else g_new[i - (CONV_WIDTH - 1)]

    def short_row(i):
        return short_in_ref[i] if i < SHORT_WIDTH - 1 else ch_new[i - (SHORT_WIDTH - 1)]

    vn = []
    for t in range(n_steps):
        rows = slab(t)
        end = POOL_BUF + t
        acc = pool_row(end)
        sums = []
        for k in range(1, max(POOL_WINDOWS)):
            acc = acc + pool_row(end - k)
            if k + 1 in POOL_WINDOWS:
                sums.append(acc)
        win = sums[0]
        for gi in range(1, N_SUB):
            win = jnp.where(grp == gi, sums[gi], win)
        d = win / cnt - a_new[t]
        y_ref[rows, 0:gw] = (_dot(d.astype(BF16), wpool_ref[...]) * pscale_ref[...]).astype(BF16)

        c = None
        for k in range(CONV_WIDTH):
            term = conv_row(t + k) * convw_ref[k:k + 1, :]
            c = term if c is None else c + term
        c = _head_layer_norm(c + convb_ref[...], clng_ref[...], clnb_ref[...])
        y_ref[rows, gw:2 * gw] = (c * jax.nn.sigmoid(c)).astype(BF16)

        vn.append(_head_layer_norm(z_ref[rows, 4 * gw:5 * gw], slng_ref[...], slnb_ref[...]))
        v_out_ref[t] = vn[t]
        s = sgb_ref[t:t + 1, :]
        for u in range(t + 1):
            s = s + sgw_ref[t * n_steps + u:t * n_steps + u + 1, :] * vn[u]
        y_ref[rows, 2 * gw:3 * gw] = (z_ref[rows, 3 * gw:4 * gw] * s).astype(BF16)

        sc = None
        for k in range(SHORT_WIDTH):
            term = short_row(t + k) * shortw_ref[k:k + 1, :]
            sc = term if sc is None else sc + term
        y_ref[rows, 3 * gw:4 * gw] = (z_ref[rows, 5 * gw:6 * gw] * sc).astype(BF16)

    o = _dot(y_ref[...], wout_ref[...])
    out_ref[...] = x + _rms_norm(o, gpost_ref[...])

    for i in range(POOL_BUF):
        pool_out_ref[i] = pool_row(i + n_steps)
    for i in range(CONV_WIDTH - 1):
        conv_out_ref[i] = conv_row(i + n_steps)
    for i in range(SHORT_WIDTH - 1):
        short_out_ref[i] = short_row(i + n_steps)


def _ffn_kernel(n_cast, x_ref, gpre_ref, gpost_ref, wup_ref, wdown_ref, *rest):
    cast_in, out_ref, cast_out, acc_ref = (rest[:n_cast], rest[n_cast], rest[n_cast + 1:-1], rest[-1])
    for src, dst in zip(cast_in, cast_out):
        dst[...] = src[...].astype(BF16)
    for r0 in range(0, x_ref.shape[0], FFN_ROWS):
        rows = slice(r0, r0 + FFN_ROWS)
        x = x_ref[rows, :]
        f = _rms_norm(x, gpre_ref[...]).astype(BF16)
        for c in range(D_FF // FF_CHUNK):
            cols = slice(c * FF_CHUNK, (c + 1) * FF_CHUNK)
            u = jnp.maximum(_dot(f, wup_ref[:, cols]), 0.0)
            part = _dot((u * u).astype(BF16), wdown_ref[cols, :])
            if c == 0:
                acc_ref[rows, :] = part
            else:
                acc_ref[rows, :] += part
        out_ref[rows, :] = x + _rms_norm(acc_ref[rows, :], gpost_ref[...])


def _const_spec(shape):
    nd = len(shape)
    return pl.BlockSpec(shape, lambda *_: (0,) * nd, pipeline_mode=pl.Buffered(1))


def _whole_spec(shape):
    nd = len(shape)
    return pl.BlockSpec(shape, lambda *_: (0,) * nd)


def _compiler_params(semantics):
    return pltpu.CompilerParams(dimension_semantics=semantics, vmem_limit_bytes=VMEM_LIMIT_BYTES)


def _prompt_mixer(x, p, w_in, w_out, w_ffn_up, w_ffn_down, layer):
    b, s, d = x.shape
    tile = PROMPT_TILE
    consts = (p["gpre"], p["gpost"], w_in, w_out, p["wpool"], p["pscale"],
              p["pinv_first"], p["pinv_rest"], p["conv_w"], p["conv_b"], p["cln_g"], p["cln_b"],
              p["sln_g"], p["sln_b"], p["sgu_wcat"], p["sgu_bias"], p["short_w"])
    gw = GROUP_WIDTH
    tiles_per_seq = s // tile
    steps = b * tiles_per_seq
    d_ff = w_ffn_up.shape[2]
    ff_slice = d_ff // steps
    assert ff_slice * steps == d_ff and ff_slice % LANES == 0
    step = lambda bi, j: bi * tiles_per_seq + j
    state_spec = lambda n: pl.BlockSpec((None, n, gw), lambda bi, j: (bi, 0, 0))
    return pl.pallas_call(
        _prompt_mixer_kernel,
        grid=(b, tiles_per_seq),
        in_specs=[pl.BlockSpec((None, tile, d), lambda bi, j: (bi, j, 0))]
        + [_const_spec(c.shape) for c in consts]
        + [pl.BlockSpec((None, d, ff_slice), lambda bi, j: (layer, 0, step(bi, j))),
           pl.BlockSpec((None, ff_slice, d), lambda bi, j: (layer, step(bi, j), 0))],
        out_specs=[pl.BlockSpec((None, tile, d), lambda bi, j: (bi, j, 0)),
                   state_spec(POOL_BUF), state_spec(CONV_WIDTH - 1), state_spec(SHORT_WIDTH - 1),
                   pl.BlockSpec((d, ff_slice), lambda bi, j: (0, step(bi, j))),
                   pl.BlockSpec((ff_slice, d), lambda bi, j: (step(bi, j), 0))],
        out_shape=[jax.ShapeDtypeStruct((b, s, d), F32),
                   jax.ShapeDtypeStruct((b, POOL_BUF, gw), F32),
                   jax.ShapeDtypeStruct((b, CONV_WIDTH - 1, gw), F32),
                   jax.ShapeDtypeStruct((b, SHORT_WIDTH - 1, gw), F32),
                   jax.ShapeDtypeStruct((d, d_ff), BF16),
                   jax.ShapeDtypeStruct((d_ff, d), BF16)],
        scratch_shapes=[pltpu.VMEM((tile, IN_WIDTH), F32),
                        pltpu.VMEM((tile, d), BF16),
                        pltpu.VMEM((tile, gw), F32),
                        pltpu.VMEM((N_SLABS, POOL_HIST + tile, LANES), F32),
                        pltpu.VMEM((N_SLABS, CONV_HIST + tile, LANES), F32),
                        pltpu.VMEM((N_SLABS, SHORT_HIST + tile, LANES), F32)],
        compiler_params=_compiler_params(("arbitrary", "arbitrary")),
        name="prompt_mixer",
    )(x, *consts, w_ffn_up, w_ffn_down)


def _sample_mixer(x, pool_st, conv_st, short_st, p, w_in, w_out):
    n, d = x.shape
    n_seq = pool_st.shape[1]
    n_steps = n // n_seq
    gw = GROUP_WIDTH
    args = (x, pool_st, conv_st, short_st,
            p["gpre"], p["gpost"], w_in, w_out, p["wpool"], p["pscale"],
            p["conv_w"], p["conv_b"], p["cln_g"], p["cln_b"], p["sln_g"], p["sln_b"],
            p["sgu_w4"], p["sgu_b4"], p["short_w"])
    out_shape = [jax.ShapeDtypeStruct((n, d), F32),
                 jax.ShapeDtypeStruct(pool_st.shape, F32),
                 jax.ShapeDtypeStruct(conv_st.shape, F32),
                 jax.ShapeDtypeStruct(short_st.shape, F32),
                 jax.ShapeDtypeStruct((n_steps, n_seq, gw), F32)]
    return pl.pallas_call(
        _sample_mixer_kernel,
        grid=(1,),
        in_specs=[_whole_spec(a.shape) for a in args],
        out_specs=[_whole_spec(o.shape) for o in out_shape],
        out_shape=out_shape,
        scratch_shapes=[pltpu.VMEM((n, IN_WIDTH), F32), pltpu.VMEM((n, d), BF16)],
        compiler_params=_compiler_params(("arbitrary",)),
        name="sample_mixer",
    )(*args)


def _ffn(x, p, w_up, w_down, name, cast=()):
    n, d = x.shape
    tile = min(FFN_TILE, n)
    steps = n // tile
    consts = (p["fpre"], p["fpost"], w_up, w_down)
    cast_in_specs, cast_out_specs, cast_shapes = [], [], []
    for w, layer in cast:
        rows = w.shape[1] // steps
        assert rows * steps == w.shape[1] and rows % (2 * SUBLANES) == 0
        cast_in_specs.append(pl.BlockSpec((None, rows, w.shape[2]), lambda i, layer=layer: (layer, i, 0)))
        cast_out_specs.append(pl.BlockSpec((rows, w.shape[2]), lambda i: (i, 0)))
        cast_shapes.append(jax.ShapeDtypeStruct(w.shape[1:], BF16))
    outs = pl.pallas_call(
        functools.partial(_ffn_kernel, len(cast)),
        grid=(steps,),
        in_specs=[pl.BlockSpec((tile, d), lambda i: (i, 0))] + [_const_spec(c.shape) for c in consts]
        + cast_in_specs,
        out_specs=[pl.BlockSpec((tile, d), lambda i: (i, 0))] + cast_out_specs,
        out_shape=[jax.ShapeDtypeStruct((n, d), F32)] + cast_shapes,
        scratch_shapes=[pltpu.VMEM((tile, d), F32)],
        compiler_params=_compiler_params(("arbitrary",)),
        name=name,
    )(x, *consts, *[w for w, _ in cast])
    return outs if cast else outs[0]


def _layer_params(l, n_steps, norm_mix_pre, norm_mix_post, norm_ffn_pre, norm_ffn_post,
                  w_pool, pool_scale, conv_w, conv_b, conv_ln_g, conv_ln_b, sgu_ln_g, sgu_ln_b,
                  sgu_w, sgu_b, short_w):
    row = lambda v: v[l].reshape(1, -1)
    gw = GROUP_WIDTH
    head_of_lane = jnp.arange(gw) // SUB_DIM
    same_head = head_of_lane[:, None] == head_of_lane[None, :]
    wp = jnp.tile(w_pool[l].reshape(gw, SUB_DIM), (1, N_SUB))
    wpool = jnp.where(same_head, wp, 0.0).astype(BF16)
    tril = jnp.tril(jnp.ones((CHUNK, CHUNK), dtype=bool))
    w_causal = jnp.where(tril[None], sgu_w[l], 0.0)
    wcat = jnp.transpose(w_causal, (1, 0, 2)).reshape(CHUNK, N_SUB * CHUNK).astype(BF16)
    bias = jnp.repeat(jnp.transpose(sgu_b[l]), SUB_DIM, axis=1)
    w4 = jnp.repeat(jnp.transpose(w_causal[:, :n_steps, :n_steps], (1, 2, 0)), SUB_DIM, axis=2)
    w4 = w4.reshape(n_steps * n_steps, gw)
    window = jnp.repeat(jnp.asarray(POOL_WINDOWS, F32), SUB_DIM)[None, :]
    pinv_first = 1.0 / jnp.minimum(window, jnp.arange(1, ROW_BLOCK + 1, dtype=F32)[:, None])
    return dict(
        pinv_first=pinv_first, pinv_rest=1.0 / window,
        gpre=row(norm_mix_pre), gpost=row(norm_mix_post), fpre=row(norm_ffn_pre),
        fpost=row(norm_ffn_post),
        wpool=wpool, pscale=row(pool_scale), conv_w=conv_w[l], conv_b=row(conv_b),
        cln_g=row(conv_ln_g), cln_b=row(conv_ln_b), sln_g=row(sgu_ln_g), sln_b=row(sgu_ln_b),
        sgu_wcat=wcat, sgu_bias=bias, sgu_w4=w4, sgu_b4=bias[:n_steps], short_w=short_w[l])


def kernel(x_prompt, x_sample, state_pool, state_conv, state_short, norm_mix_pre, norm_mix_post, norm_ffn_pre, norm_ffn_post, w_in, w_out, w_pool, pool_scale, conv_w, conv_b, conv_ln_g, conv_ln_b, sgu_ln_g, sgu_ln_b, sgu_w, sgu_b, short_w, w_ffn_up, w_ffn_down):
    depth = w_in.shape[0]
    bp, seq, d = x_prompt.shape
    n_seq, n_steps, _ = x_sample.shape
    assert seq % PROMPT_TILE == 0 and PROMPT_TILE % MATMUL_ROWS == 0 and MATMUL_ROWS % ROW_BLOCK == 0
    assert ROW_BLOCK >= max(POOL_WINDOWS) and n_steps <= CHUNK and PAST_LEN % CHUNK == 0

    yp = x_prompt
    ys = jnp.transpose(x_sample, (1, 0, 2)).reshape(n_steps * n_seq, d)
    outs = [[] for _ in range(7)]
    w_in_b, w_out_b = w_in[0].astype(BF16), w_out[0].astype(BF16)
    for l in range(depth):
        p = _layer_params(l, n_steps, norm_mix_pre, norm_mix_post, norm_ffn_pre, norm_ffn_post,
                          w_pool, pool_scale, conv_w, conv_b, conv_ln_g, conv_ln_b,
                          sgu_ln_g, sgu_ln_b, sgu_w, sgu_b, short_w)
        yp, pool_p, conv_p, short_p, w_up_b, w_down_b = _prompt_mixer(
            yp, p, w_in_b, w_out_b, w_ffn_up, w_ffn_down, l)
        cast = ((w_in, l + 1), (w_out, l + 1)) if l + 1 < depth else ()
        ffn_out = _ffn(yp.reshape(bp * seq, d), p, w_up_b, w_down_b, "prompt_ffn", cast)
        yp, w_next = (ffn_out[0], ffn_out[1:]) if cast else (ffn_out, None)
        yp = yp.reshape(bp, seq, d)

        ys, pool_s, conv_s, short_s, v_s = _sample_mixer(
            ys, jnp.transpose(state_pool[l], (1, 0, 2)), jnp.transpose(state_conv[l], (1, 0, 2)),
            jnp.transpose(state_short[l], (1, 0, 2)), p, w_in_b, w_out_b)
        ys = _ffn(ys, p, w_up_b, w_down_b, "sample_ffn")
        if w_next is not None:
            w_in_b, w_out_b = w_next

        to_seq_major = lambda a: jnp.transpose(a, (1, 0, 2))
        for lst, val in zip(outs, (pool_p, to_seq_major(pool_s), conv_p, to_seq_major(conv_s),
                                   short_p, to_seq_major(short_s), to_seq_major(v_s))):
            lst.append(val)

    ys = jnp.transpose(ys.reshape(n_steps, n_seq, d), (1, 0, 2))
    return (yp, ys) + tuple(jnp.stack(o) for o in outs)
```

```python
import functools

import jax
import jax.numpy as jnp
import numpy as np
from jax import lax
from jax.experimental import pallas as pl
from jax.experimental.pallas import tpu as pltpu

D_MODEL = 1024
GROUP_WIDTH = 256
N_SUB = 4
SUB_DIM = 64
POOL_WINDOWS = (2, 4, 8, 16)
POOL_BUF = 15
CONV_WIDTH = 31
SHORT_WIDTH = 3
CHUNK = 128
D_FF = 4096
EPS = 1e-6
PAST_LEN = 16384
IN_WIDTH = 8 * GROUP_WIDTH

SUBLANES = 8
LANES = 128
N_SLABS = GROUP_WIDTH // LANES
POOL_HIST = 16
CONV_HIST = 32
SHORT_HIST = 8

ROW_BLOCK = CHUNK
MATMUL_ROWS = 256
PROMPT_TILE = 1024
FFN_TILE = 1024
FFN_ROWS = 512
FF_CHUNK = 512
VMEM_LIMIT_BYTES = 56 * 1024 * 1024

F32 = jnp.float32
BF16 = jnp.bfloat16


def _rms_norm(x, g):
    ms = jnp.mean(x * x, axis=-1, keepdims=True)
    return x * lax.rsqrt(ms + EPS) * g


def _dot(a, b):
    return jnp.dot(a, b, preferred_element_type=F32)


def _head_mean(x, low_head):
    s_low = jnp.sum(jnp.where(low_head, x, 0.0), axis=-1, keepdims=True)
    s_high = jnp.sum(jnp.where(low_head, 0.0, x), axis=-1, keepdims=True)
    return jnp.where(low_head, s_low, s_high) * (1.0 / SUB_DIM)


def _head_layer_norm(x, g, b):
    low_head = lax.broadcasted_iota(jnp.int32, (x.shape[0], LANES), 1) < SUB_DIM
    out = []
    for s in range(N_SLABS):
        lanes = slice(s * LANES, (s + 1) * LANES)
        xs = x[:, lanes]
        xc = xs - _head_mean(xs, low_head)
        var = _head_mean(xc * xc, low_head)
        out.append(xc * lax.rsqrt(var + EPS) * g[:, lanes] + b[:, lanes])
    return jnp.concatenate(out, axis=1)


def _lane_group(shape):
    return jnp.right_shift(lax.broadcasted_iota(jnp.int32, shape, 1), SUB_DIM.bit_length() - 1)


def _pool_window(shape):
    grp = _lane_group(shape)
    w = jnp.full(shape, POOL_WINDOWS[0], jnp.int32)
    for gi in range(1, N_SUB):
        w = jnp.where(grp == gi, POOL_WINDOWS[gi], w)
    return grp, w


def _gating_rhs(vn):
    grp = _lane_group(vn.shape)
    return jnp.concatenate([jnp.where(grp == h, vn, 0.0).astype(BF16) for h in range(N_SUB)], axis=0)


def _cols(group, slab):
    lo = group * GROUP_WIDTH + slab * LANES
    return slice(lo, lo + LANES)


def _trailing_sum(ext_ref, slab, start, n_rows, width):
    acc = ext_ref[slab, start:start + n_rows, :]
    for k in range(1, width):
        acc = acc + ext_ref[slab, start - k:start - k + n_rows, :]
    return acc


def _causal_taps(ext_ref, slab, w_ref, hist, n_taps, r):
    first = hist + r - (n_taps - 1)
    lanes = slice(slab * LANES, (slab + 1) * LANES)
    acc = None
    for k in range(n_taps):
        term = ext_ref[slab, first + k:first + k + ROW_BLOCK, :] * w_ref[k:k + 1, lanes]
        acc = term if acc is None else acc + term
    return acc


def _gate_rows(r, z_ref, y_ref, vn_ref, swcat_ref, sbias_ref):
    rows = slice(r, r + ROW_BLOCK)
    gw = GROUP_WIDTH
    sg = _dot(swcat_ref[...], _gating_rhs(vn_ref[rows, :])) + sbias_ref[...]
    y_ref[rows, 2 * gw:3 * gw] = (z_ref[rows, 3 * gw:4 * gw] * sg).astype(BF16)


def _mix_rows(r, seq_start, z_ref, y_ref, vn_ref, pool_ext, conv_ext, short_ext, wpool_ref,
              pscale_ref, pinv_first_ref, pinv_rest_ref, convw_ref, convb_ref, clng_ref, clnb_ref,
              slng_ref, slnb_ref, shortw_ref):
    rows = slice(r, r + ROW_BLOCK)
    gw = GROUP_WIDTH
    low_head = lax.broadcasted_iota(jnp.int32, (ROW_BLOCK, LANES), 1) < SUB_DIM

    d = []
    for s in range(N_SLABS):
        lanes = slice(s * LANES, (s + 1) * LANES)
        a = z_ref[rows, _cols(0, s)]
        base = POOL_HIST + r
        pool_ext[s, base:base + ROW_BLOCK, :] = a
        small, large = POOL_WINDOWS[2 * s], POOL_WINDOWS[2 * s + 1]
        assert large == 2 * small
        if small % SUBLANES == 0:
            run = _trailing_sum(pool_ext, s, base - small, ROW_BLOCK + small, small)
            s_small = run[small:]
            s_large = s_small + run[:ROW_BLOCK]
        else:
            s_small = _trailing_sum(pool_ext, s, base, ROW_BLOCK, small)
            s_large = s_small + _trailing_sum(pool_ext, s, base - small, ROW_BLOCK, small)
        win = jnp.where(low_head, s_small, s_large)
        inv = pinv_rest_ref[:, lanes]
        if seq_start is not None:
            inv = jnp.where(seq_start, pinv_first_ref[:, lanes], inv)
        d.append(win * inv - a)
    d = jnp.concatenate(d, axis=1).astype(BF16)
    y_ref[rows, 0:gw] = (_dot(d, wpool_ref[...]) * pscale_ref[...]).astype(BF16)

    c = []
    for s in range(N_SLABS):
        g = z_ref[rows, _cols(1, s)] * jax.nn.sigmoid(z_ref[rows, _cols(2, s)])
        conv_ext[s, CONV_HIST + r:CONV_HIST + r + ROW_BLOCK, :] = g
        c.append(_causal_taps(conv_ext, s, convw_ref, CONV_HIST, CONV_WIDTH, r))
    c = jnp.concatenate(c, axis=1) + convb_ref[...]
    c = _head_layer_norm(c, clng_ref[...], clnb_ref[...])
    y_ref[rows, gw:2 * gw] = (c * jax.nn.sigmoid(c)).astype(BF16)

    vn_ref[rows, :] = _head_layer_norm(z_ref[rows, 4 * gw:5 * gw], slng_ref[...], slnb_ref[...])

    for s in range(N_SLABS):
        ch = z_ref[rows, _cols(6, s)] * z_ref[rows, _cols(7, s)]
        short_ext[s, SHORT_HIST + r:SHORT_HIST + r + ROW_BLOCK, :] = ch
        sc = _causal_taps(short_ext, s, shortw_ref, SHORT_HIST, SHORT_WIDTH, r)
        y_ref[rows, _cols(3, s)] = (z_ref[rows, _cols(5, s)] * sc).astype(BF16)


def _layer_views(layer, *refs):
    return [r.at[pl.ds(layer, 1)] if len(r.shape) == 2 else r.at[layer] for r in refs]


def _prompt_mixer_kernel(layer, x_ref, win_ref, wout_ref, pinv_first_ref, pinv_rest_ref,
                         gpre_ref, gpost_ref, wpool_ref, pscale_ref, convw_ref, convb_ref, clng_ref,
                         clnb_ref, slng_ref, slnb_ref, sguw_ref, sbias_ref, shortw_ref,
                         wup_f32_ref, wdown_f32_ref,
                         out_ref, pool_out_ref, conv_out_ref, short_out_ref, wup_ref, wdown_ref,
                         z_ref, y_ref, vn_ref, swcat_ref, pool_ext, conv_ext, short_ext):
    (gpre_ref, gpost_ref, wpool_ref, pscale_ref, convw_ref, convb_ref, clng_ref, clnb_ref, slng_ref,
     slnb_ref, sguw_ref, sbias_ref, shortw_ref) = _layer_views(
         layer, gpre_ref, gpost_ref, wpool_ref, pscale_ref, convw_ref, convb_ref, clng_ref, clnb_ref,
         slng_ref, slnb_ref, sguw_ref, sbias_ref, shortw_ref)
    j = pl.program_id(1)
    tile = x_ref.shape[0]
    wup_ref[...] = wup_f32_ref[...].astype(BF16)
    wdown_ref[...] = wdown_f32_ref[...].astype(BF16)
    causal = (lax.broadcasted_iota(jnp.int32, (CHUNK, CHUNK), 0)
              >= lax.broadcasted_iota(jnp.int32, (CHUNK, CHUNK), 1))
    for hd in range(N_SUB):
        swcat_ref[:, hd * CHUNK:(hd + 1) * CHUNK] = jnp.where(causal, sguw_ref[hd], 0.0).astype(BF16)

    @pl.when(j == 0)
    def _():
        pool_ext[:, 0:POOL_HIST, :] = jnp.zeros((N_SLABS, POOL_HIST, LANES), F32)
        conv_ext[:, 0:CONV_HIST, :] = jnp.zeros((N_SLABS, CONV_HIST, LANES), F32)
        short_ext[:, 0:SHORT_HIST, :] = jnp.zeros((N_SLABS, SHORT_HIST, LANES), F32)

    def project_out(r0):
        rows = slice(r0, r0 + MATMUL_ROWS)
        o = _dot(y_ref[rows, :], wout_ref[...])
        out_ref[rows, :] = x_ref[rows, :] + _rms_norm(o, gpost_ref[...])

    for r0 in range(0, tile, MATMUL_ROWS):
        rows = slice(r0, r0 + MATMUL_ROWS)
        h = _rms_norm(x_ref[rows, :], gpre_ref[...]).astype(BF16)
        z_ref[rows, :] = _dot(h, win_ref[...])
        for r in range(r0, r0 + MATMUL_ROWS, ROW_BLOCK):
            _mix_rows(r, (j == 0) if r == 0 else None, z_ref, y_ref, vn_ref, pool_ext, conv_ext,
                      short_ext, wpool_ref, pscale_ref, pinv_first_ref, pinv_rest_ref, convw_ref,
                      convb_ref, clng_ref, clnb_ref, slng_ref, slnb_ref, shortw_ref)
            if r > 0:
                _gate_rows(r - ROW_BLOCK, z_ref, y_ref, vn_ref, swcat_ref, sbias_ref)
            if r == r0 and r0 > 0:
                project_out(r0 - MATMUL_ROWS)
    _gate_rows(tile - ROW_BLOCK, z_ref, y_ref, vn_ref, swcat_ref, sbias_ref)
    project_out(tile - MATMUL_ROWS)

    pool_ext[:, 0:POOL_HIST, :] = pool_ext[:, tile:tile + POOL_HIST, :]
    conv_ext[:, 0:CONV_HIST, :] = conv_ext[:, tile:tile + CONV_HIST, :]
    short_ext[:, 0:SHORT_HIST, :] = short_ext[:, tile:tile + SHORT_HIST, :]

    @pl.when(j == pl.num_programs(1) - 1)
    def _():
        for s in range(N_SLABS):
            lanes = slice(s * LANES, (s + 1) * LANES)
            pool_out_ref[:, lanes] = pool_ext[s, POOL_HIST - POOL_BUF:POOL_HIST, :]
            conv_out_ref[:, lanes] = conv_ext[s, CONV_HIST - (CONV_WIDTH - 1):CONV_HIST, :]
            short_out_ref[:, lanes] = short_ext[s, SHORT_HIST - (SHORT_WIDTH - 1):SHORT_HIST, :]


def _sample_mixer_kernel(layer, x_ref, pool_in_ref, conv_in_ref, short_in_ref, win_ref, wout_ref,
                         gpre_ref, gpost_ref, wpool_ref, pscale_ref, convw_ref, convb_ref, clng_ref,
                         clnb_ref, slng_ref, slnb_ref, sgw_ref, sgb_ref, shortw_ref,
                         out_ref, pool_out_ref, conv_out_ref, short_out_ref, v_out_ref,
                         z_ref, y_ref):
    (gpre_ref, gpost_ref, wpool_ref, pscale_ref, convw_ref, convb_ref, clng_ref, clnb_ref, slng_ref,
     slnb_ref, sgw_ref, sgb_ref, shortw_ref) = _layer_views(
         layer, gpre_ref, gpost_ref, wpool_ref, pscale_ref, convw_ref, convb_ref, clng_ref, clnb_ref,
         slng_ref, slnb_ref, sgw_ref, sgb_ref, shortw_ref)
    n_steps, n_seq = v_out_ref.shape[0], v_out_ref.shape[1]
    gw = GROUP_WIDTH
    x = x_ref[...]
    h = _rms_norm(x, gpre_ref[...]).astype(BF16)
    z_ref[...] = _dot(h, win_ref[...])
    grp, w = _pool_window((n_seq, gw))
    cnt = jnp.minimum(w, PAST_LEN + 1).astype(F32)

    def slab(t):
        return slice(t * n_seq, (t + 1) * n_seq)

    a_new = [z_ref[slab(t), 0:gw] for t in range(n_steps)]
    g_new = [z_ref[slab(t), gw:2 * gw] * jax.nn.sigmoid(z_ref[slab(t), 2 * gw:3 * gw])
             for t in range(n_steps)]
    ch_new = [z_ref[slab(t), 6 * gw:7 * gw] * z_ref[slab(t), 7 * gw:8 * gw]
              for t in range(n_steps)]

    def pool_row(i):
        return pool_in_ref[i] if i < POOL_BUF else a_new[i - POOL_BUF]

    def conv_row(i):
        return conv_in_ref[i] if i < CONV_WIDTH - 1 else g_new[i - (CONV_WIDTH - 1)]

    def short_row(i):
        return short_in_ref[i] if i < SHORT_WIDTH - 1 else ch_new[i - (SHORT_WIDTH - 1)]

    vn = []
    for t in range(n_steps):
        rows = slab(t)
        end = POOL_BUF + t
        acc = pool_row(end)
        sums = []
        for k in range(1, max(POOL_WINDOWS)):
            acc = acc + pool_row(end - k)
            if k + 1 in POOL_WINDOWS:
                sums.append(acc)
        win = sums[0]
        for gi in range(1, N_SUB):
            win = jnp.where(grp == gi, sums[gi], win)
        d = win / cnt - a_new[t]
        y_ref[rows, 0:gw] = (_dot(d.astype(BF16), wpool_ref[...]) * pscale_ref[...]).astype(BF16)

        c = None
        for k in range(CONV_WIDTH):
            term = conv_row(t + k) * convw_ref[k:k + 1, :]
            c = term if c is None else c + term
        c = _head_layer_norm(c + convb_ref[...], clng_ref[...], clnb_ref[...])
        y_ref[rows, gw:2 * gw] = (c * jax.nn.sigmoid(c)).astype(BF16)

        vn.append(_head_layer_norm(z_ref[rows, 4 * gw:5 * gw], slng_ref[...], slnb_ref[...]))
        v_out_ref[t] = vn[t]
        s = sgb_ref[t:t + 1, :]
        for u in range(t + 1):
            s = s + sgw_ref[t * n_steps + u:t * n_steps + u + 1, :] * vn[u]
        y_ref[rows, 2 * gw:3 * gw] = (z_ref[rows, 3 * gw:4 * gw] * s).astype(BF16)

        sc = None
        for k in range(SHORT_WIDTH):
            term = short_row(t + k) * shortw_ref[k:k + 1, :]
            sc = term if sc is None else sc + term
        y_ref[rows, 3 * gw:4 * gw] = (z_ref[rows, 5 * gw:6 * gw] * sc).astype(BF16)

    o = _dot(y_ref[...], wout_ref[...])
    out_ref[...] = x + _rms_norm(o, gpost_ref[...])

    for i in range(POOL_BUF):
        pool_out_ref[i] = pool_row(i + n_steps)
    for i in range(CONV_WIDTH - 1):
        conv_out_ref[i] = conv_row(i + n_steps)
    for i in range(SHORT_WIDTH - 1):
        short_out_ref[i] = short_row(i + n_steps)


def _ffn_kernel(layer, n_cast, x_ref, gpre_ref, gpost_ref, wup_ref, wdown_ref, *rest):
    cast_in, out_ref, cast_out, acc_ref = (rest[:n_cast], rest[n_cast], rest[n_cast + 1:-1], rest[-1])
    gpre_ref, gpost_ref = _layer_views(layer, gpre_ref, gpost_ref)
    for src, dst in zip(cast_in, cast_out):
        dst[...] = src[...].astype(BF16)
    for r0 in range(0, x_ref.shape[0], FFN_ROWS):
        rows = slice(r0, r0 + FFN_ROWS)
        x = x_ref[rows, :]
        f = _rms_norm(x, gpre_ref[...]).astype(BF16)
        for c in range(D_FF // FF_CHUNK):
            cols = slice(c * FF_CHUNK, (c + 1) * FF_CHUNK)
            u = jnp.maximum(_dot(f, wup_ref[:, cols]), 0.0)
            part = _dot((u * u).astype(BF16), wdown_ref[cols, :])
            if c == 0:
                acc_ref[rows, :] = part
            else:
                acc_ref[rows, :] += part
        out_ref[rows, :] = x + _rms_norm(acc_ref[rows, :], gpost_ref[...])


def _const_spec(shape):
    nd = len(shape)
    return pl.BlockSpec(shape, lambda *_: (0,) * nd, pipeline_mode=pl.Buffered(1))


def _whole_spec(shape):
    nd = len(shape)
    return pl.BlockSpec(shape, lambda *_: (0,) * nd)


def _compiler_params(semantics):
    return pltpu.CompilerParams(dimension_semantics=semantics, vmem_limit_bytes=VMEM_LIMIT_BYTES)


def _prompt_mixer(x, p, w_in, w_out, w_ffn_up, w_ffn_down, layer):
    b, s, d = x.shape
    tile = PROMPT_TILE
    consts = (w_in, w_out, POOL_INV_FIRST, POOL_INV_REST,
              p["gpre"], p["gpost"], p["wpool"], p["pscale"], p["conv_w"], p["conv_b"], p["cln_g"],
              p["cln_b"], p["sln_g"], p["sln_b"], p["sgu_w"], p["sgu_bias"], p["short_w"])
    gw = GROUP_WIDTH
    tiles_per_seq = s // tile
    steps = b * tiles_per_seq
    d_ff = w_ffn_up.shape[2]
    ff_slice = d_ff // steps
    assert ff_slice * steps == d_ff and ff_slice % LANES == 0
    step = lambda bi, j: bi * tiles_per_seq + j
    state_spec = lambda n: pl.BlockSpec((None, n, gw), lambda bi, j: (bi, 0, 0))
    return pl.pallas_call(
        functools.partial(_prompt_mixer_kernel, layer),
        grid=(b, tiles_per_seq),
        in_specs=[pl.BlockSpec((None, tile, d), lambda bi, j: (bi, j, 0))]
        + [_const_spec(c.shape) for c in consts]
        + [pl.BlockSpec((None, d, ff_slice), lambda bi, j: (layer, 0, step(bi, j))),
           pl.BlockSpec((None, ff_slice, d), lambda bi, j: (layer, step(bi, j), 0))],
        out_specs=[pl.BlockSpec((None, tile, d), lambda bi, j: (bi, j, 0)),
                   state_spec(POOL_BUF), state_spec(CONV_WIDTH - 1), state_spec(SHORT_WIDTH - 1),
                   pl.BlockSpec((d, ff_slice), lambda bi, j: (0, step(bi, j))),
                   pl.BlockSpec((ff_slice, d), lambda bi, j: (step(bi, j), 0))],
        out_shape=[jax.ShapeDtypeStruct((b, s, d), F32),
                   jax.ShapeDtypeStruct((b, POOL_BUF, gw), F32),
                   jax.ShapeDtypeStruct((b, CONV_WIDTH - 1, gw), F32),
                   jax.ShapeDtypeStruct((b, SHORT_WIDTH - 1, gw), F32),
                   jax.ShapeDtypeStruct((d, d_ff), BF16),
                   jax.ShapeDtypeStruct((d_ff, d), BF16)],
        scratch_shapes=[pltpu.VMEM((tile, IN_WIDTH), F32),
                        pltpu.VMEM((tile, d), BF16),
                        pltpu.VMEM((tile, gw), F32),
                        pltpu.VMEM((CHUNK, N_SUB * CHUNK), BF16),
                        pltpu.VMEM((N_SLABS, POOL_HIST + tile, LANES), F32),
                        pltpu.VMEM((N_SLABS, CONV_HIST + tile, LANES), F32),
                        pltpu.VMEM((N_SLABS, SHORT_HIST + tile, LANES), F32)],
        compiler_params=_compiler_params(("arbitrary", "arbitrary")),
        name="prompt_mixer",
    )(x, *consts, w_ffn_up, w_ffn_down)


def _sample_mixer(x, pool_st, conv_st, short_st, p, w_in, w_out, layer):
    n, d = x.shape
    n_seq = pool_st.shape[2]
    n_steps = n // n_seq
    gw = GROUP_WIDTH
    states = (pool_st, conv_st, short_st)
    consts = (w_in, w_out, p["gpre"], p["gpost"], p["wpool"], p["pscale"], p["conv_w"], p["conv_b"],
              p["cln_g"], p["cln_b"], p["sln_g"], p["sln_b"], p["sgu_w4"], p["sgu_bias"], p["short_w"])
    state_spec = lambda st: pl.BlockSpec((None,) + st.shape[1:], lambda i: (layer, 0, 0, 0))
    out_shape = ([jax.ShapeDtypeStruct((n, d), F32)]
                 + [jax.ShapeDtypeStruct(st.shape[1:], F32) for st in states]
                 + [jax.ShapeDtypeStruct((n_steps, n_seq, gw), F32)])
    return pl.pallas_call(
        functools.partial(_sample_mixer_kernel, layer),
        grid=(1,),
        in_specs=[_whole_spec(x.shape)] + [state_spec(st) for st in states]
        + [_whole_spec(c.shape) for c in consts],
        out_specs=[_whole_spec(o.shape) for o in out_shape],
        out_shape=out_shape,
        scratch_shapes=[pltpu.VMEM((n, IN_WIDTH), F32), pltpu.VMEM((n, d), BF16)],
        compiler_params=_compiler_params(("arbitrary",)),
        name="sample_mixer",
    )(x, *states, *consts)


def _ffn(x, p, w_up, w_down, layer, name, cast=()):
    n, d = x.shape
    tile = min(FFN_TILE, n)
    steps = n // tile
    consts = (p["fpre"], p["fpost"], w_up, w_down)
    cast_in_specs, cast_out_specs, cast_shapes = [], [], []
    for w, w_layer in cast:
        rows = w.shape[1] // steps
        assert rows * steps == w.shape[1] and rows % (2 * SUBLANES) == 0
        cast_in_specs.append(
            pl.BlockSpec((None, rows, w.shape[2]), lambda i, w_layer=w_layer: (w_layer, i, 0)))
        cast_out_specs.append(pl.BlockSpec((rows, w.shape[2]), lambda i: (i, 0)))
        cast_shapes.append(jax.ShapeDtypeStruct(w.shape[1:], BF16))
    outs = pl.pallas_call(
        functools.partial(_ffn_kernel, layer, len(cast)),
        grid=(steps,),
        in_specs=[pl.BlockSpec((tile, d), lambda i: (i, 0))] + [_const_spec(c.shape) for c in consts]
        + cast_in_specs,
        out_specs=[pl.BlockSpec((tile, d), lambda i: (i, 0))] + cast_out_specs,
        out_shape=[jax.ShapeDtypeStruct((n, d), F32)] + cast_shapes,
        scratch_shapes=[pltpu.VMEM((tile, d), F32)],
        compiler_params=_compiler_params(("arbitrary",)),
        name=name,
    )(x, *consts, *[w for w, _ in cast])
    return outs if cast else outs[0]


def _pool_inverse_counts():
    window = np.repeat(np.asarray(POOL_WINDOWS, np.float32), SUB_DIM)[None, :]
    first = 1.0 / np.minimum(window, np.arange(1, ROW_BLOCK + 1, dtype=np.float32)[:, None])
    return first.astype(np.float32), (1.0 / window).astype(np.float32)


POOL_INV_FIRST, POOL_INV_REST = _pool_inverse_counts()


def _stacked_params(n_steps, norm_mix_pre, norm_mix_post, norm_ffn_pre, norm_ffn_post, w_pool,
                    pool_scale, conv_w, conv_b, conv_ln_g, conv_ln_b, sgu_ln_g, sgu_ln_b, sgu_w,
                    sgu_b, short_w):
    depth = w_pool.shape[0]
    gw = GROUP_WIDTH
    head_of_lane = np.arange(gw) // SUB_DIM
    same_head = head_of_lane[:, None] == head_of_lane[None, :]
    wp = jnp.tile(w_pool.reshape(depth, gw, SUB_DIM), (1, 1, N_SUB))
    wpool = jnp.where(same_head[None], wp, 0.0).astype(BF16)
    bias = jnp.repeat(jnp.swapaxes(sgu_b, 1, 2), SUB_DIM, axis=2)
    causal = np.tril(np.ones((n_steps, n_steps), dtype=bool))
    w4 = jnp.where(causal[None, None], sgu_w[:, :, :n_steps, :n_steps], 0.0)
    w4 = jnp.repeat(jnp.transpose(w4, (0, 2, 3, 1)), SUB_DIM, axis=3)
    w4 = w4.reshape(depth, n_steps * n_steps, gw)
    return dict(gpre=norm_mix_pre, gpost=norm_mix_post, fpre=norm_ffn_pre, fpost=norm_ffn_post,
                wpool=wpool, pscale=pool_scale, conv_w=conv_w, conv_b=conv_b, cln_g=conv_ln_g,
                cln_b=conv_ln_b, sln_g=sgu_ln_g, sln_b=sgu_ln_b, sgu_w=sgu_w, sgu_bias=bias,
                sgu_w4=w4, short_w=short_w)


def kernel(x_prompt, x_sample, state_pool, state_conv, state_short, norm_mix_pre, norm_mix_post, norm_ffn_pre, norm_ffn_post, w_in, w_out, w_pool, pool_scale, conv_w, conv_b, conv_ln_g, conv_ln_b, sgu_ln_g, sgu_ln_b, sgu_w, sgu_b, short_w, w_ffn_up, w_ffn_down):
    depth = w_in.shape[0]
    bp, seq, d = x_prompt.shape
    n_seq, n_steps, _ = x_sample.shape
    assert seq % PROMPT_TILE == 0 and PROMPT_TILE % MATMUL_ROWS == 0 and MATMUL_ROWS % ROW_BLOCK == 0
    assert ROW_BLOCK >= max(POOL_WINDOWS) and n_steps <= CHUNK and PAST_LEN % CHUNK == 0

    p = _stacked_params(n_steps, norm_mix_pre, norm_mix_post, norm_ffn_pre, norm_ffn_post, w_pool,
                        pool_scale, conv_w, conv_b, conv_ln_g, conv_ln_b, sgu_ln_g, sgu_ln_b, sgu_w,
                        sgu_b, short_w)
    yp = x_prompt
    ys = jnp.transpose(x_sample, (1, 0, 2)).reshape(n_steps * n_seq, d)
    hist_major = lambda a: jnp.transpose(a, (0, 2, 1, 3))
    pool_in, conv_in, short_in = hist_major(state_pool), hist_major(state_conv), hist_major(state_short)
    prompt_states = [[] for _ in range(3)]
    sample_states = [[] for _ in range(4)]
    w_in_b, w_out_b = w_in[0].astype(BF16), w_out[0].astype(BF16)
    for l in range(depth):
        yp, pool_p, conv_p, short_p, w_up_b, w_down_b = _prompt_mixer(
            yp, p, w_in_b, w_out_b, w_ffn_up, w_ffn_down, l)
        cast = ((w_in, l + 1), (w_out, l + 1)) if l + 1 < depth else ()
        ffn_out = _ffn(yp.reshape(bp * seq, d), p, w_up_b, w_down_b, l, "prompt_ffn", cast)
        yp, w_next = (ffn_out[0], ffn_out[1:]) if cast else (ffn_out, None)
        yp = yp.reshape(bp, seq, d)

        ys, pool_s, conv_s, short_s, v_s = _sample_mixer(
            ys, pool_in, conv_in, short_in, p, w_in_b, w_out_b, l)
        ys = _ffn(ys, p, w_up_b, w_down_b, l, "sample_ffn")
        if w_next is not None:
            w_in_b, w_out_b = w_next

        for lst, val in zip(prompt_states, (pool_p, conv_p, short_p)):
            lst.append(val)
        for lst, val in zip(sample_states, (pool_s, conv_s, short_s, v_s)):
            lst.append(val)

    ys = jnp.transpose(ys.reshape(n_steps, n_seq, d), (1, 0, 2))
    pool_p, conv_p, short_p = (jnp.stack(o) for o in prompt_states)
    pool_s, conv_s, short_s, v_s = (hist_major(jnp.stack(o)) for o in sample_states)
    return (yp, ys, pool_p, pool_s, conv_p, conv_s, short_p, short_s, v_s)
```

```python
import functools

import jax
import jax.numpy as jnp
import numpy as np
from jax import lax
from jax.experimental import pallas as pl
from jax.experimental.pallas import tpu as pltpu

D_MODEL = 1024
GROUP_WIDTH = 256
N_SUB = 4
SUB_DIM = 64
POOL_WINDOWS = (2, 4, 8, 16)
POOL_BUF = 15
CONV_WIDTH = 31
SHORT_WIDTH = 3
CHUNK = 128
D_FF = 4096
EPS = 1e-6
PAST_LEN = 16384
IN_WIDTH = 8 * GROUP_WIDTH

SUBLANES = 8
LANES = 128
N_SLABS = GROUP_WIDTH // LANES
POOL_HIST = 16
CONV_HIST = 32
SHORT_HIST = 8

ROW_BLOCK = CHUNK
MATMUL_ROWS = 512
PROMPT_TILE = 1024
FFN_TILE = 1024
FFN_ROWS = 512
FF_CHUNK = 1024
VMEM_LIMIT_BYTES = 56 * 1024 * 1024

F32 = jnp.float32
BF16 = jnp.bfloat16


def _rms_norm(x, g):
    ms = jnp.mean(x * x, axis=-1, keepdims=True)
    return x * lax.rsqrt(ms + EPS) * g


def _dot(a, b):
    return jnp.dot(a, b, preferred_element_type=F32)


def _head_mean(x, low_head):
    s_low = jnp.sum(jnp.where(low_head, x, 0.0), axis=-1, keepdims=True)
    s_high = jnp.sum(jnp.where(low_head, 0.0, x), axis=-1, keepdims=True)
    return jnp.where(low_head, s_low, s_high) * (1.0 / SUB_DIM)


def _head_layer_norm(x, g, b):
    low_head = lax.broadcasted_iota(jnp.int32, (x.shape[0], LANES), 1) < SUB_DIM
    out = []
    for s in range(N_SLABS):
        lanes = slice(s * LANES, (s + 1) * LANES)
        xs = x[:, lanes]
        xc = xs - _head_mean(xs, low_head)
        var = _head_mean(xc * xc, low_head)
        out.append(xc * lax.rsqrt(var + EPS) * g[:, lanes] + b[:, lanes])
    return jnp.concatenate(out, axis=1)


def _lane_group(shape):
    return jnp.right_shift(lax.broadcasted_iota(jnp.int32, shape, 1), SUB_DIM.bit_length() - 1)


def _pool_window(shape):
    grp = _lane_group(shape)
    w = jnp.full(shape, POOL_WINDOWS[0], jnp.int32)
    for gi in range(1, N_SUB):
        w = jnp.where(grp == gi, POOL_WINDOWS[gi], w)
    return grp, w


def _gating_rhs(vn):
    grp = _lane_group(vn.shape)
    return jnp.concatenate([jnp.where(grp == h, vn, 0.0).astype(BF16) for h in range(N_SUB)], axis=0)


def _cols(group, slab):
    lo = group * GROUP_WIDTH + slab * LANES
    return slice(lo, lo + LANES)


def _trailing_sum(ext_ref, slab, start, n_rows, width):
    acc = ext_ref[slab, start:start + n_rows, :]
    for k in range(1, width):
        acc = acc + ext_ref[slab, start - k:start - k + n_rows, :]
    return acc


def _causal_taps(ext_ref, slab, w_ref, hist, n_taps, r):
    first = hist + r - (n_taps - 1)
    lanes = slice(slab * LANES, (slab + 1) * LANES)
    acc = None
    for k in range(n_taps):
        term = ext_ref[slab, first + k:first + k + ROW_BLOCK, :] * w_ref[k:k + 1, lanes]
        acc = term if acc is None else acc + term
    return acc


def _gate_rows(r, z_ref, y_ref, vn_ref, swcat_ref, sbias_ref):
    rows = slice(r, r + ROW_BLOCK)
    gw = GROUP_WIDTH
    sg = _dot(swcat_ref[...], _gating_rhs(vn_ref[rows, :])) + sbias_ref[...]
    y_ref[rows, 2 * gw:3 * gw] = (z_ref[rows, 3 * gw:4 * gw] * sg).astype(BF16)


def _mix_rows(r, seq_start, z_ref, y_ref, vn_ref, pool_ext, conv_ext, short_ext, wpool_ref,
              pscale_ref, pinv_first_ref, pinv_rest_ref, convw_ref, convb_ref, clng_ref, clnb_ref,
              slng_ref, slnb_ref, shortw_ref):
    rows = slice(r, r + ROW_BLOCK)
    gw = GROUP_WIDTH
    low_head = lax.broadcasted_iota(jnp.int32, (ROW_BLOCK, LANES), 1) < SUB_DIM

    d = []
    for s in range(N_SLABS):
        lanes = slice(s * LANES, (s + 1) * LANES)
        a = z_ref[rows, _cols(0, s)]
        base = POOL_HIST + r
        pool_ext[s, base:base + ROW_BLOCK, :] = a
        small, large = POOL_WINDOWS[2 * s], POOL_WINDOWS[2 * s + 1]
        assert large == 2 * small
        if small % SUBLANES == 0:
            run = _trailing_sum(pool_ext, s, base - small, ROW_BLOCK + small, small)
            s_small = run[small:]
            s_large = s_small + run[:ROW_BLOCK]
        else:
            s_small = _trailing_sum(pool_ext, s, base, ROW_BLOCK, small)
            s_large = s_small + _trailing_sum(pool_ext, s, base - small, ROW_BLOCK, small)
        win = jnp.where(low_head, s_small, s_large)
        inv = pinv_rest_ref[:, lanes]
        if seq_start is not None:
            inv = jnp.where(seq_start, pinv_first_ref[:, lanes], inv)
        d.append(win * inv - a)
    d = jnp.concatenate(d, axis=1).astype(BF16)
    y_ref[rows, 0:gw] = (_dot(d, wpool_ref[...]) * pscale_ref[...]).astype(BF16)

    c = []
    for s in range(N_SLABS):
        g = z_ref[rows, _cols(1, s)] * jax.nn.sigmoid(z_ref[rows, _cols(2, s)])
        conv_ext[s, CONV_HIST + r:CONV_HIST + r + ROW_BLOCK, :] = g
        c.append(_causal_taps(conv_ext, s, convw_ref, CONV_HIST, CONV_WIDTH, r))
    c = jnp.concatenate(c, axis=1) + convb_ref[...]
    c = _head_layer_norm(c, clng_ref[...], clnb_ref[...])
    y_ref[rows, gw:2 * gw] = (c * jax.nn.sigmoid(c)).astype(BF16)

    vn_ref[rows, :] = _head_layer_norm(z_ref[rows, 4 * gw:5 * gw], slng_ref[...], slnb_ref[...])

    for s in range(N_SLABS):
        ch = z_ref[rows, _cols(6, s)] * z_ref[rows, _cols(7, s)]
        short_ext[s, SHORT_HIST + r:SHORT_HIST + r + ROW_BLOCK, :] = ch
        sc = _causal_taps(short_ext, s, shortw_ref, SHORT_HIST, SHORT_WIDTH, r)
        y_ref[rows, _cols(3, s)] = (z_ref[rows, _cols(5, s)] * sc).astype(BF16)


def _layer_views(layer, *refs):
    return [r.at[pl.ds(layer, 1)] if len(r.shape) == 2 else r.at[layer] for r in refs]


def _prompt_mixer_kernel(layer, x_ref, win_ref, wout_ref, pinv_first_ref, pinv_rest_ref,
                         gpre_ref, gpost_ref, wpool_ref, pscale_ref, convw_ref, convb_ref, clng_ref,
                         clnb_ref, slng_ref, slnb_ref, sguw_ref, sbias_ref, shortw_ref,
                         wup_f32_ref, wdown_f32_ref,
                         out_ref, pool_out_ref, conv_out_ref, short_out_ref, wup_ref, wdown_ref,
                         z_ref, y_ref, vn_ref, swcat_ref, pool_ext, conv_ext, short_ext):
    (gpre_ref, gpost_ref, wpool_ref, pscale_ref, convw_ref, convb_ref, clng_ref, clnb_ref, slng_ref,
     slnb_ref, sguw_ref, sbias_ref, shortw_ref) = _layer_views(
         layer, gpre_ref, gpost_ref, wpool_ref, pscale_ref, convw_ref, convb_ref, clng_ref, clnb_ref,
         slng_ref, slnb_ref, sguw_ref, sbias_ref, shortw_ref)
    j = pl.program_id(1)
    tile = x_ref.shape[0]
    wup_ref[...] = wup_f32_ref[...].astype(BF16)
    wdown_ref[...] = wdown_f32_ref[...].astype(BF16)
    causal = (lax.broadcasted_iota(jnp.int32, (CHUNK, CHUNK), 0)
              >= lax.broadcasted_iota(jnp.int32, (CHUNK, CHUNK), 1))
    for hd in range(N_SUB):
        swcat_ref[:, hd * CHUNK:(hd + 1) * CHUNK] = jnp.where(causal, sguw_ref[hd], 0.0).astype(BF16)

    @pl.when(j == 0)
    def _():
        pool_ext[:, 0:POOL_HIST, :] = jnp.zeros((N_SLABS, POOL_HIST, LANES), F32)
        conv_ext[:, 0:CONV_HIST, :] = jnp.zeros((N_SLABS, CONV_HIST, LANES), F32)
        short_ext[:, 0:SHORT_HIST, :] = jnp.zeros((N_SLABS, SHORT_HIST, LANES), F32)

    def project_out(r0):
        rows = slice(r0, r0 + MATMUL_ROWS)
        o = _dot(y_ref[rows, :], wout_ref[...])
        out_ref[rows, :] = x_ref[rows, :] + _rms_norm(o, gpost_ref[...])

    for r0 in range(0, tile, MATMUL_ROWS):
        rows = slice(r0, r0 + MATMUL_ROWS)
        h = _rms_norm(x_ref[rows, :], gpre_ref[...]).astype(BF16)
        z_ref[rows, :] = _dot(h, win_ref[...])
        for r in range(r0, r0 + MATMUL_ROWS, ROW_BLOCK):
            _mix_rows(r, (j == 0) if r == 0 else None, z_ref, y_ref, vn_ref, pool_ext, conv_ext,
                      short_ext, wpool_ref, pscale_ref, pinv_first_ref, pinv_rest_ref, convw_ref,
                      convb_ref, clng_ref, clnb_ref, slng_ref, slnb_ref, shortw_ref)
            if r > 0:
                _gate_rows(r - ROW_BLOCK, z_ref, y_ref, vn_ref, swcat_ref, sbias_ref)
            if r == r0 and r0 > 0:
                project_out(r0 - MATMUL_ROWS)
    _gate_rows(tile - ROW_BLOCK, z_ref, y_ref, vn_ref, swcat_ref, sbias_ref)
    project_out(tile - MATMUL_ROWS)

    pool_ext[:, 0:POOL_HIST, :] = pool_ext[:, tile:tile + POOL_HIST, :]
    conv_ext[:, 0:CONV_HIST, :] = conv_ext[:, tile:tile + CONV_HIST, :]
    short_ext[:, 0:SHORT_HIST, :] = short_ext[:, tile:tile + SHORT_HIST, :]

    @pl.when(j == pl.num_programs(1) - 1)
    def _():
        for s in range(N_SLABS):
            lanes = slice(s * LANES, (s + 1) * LANES)
            pool_out_ref[:, lanes] = pool_ext[s, POOL_HIST - POOL_BUF:POOL_HIST, :]
            conv_out_ref[:, lanes] = conv_ext[s, CONV_HIST - (CONV_WIDTH - 1):CONV_HIST, :]
            short_out_ref[:, lanes] = short_ext[s, SHORT_HIST - (SHORT_WIDTH - 1):SHORT_HIST, :]


def _sample_mixer_kernel(layer, x_ref, pool_in_ref, conv_in_ref, short_in_ref, win_ref, wout_ref,
                         gpre_ref, gpost_ref, wpool_ref, pscale_ref, convw_ref, convb_ref, clng_ref,
                         clnb_ref, slng_ref, slnb_ref, sgw_ref, sgb_ref, shortw_ref,
                         out_ref, pool_out_ref, conv_out_ref, short_out_ref, v_out_ref,
                         z_ref, y_ref):
    (gpre_ref, gpost_ref, wpool_ref, pscale_ref, convw_ref, convb_ref, clng_ref, clnb_ref, slng_ref,
     slnb_ref, sgw_ref, sgb_ref, shortw_ref) = _layer_views(
         layer, gpre_ref, gpost_ref, wpool_ref, pscale_ref, convw_ref, convb_ref, clng_ref, clnb_ref,
         slng_ref, slnb_ref, sgw_ref, sgb_ref, shortw_ref)
    n_steps, n_seq = v_out_ref.shape[0], v_out_ref.shape[1]
    gw = GROUP_WIDTH
    x = x_ref[...]
    h = _rms_norm(x, gpre_ref[...]).astype(BF16)
    z_ref[...] = _dot(h, win_ref[...])
    grp, w = _pool_window((n_seq, gw))
    cnt = jnp.minimum(w, PAST_LEN + 1).astype(F32)

    def slab(t):
        return slice(t * n_seq, (t + 1) * n_seq)

    a_new = [z_ref[slab(t), 0:gw] for t in range(n_steps)]
    g_new = [z_ref[slab(t), gw:2 * gw] * jax.nn.sigmoid(z_ref[slab(t), 2 * gw:3 * gw])
             for t in range(n_steps)]
    ch_new = [z_ref[slab(t), 6 * gw:7 * gw] * z_ref[slab(t), 7 * gw:8 * gw]
              for t in range(n_steps)]

    def pool_row(i):
        return pool_in_ref[i] if i < POOL_BUF else a_new[i - POOL_BUF]

    def conv_row(i):
        return conv_in_ref[i] if i < CONV_WIDTH - 1 else g_new[i - (CONV_WIDTH - 1)]

    def short_row(i):
        return short_in_ref[i] if i < SHORT_WIDTH - 1 else ch_new[i - (SHORT_WIDTH - 1)]

    vn = []
    for t in range(n_steps):
        rows = slab(t)
        end = POOL_BUF + t
        acc = pool_row(end)
        sums = []
        for k in range(1, max(POOL_WINDOWS)):
            acc = acc + pool_row(end - k)
            if k + 1 in POOL_WINDOWS:
                sums.append(acc)
        win = sums[0]
        for gi in range(1, N_SUB):
            win = jnp.where(grp == gi, sums[gi], win)
        d = win / cnt - a_new[t]
        y_ref[rows, 0:gw] = (_dot(d.astype(BF16), wpool_ref[...]) * pscale_ref[...]).astype(BF16)

        c = None
        for k in range(CONV_WIDTH):
            term = conv_row(t + k) * convw_ref[k:k + 1, :]
            c = term if c is None else c + term
        c = _head_layer_norm(c + convb_ref[...], clng_ref[...], clnb_ref[...])
        y_ref[rows, gw:2 * gw] = (c * jax.nn.sigmoid(c)).astype(BF16)

        vn.append(_head_layer_norm(z_ref[rows, 4 * gw:5 * gw], slng_ref[...], slnb_ref[...]))
        v_out_ref[t] = vn[t]
        s = sgb_ref[t:t + 1, :]
        for u in range(t + 1):
            s = s + sgw_ref[t * n_steps + u:t * n_steps + u + 1, :] * vn[u]
        y_ref[rows, 2 * gw:3 * gw] = (z_ref[rows, 3 * gw:4 * gw] * s).astype(BF16)

        sc = None
        for k in range(SHORT_WIDTH):
            term = short_row(t + k) * shortw_ref[k:k + 1, :]
            sc = term if sc is None else sc + term
        y_ref[rows, 3 * gw:4 * gw] = (z_ref[rows, 5 * gw:6 * gw] * sc).astype(BF16)

    o = _dot(y_ref[...], wout_ref[...])
    out_ref[...] = x + _rms_norm(o, gpost_ref[...])

    for i in range(POOL_BUF):
        pool_out_ref[i] = pool_row(i + n_steps)
    for i in range(CONV_WIDTH - 1):
        conv_out_ref[i] = conv_row(i + n_steps)
    for i in range(SHORT_WIDTH - 1):
        short_out_ref[i] = short_row(i + n_steps)


def _mlp_rows(x_ref, out_ref, acc_ref, gpre_ref, gpost_ref, wup_ref, wdown_ref):
    for r0 in range(0, x_ref.shape[0], FFN_ROWS):
        rows = slice(r0, r0 + FFN_ROWS)
        x = x_ref[rows, :]
        f = _rms_norm(x, gpre_ref[...]).astype(BF16)
        for c in range(D_FF // FF_CHUNK):
            cols = slice(c * FF_CHUNK, (c + 1) * FF_CHUNK)
            u = jnp.maximum(_dot(f, wup_ref[:, cols]), 0.0)
            part = _dot((u * u).astype(BF16), wdown_ref[cols, :])
            if c == 0:
                acc_ref[rows, :] = part
            else:
                acc_ref[rows, :] += part
        out_ref[rows, :] = x + _rms_norm(acc_ref[rows, :], gpost_ref[...])


def _ffn_kernel(layer, n_cast, x_ref, xs_ref, gpre_ref, gpost_ref, wup_ref, wdown_ref, *rest):
    cast_in, out_ref, outs_ref = rest[:n_cast], rest[n_cast], rest[n_cast + 1]
    cast_out, acc_ref = rest[n_cast + 2:-1], rest[-1]
    gpre_ref, gpost_ref = _layer_views(layer, gpre_ref, gpost_ref)
    i = pl.program_id(0)
    last = pl.num_programs(0) - 1

    @pl.when(i < last)
    def _():
        for src, dst in zip(cast_in, cast_out):
            dst[...] = src[...].astype(BF16)
        _mlp_rows(x_ref, out_ref, acc_ref, gpre_ref, gpost_ref, wup_ref, wdown_ref)

    @pl.when(i == last)
    def _():
        _mlp_rows(xs_ref, outs_ref, acc_ref, gpre_ref, gpost_ref, wup_ref, wdown_ref)


def _const_spec(shape):
    nd = len(shape)
    return pl.BlockSpec(shape, lambda *_: (0,) * nd, pipeline_mode=pl.Buffered(1))


def _whole_spec(shape):
    nd = len(shape)
    return pl.BlockSpec(shape, lambda *_: (0,) * nd)


def _compiler_params(semantics):
    return pltpu.CompilerParams(dimension_semantics=semantics, vmem_limit_bytes=VMEM_LIMIT_BYTES)


def _prompt_mixer(x, p, w_in, w_out, w_ffn_up, w_ffn_down, layer):
    b, s, d = x.shape
    tile = PROMPT_TILE
    consts = (w_in, w_out, POOL_INV_FIRST, POOL_INV_REST,
              p["gpre"], p["gpost"], p["wpool"], p["pscale"], p["conv_w"], p["conv_b"], p["cln_g"],
              p["cln_b"], p["sln_g"], p["sln_b"], p["sgu_w"], p["sgu_bias"], p["short_w"])
    gw = GROUP_WIDTH
    tiles_per_seq = s // tile
    steps = b * tiles_per_seq
    d_ff = w_ffn_up.shape[2]
    ff_slice = d_ff // steps
    assert ff_slice * steps == d_ff and ff_slice % LANES == 0
    step = lambda bi, j: bi * tiles_per_seq + j
    state_spec = lambda n: pl.BlockSpec((None, n, gw), lambda bi, j: (bi, 0, 0))
    return pl.pallas_call(
        functools.partial(_prompt_mixer_kernel, layer),
        grid=(b, tiles_per_seq),
        in_specs=[pl.BlockSpec((None, tile, d), lambda bi, j: (bi, j, 0))]
        + [_const_spec(c.shape) for c in consts]
        + [pl.BlockSpec((None, d, ff_slice), lambda bi, j: (layer, 0, step(bi, j))),
           pl.BlockSpec((None, ff_slice, d), lambda bi, j: (layer, step(bi, j), 0))],
        out_specs=[pl.BlockSpec((None, tile, d), lambda bi, j: (bi, j, 0)),
                   state_spec(POOL_BUF), state_spec(CONV_WIDTH - 1), state_spec(SHORT_WIDTH - 1),
                   pl.BlockSpec((d, ff_slice), lambda bi, j: (0, step(bi, j))),
                   pl.BlockSpec((ff_slice, d), lambda bi, j: (step(bi, j), 0))],
        out_shape=[jax.ShapeDtypeStruct((b, s, d), F32),
                   jax.ShapeDtypeStruct((b, POOL_BUF, gw), F32),
                   jax.ShapeDtypeStruct((b, CONV_WIDTH - 1, gw), F32),
                   jax.ShapeDtypeStruct((b, SHORT_WIDTH - 1, gw), F32),
                   jax.ShapeDtypeStruct((d, d_ff), BF16),
                   jax.ShapeDtypeStruct((d_ff, d), BF16)],
        scratch_shapes=[pltpu.VMEM((tile, IN_WIDTH), F32),
                        pltpu.VMEM((tile, d), BF16),
                        pltpu.VMEM((tile, gw), F32),
                        pltpu.VMEM((CHUNK, N_SUB * CHUNK), BF16),
                        pltpu.VMEM((N_SLABS, POOL_HIST + tile, LANES), F32),
                        pltpu.VMEM((N_SLABS, CONV_HIST + tile, LANES), F32),
                        pltpu.VMEM((N_SLABS, SHORT_HIST + tile, LANES), F32)],
        compiler_params=_compiler_params(("arbitrary", "arbitrary")),
        name="prompt_mixer",
    )(x, *consts, w_ffn_up, w_ffn_down)


def _sample_mixer(x, pool_st, conv_st, short_st, p, w_in, w_out, layer):
    n, d = x.shape
    n_seq = pool_st.shape[2]
    n_steps = n // n_seq
    gw = GROUP_WIDTH
    states = (pool_st, conv_st, short_st)
    consts = (w_in, w_out, p["gpre"], p["gpost"], p["wpool"], p["pscale"], p["conv_w"], p["conv_b"],
              p["cln_g"], p["cln_b"], p["sln_g"], p["sln_b"], p["sgu_w4"], p["sgu_bias"], p["short_w"])
    state_spec = lambda st: pl.BlockSpec((None,) + st.shape[1:], lambda i: (layer, 0, 0, 0))
    out_shape = ([jax.ShapeDtypeStruct((n, d), F32)]
                 + [jax.ShapeDtypeStruct(st.shape[1:], F32) for st in states]
                 + [jax.ShapeDtypeStruct((n_steps, n_seq, gw), F32)])
    return pl.pallas_call(
        functools.partial(_sample_mixer_kernel, layer),
        grid=(1,),
        in_specs=[_whole_spec(x.shape)] + [state_spec(st) for st in states]
        + [_whole_spec(c.shape) for c in consts],
        out_specs=[_whole_spec(o.shape) for o in out_shape],
        out_shape=out_shape,
        scratch_shapes=[pltpu.VMEM((n, IN_WIDTH), F32), pltpu.VMEM((n, d), BF16)],
        compiler_params=_compiler_params(("arbitrary",)),
        name="sample_mixer",
    )(x, *states, *consts)


def _ffn(x, xs, p, w_up, w_down, layer, cast=()):
    n, d = x.shape
    ns = xs.shape[0]
    tile = FFN_TILE
    steps = n // tile
    assert steps * tile == n and ns <= tile and ns % FFN_ROWS == 0
    consts = (p["fpre"], p["fpost"], w_up, w_down)
    prompt_tile = lambda i: (jnp.minimum(i, steps - 1), 0)
    cast_in_specs, cast_out_specs, cast_shapes = [], [], []
    for w, w_layer in cast:
        rows = w.shape[1] // steps
        assert rows * steps == w.shape[1] and rows % (2 * SUBLANES) == 0
        cast_in_specs.append(pl.BlockSpec(
            (None, rows, w.shape[2]), lambda i, w_layer=w_layer: (w_layer,) + prompt_tile(i)))
        cast_out_specs.append(pl.BlockSpec((rows, w.shape[2]), prompt_tile))
        cast_shapes.append(jax.ShapeDtypeStruct(w.shape[1:], BF16))
    return pl.pallas_call(
        functools.partial(_ffn_kernel, layer, len(cast)),
        grid=(steps + 1,),
        in_specs=[pl.BlockSpec((tile, d), prompt_tile), _whole_spec(xs.shape)]
        + [_const_spec(c.shape) for c in consts] + cast_in_specs,
        out_specs=[pl.BlockSpec((tile, d), prompt_tile), _whole_spec(xs.shape)] + cast_out_specs,
        out_shape=[jax.ShapeDtypeStruct((n, d), F32), jax.ShapeDtypeStruct((ns, d), F32)] + cast_shapes,
        scratch_shapes=[pltpu.VMEM((tile, d), F32)],
        compiler_params=_compiler_params(("arbitrary",)),
        name="ffn",
    )(x, xs, *consts, *[w for w, _ in cast])


def _pool_inverse_counts():
    window = np.repeat(np.asarray(POOL_WINDOWS, np.float32), SUB_DIM)[None, :]
    first = 1.0 / np.minimum(window, np.arange(1, ROW_BLOCK + 1, dtype=np.float32)[:, None])
    return first.astype(np.float32), (1.0 / window).astype(np.float32)


POOL_INV_FIRST, POOL_INV_REST = _pool_inverse_counts()


def _stacked_params(n_steps, norm_mix_pre, norm_mix_post, norm_ffn_pre, norm_ffn_post, w_pool,
                    pool_scale, conv_w, conv_b, conv_ln_g, conv_ln_b, sgu_ln_g, sgu_ln_b, sgu_w,
                    sgu_b, short_w):
    depth = w_pool.shape[0]
    gw = GROUP_WIDTH
    head_of_lane = np.arange(gw) // SUB_DIM
    same_head = head_of_lane[:, None] == head_of_lane[None, :]
    wp = jnp.tile(w_pool.reshape(depth, gw, SUB_DIM), (1, 1, N_SUB))
    wpool = jnp.where(same_head[None], wp, 0.0).astype(BF16)
    bias = jnp.repeat(jnp.swapaxes(sgu_b, 1, 2), SUB_DIM, axis=2)
    causal = np.tril(np.ones((n_steps, n_steps), dtype=bool))
    w4 = jnp.where(causal[None, None], sgu_w[:, :, :n_steps, :n_steps], 0.0)
    w4 = jnp.repeat(jnp.transpose(w4, (0, 2, 3, 1)), SUB_DIM, axis=3)
    w4 = w4.reshape(depth, n_steps * n_steps, gw)
    return dict(gpre=norm_mix_pre, gpost=norm_mix_post, fpre=norm_ffn_pre, fpost=norm_ffn_post,
                wpool=wpool, pscale=pool_scale, conv_w=conv_w, conv_b=conv_b, cln_g=conv_ln_g,
                cln_b=conv_ln_b, sln_g=sgu_ln_g, sln_b=sgu_ln_b, sgu_w=sgu_w, sgu_bias=bias,
                sgu_w4=w4, short_w=short_w)


def kernel(x_prompt, x_sample, state_pool, state_conv, state_short, norm_mix_pre, norm_mix_post, norm_ffn_pre, norm_ffn_post, w_in, w_out, w_pool, pool_scale, conv_w, conv_b, conv_ln_g, conv_ln_b, sgu_ln_g, sgu_ln_b, sgu_w, sgu_b, short_w, w_ffn_up, w_ffn_down):
    depth = w_in.shape[0]
    bp, seq, d = x_prompt.shape
    n_seq, n_steps, _ = x_sample.shape
    assert seq % PROMPT_TILE == 0 and PROMPT_TILE % MATMUL_ROWS == 0 and MATMUL_ROWS % ROW_BLOCK == 0
    assert ROW_BLOCK >= max(POOL_WINDOWS) and n_steps <= CHUNK and PAST_LEN % CHUNK == 0

    p = _stacked_params(n_steps, norm_mix_pre, norm_mix_post, norm_ffn_pre, norm_ffn_post, w_pool,
                        pool_scale, conv_w, conv_b, conv_ln_g, conv_ln_b, sgu_ln_g, sgu_ln_b, sgu_w,
                        sgu_b, short_w)
    yp = x_prompt
    ys = jnp.transpose(x_sample, (1, 0, 2)).reshape(n_steps * n_seq, d)
    hist_major = lambda a: jnp.transpose(a, (0, 2, 1, 3))
    pool_in, conv_in, short_in = hist_major(state_pool), hist_major(state_conv), hist_major(state_short)
    prompt_states = [[] for _ in range(3)]
    sample_states = [[] for _ in range(4)]
    w_in_b, w_out_b = w_in[0].astype(BF16), w_out[0].astype(BF16)
    for l in range(depth):
        yp, pool_p, conv_p, short_p, w_up_b, w_down_b = _prompt_mixer(
            yp, p, w_in_b, w_out_b, w_ffn_up, w_ffn_down, l)
        ys, pool_s, conv_s, short_s, v_s = _sample_mixer(
            ys, pool_in, conv_in, short_in, p, w_in_b, w_out_b, l)
        cast = ((w_in, l + 1), (w_out, l + 1)) if l + 1 < depth else ()
        yp, ys, *w_next = _ffn(yp.reshape(bp * seq, d), ys, p, w_up_b, w_down_b, l, cast)
        yp = yp.reshape(bp, seq, d)
        if w_next:
            w_in_b, w_out_b = w_next

        for lst, val in zip(prompt_states, (pool_p, conv_p, short_p)):
            lst.append(val)
        for lst, val in zip(sample_states, (pool_s, conv_s, short_s, v_s)):
            lst.append(val)

    ys = jnp.transpose(ys.reshape(n_steps, n_seq, d), (1, 0, 2))
    pool_p, conv_p, short_p = (jnp.stack(o) for o in prompt_states)
    pool_s, conv_s, short_s, v_s = (hist_major(jnp.stack(o)) for o in sample_states)
    return (yp, ys, pool_p, pool_s, conv_p, conv_s, short_p, short_s, v_s)
```

```python
import functools

import jax
import jax.numpy as jnp
import numpy as np
from jax import lax
from jax.experimental import pallas as pl
from jax.experimental.pallas import tpu as pltpu

D_MODEL = 1024
GROUP_WIDTH = 256
N_SUB = 4
SUB_DIM = 64
POOL_WINDOWS = (2, 4, 8, 16)
POOL_BUF = 15
CONV_WIDTH = 31
SHORT_WIDTH = 3
CHUNK = 128
D_FF = 4096
EPS = 1e-6
PAST_LEN = 16384
IN_WIDTH = 8 * GROUP_WIDTH

SUBLANES = 8
LANES = 128
N_SLABS = GROUP_WIDTH // LANES
POOL_HIST = 16
CONV_HIST = 32
SHORT_HIST = 8

ROW_BLOCK = CHUNK
MATMUL_ROWS = 256
PROMPT_TILE = 1024
FFN_TILE = 1024
FFN_ROWS = 512
FF_CHUNK = 1024
VMEM_LIMIT_BYTES = 56 * 1024 * 1024

F32 = jnp.float32
BF16 = jnp.bfloat16


def _rms_norm(x, g):
    ms = jnp.mean(x * x, axis=-1, keepdims=True)
    return x * lax.rsqrt(ms + EPS) * g


def _dot(a, b):
    return jnp.dot(a, b, preferred_element_type=F32)


def _head_mean(x, low_head):
    s_low = jnp.sum(jnp.where(low_head, x, 0.0), axis=-1, keepdims=True)
    s_high = jnp.sum(jnp.where(low_head, 0.0, x), axis=-1, keepdims=True)
    return jnp.where(low_head, s_low, s_high) * (1.0 / SUB_DIM)


def _head_layer_norm(x, g, b):
    low_head = lax.broadcasted_iota(jnp.int32, (x.shape[0], LANES), 1) < SUB_DIM
    out = []
    for s in range(N_SLABS):
        lanes = slice(s * LANES, (s + 1) * LANES)
        xs = x[:, lanes]
        xc = xs - _head_mean(xs, low_head)
        var = _head_mean(xc * xc, low_head)
        out.append(xc * lax.rsqrt(var + EPS) * g[:, lanes] + b[:, lanes])
    return jnp.concatenate(out, axis=1)


def _lane_group(shape):
    return jnp.right_shift(lax.broadcasted_iota(jnp.int32, shape, 1), SUB_DIM.bit_length() - 1)


def _pool_window(shape):
    grp = _lane_group(shape)
    w = jnp.full(shape, POOL_WINDOWS[0], jnp.int32)
    for gi in range(1, N_SUB):
        w = jnp.where(grp == gi, POOL_WINDOWS[gi], w)
    return grp, w


def _gating_rhs(vn):
    grp = _lane_group(vn.shape)
    return jnp.concatenate([jnp.where(grp == h, vn, 0.0).astype(BF16) for h in range(N_SUB)], axis=0)


def _cols(group, slab):
    lo = group * GROUP_WIDTH + slab * LANES
    return slice(lo, lo + LANES)


def _trailing_sum(ext_ref, slab, start, n_rows, width):
    acc = ext_ref[slab, start:start + n_rows, :]
    for k in range(1, width):
        acc = acc + ext_ref[slab, start - k:start - k + n_rows, :]
    return acc


def _causal_taps(ext_ref, slab, w_ref, hist, n_taps, r):
    first = hist + r - (n_taps - 1)
    lanes = slice(slab * LANES, (slab + 1) * LANES)
    acc = None
    for k in range(n_taps):
        term = ext_ref[slab, first + k:first + k + ROW_BLOCK, :] * w_ref[k:k + 1, lanes]
        acc = term if acc is None else acc + term
    return acc


def _gate_rows(r, z_ref, y_ref, vn_ref, swcat_ref, sbias_ref):
    rows = slice(r, r + ROW_BLOCK)
    gw = GROUP_WIDTH
    sg = _dot(swcat_ref[...], _gating_rhs(vn_ref[rows, :])) + sbias_ref[...]
    y_ref[rows, 2 * gw:3 * gw] = (z_ref[rows, 3 * gw:4 * gw] * sg).astype(BF16)


def _mix_rows(r, seq_start, z_ref, y_ref, vn_ref, pool_ext, conv_ext, short_ext, wpool_ref,
              pscale_ref, pinv_first_ref, pinv_rest_ref, convw_ref, convb_ref, clng_ref, clnb_ref,
              slng_ref, slnb_ref, shortw_ref):
    rows = slice(r, r + ROW_BLOCK)
    gw = GROUP_WIDTH
    low_head = lax.broadcasted_iota(jnp.int32, (ROW_BLOCK, LANES), 1) < SUB_DIM

    d = []
    for s in range(N_SLABS):
        lanes = slice(s * LANES, (s + 1) * LANES)
        a = z_ref[rows, _cols(0, s)]
        base = POOL_HIST + r
        pool_ext[s, base:base + ROW_BLOCK, :] = a
        small, large = POOL_WINDOWS[2 * s], POOL_WINDOWS[2 * s + 1]
        assert large == 2 * small
        if small % SUBLANES == 0:
            run = _trailing_sum(pool_ext, s, base - small, ROW_BLOCK + small, small)
            s_small = run[small:]
            s_large = s_small + run[:ROW_BLOCK]
        else:
            s_small = _trailing_sum(pool_ext, s, base, ROW_BLOCK, small)
            s_large = s_small + _trailing_sum(pool_ext, s, base - small, ROW_BLOCK, small)
        win = jnp.where(low_head, s_small, s_large)
        inv = pinv_rest_ref[:, lanes]
        if seq_start is not None:
            inv = jnp.where(seq_start, pinv_first_ref[:, lanes], inv)
        d.append(win * inv - a)
    d = jnp.concatenate(d, axis=1).astype(BF16)
    y_ref[rows, 0:gw] = (_dot(d, wpool_ref[...]) * pscale_ref[...]).astype(BF16)

    c = []
    for s in range(N_SLABS):
        g = z_ref[rows, _cols(1, s)] * jax.nn.sigmoid(z_ref[rows, _cols(2, s)])
        conv_ext[s, CONV_HIST + r:CONV_HIST + r + ROW_BLOCK, :] = g
        c.append(_causal_taps(conv_ext, s, convw_ref, CONV_HIST, CONV_WIDTH, r))
    c = jnp.concatenate(c, axis=1) + convb_ref[...]
    c = _head_layer_norm(c, clng_ref[...], clnb_ref[...])
    y_ref[rows, gw:2 * gw] = (c * jax.nn.sigmoid(c)).astype(BF16)

    vn_ref[rows, :] = _head_layer_norm(z_ref[rows, 4 * gw:5 * gw], slng_ref[...], slnb_ref[...])

    for s in range(N_SLABS):
        ch = z_ref[rows, _cols(6, s)] * z_ref[rows, _cols(7, s)]
        short_ext[s, SHORT_HIST + r:SHORT_HIST + r + ROW_BLOCK, :] = ch
        sc = _causal_taps(short_ext, s, shortw_ref, SHORT_HIST, SHORT_WIDTH, r)
        y_ref[rows, _cols(3, s)] = (z_ref[rows, _cols(5, s)] * sc).astype(BF16)


def _layer_views(layer, *refs):
    return [r.at[pl.ds(layer, 1)] if len(r.shape) == 2 else r.at[layer] for r in refs]


def _prompt_mixer_kernel(layer, x_ref, win_ref, wout_ref, pinv_first_ref, pinv_rest_ref,
                         gpre_ref, gpost_ref, wpool_ref, pscale_ref, convw_ref, convb_ref, clng_ref,
                         clnb_ref, slng_ref, slnb_ref, sguw_ref, sbias_ref, shortw_ref,
                         wup_f32_ref, wdown_f32_ref,
                         out_ref, pool_out_ref, conv_out_ref, short_out_ref, wup_ref, wdown_ref,
                         z_ref, y_ref, vn_ref, swcat_ref, h_ref, o_ref, pool_ext, conv_ext, short_ext):
    (gpre_ref, gpost_ref, wpool_ref, pscale_ref, convw_ref, convb_ref, clng_ref, clnb_ref, slng_ref,
     slnb_ref, sguw_ref, sbias_ref, shortw_ref) = _layer_views(
         layer, gpre_ref, gpost_ref, wpool_ref, pscale_ref, convw_ref, convb_ref, clng_ref, clnb_ref,
         slng_ref, slnb_ref, sguw_ref, sbias_ref, shortw_ref)
    j = pl.program_id(1)
    tile = x_ref.shape[0]
    wup_ref[...] = wup_f32_ref[...].astype(BF16)
    wdown_ref[...] = wdown_f32_ref[...].astype(BF16)
    causal = (lax.broadcasted_iota(jnp.int32, (CHUNK, CHUNK), 0)
              >= lax.broadcasted_iota(jnp.int32, (CHUNK, CHUNK), 1))
    for hd in range(N_SUB):
        swcat_ref[:, hd * CHUNK:(hd + 1) * CHUNK] = jnp.where(causal, sguw_ref[hd], 0.0).astype(BF16)

    @pl.when(j == 0)
    def _():
        pool_ext[:, 0:POOL_HIST, :] = jnp.zeros((N_SLABS, POOL_HIST, LANES), F32)
        conv_ext[:, 0:CONV_HIST, :] = jnp.zeros((N_SLABS, CONV_HIST, LANES), F32)
        short_ext[:, 0:SHORT_HIST, :] = jnp.zeros((N_SLABS, SHORT_HIST, LANES), F32)

    gw = GROUP_WIDTH
    in_groups = range(win_ref.shape[1] // gw)
    out_groups = range(wout_ref.shape[1] // gw)

    def block_rows(r0):
        return slice(r0, r0 + MATMUL_ROWS)

    def normalise_in(r0):
        h_ref[block_rows(r0), :] = _rms_norm(x_ref[block_rows(r0), :], gpre_ref[...]).astype(BF16)

    def project_in(r0, g):
        cols = slice(g * gw, (g + 1) * gw)
        z_ref[block_rows(r0), cols] = _dot(h_ref[block_rows(r0), :], win_ref[:, cols])

    def project_out(r0, g):
        cols = slice(g * gw, (g + 1) * gw)
        o_ref[block_rows(r0), cols] = _dot(y_ref[block_rows(r0), :], wout_ref[:, cols])

    def finish(r0):
        rows = block_rows(r0)
        out_ref[rows, :] = x_ref[rows, :] + _rms_norm(o_ref[rows, :], gpost_ref[...])

    def mix(r):
        _mix_rows(r, (j == 0) if r == 0 else None, z_ref, y_ref, vn_ref, pool_ext, conv_ext,
                  short_ext, wpool_ref, pscale_ref, pinv_first_ref, pinv_rest_ref, convw_ref,
                  convb_ref, clng_ref, clnb_ref, slng_ref, slnb_ref, shortw_ref)

    def gate(r):
        _gate_rows(r, z_ref, y_ref, vn_ref, swcat_ref, sbias_ref)

    first, second = range(0, MATMUL_ROWS, ROW_BLOCK)
    normalise_in(0)
    for g in in_groups:
        project_in(0, g)
    for r0 in range(0, tile, MATMUL_ROWS):
        nxt, prv = r0 + MATMUL_ROWS, r0 - MATMUL_ROWS
        half = len(in_groups) // 2
        if nxt < tile:
            normalise_in(nxt)
            for g in in_groups[:half]:
                project_in(nxt, g)
        mix(r0 + first)
        if prv >= 0:
            gate(prv + second)
        if nxt < tile:
            for g in in_groups[half:]:
                project_in(nxt, g)
        if prv >= 0:
            for g in out_groups:
                project_out(prv, g)
            finish(prv)
        mix(r0 + second)
        gate(r0 + first)
    last = tile - MATMUL_ROWS
    gate(last + second)
    for g in out_groups:
        project_out(last, g)
    finish(last)

    pool_ext[:, 0:POOL_HIST, :] = pool_ext[:, tile:tile + POOL_HIST, :]
    conv_ext[:, 0:CONV_HIST, :] = conv_ext[:, tile:tile + CONV_HIST, :]
    short_ext[:, 0:SHORT_HIST, :] = short_ext[:, tile:tile + SHORT_HIST, :]

    @pl.when(j == pl.num_programs(1) - 1)
    def _():
        for s in range(N_SLABS):
            lanes = slice(s * LANES, (s + 1) * LANES)
            pool_out_ref[:, lanes] = pool_ext[s, POOL_HIST - POOL_BUF:POOL_HIST, :]
            conv_out_ref[:, lanes] = conv_ext[s, CONV_HIST - (CONV_WIDTH - 1):CONV_HIST, :]
            short_out_ref[:, lanes] = short_ext[s, SHORT_HIST - (SHORT_WIDTH - 1):SHORT_HIST, :]


def _sample_mixer_kernel(layer, x_ref, pool_in_ref, conv_in_ref, short_in_ref, win_ref, wout_ref,
                         gpre_ref, gpost_ref, wpool_ref, pscale_ref, convw_ref, convb_ref, clng_ref,
                         clnb_ref, slng_ref, slnb_ref, sgw_ref, sgb_ref, shortw_ref,
                         out_ref, pool_out_ref, conv_out_ref, short_out_ref, v_out_ref,
                         z_ref, y_ref):
    (gpre_ref, gpost_ref, wpool_ref, pscale_ref, convw_ref, convb_ref, clng_ref, clnb_ref, slng_ref,
     slnb_ref, sgw_ref, sgb_ref, shortw_ref) = _layer_views(
         layer, gpre_ref, gpost_ref, wpool_ref, pscale_ref, convw_ref, convb_ref, clng_ref, clnb_ref,
         slng_ref, slnb_ref, sgw_ref, sgb_ref, shortw_ref)
    n_steps, n_seq = v_out_ref.shape[0], v_out_ref.shape[1]
    gw = GROUP_WIDTH
    x = x_ref[...]
    h = _rms_norm(x, gpre_ref[...]).astype(BF16)
    z_ref[...] = _dot(h, win_ref[...])
    grp, w = _pool_window((n_seq, gw))
    cnt = jnp.minimum(w, PAST_LEN + 1).astype(F32)

    def slab(t):
        return slice(t * n_seq, (t + 1) * n_seq)

    a_new = [z_ref[slab(t), 0:gw] for t in range(n_steps)]
    g_new = [z_ref[slab(t), gw:2 * gw] * jax.nn.sigmoid(z_ref[slab(t), 2 * gw:3 * gw])
             for t in range(n_steps)]
    ch_new = [z_ref[slab(t), 6 * gw:7 * gw] * z_ref[slab(t), 7 * gw:8 * gw]
              for t in range(n_steps)]

    def pool_row(i):
        return pool_in_ref[i] if i < POOL_BUF else a_new[i - POOL_BUF]

    def conv_row(i):
        return conv_in_ref[i] if i < CONV_WIDTH - 1 else g_new[i - (CONV_WIDTH - 1)]

    def short_row(i):
        return short_in_ref[i] if i < SHORT_WIDTH - 1 else ch_new[i - (SHORT_WIDTH - 1)]

    vn = []
    for t in range(n_steps):
        rows = slab(t)
        end = POOL_BUF + t
        acc = pool_row(end)
        sums = []
        for k in range(1, max(POOL_WINDOWS)):
            acc = acc + pool_row(end - k)
            if k + 1 in POOL_WINDOWS:
                sums.append(acc)
        win = sums[0]
        for gi in range(1, N_SUB):
            win = jnp.where(grp == gi, sums[gi], win)
        d = win / cnt - a_new[t]
        y_ref[rows, 0:gw] = (_dot(d.astype(BF16), wpool_ref[...]) * pscale_ref[...]).astype(BF16)

        c = None
        for k in range(CONV_WIDTH):
            term = conv_row(t + k) * convw_ref[k:k + 1, :]
            c = term if c is None else c + term
        c = _head_layer_norm(c + convb_ref[...], clng_ref[...], clnb_ref[...])
        y_ref[rows, gw:2 * gw] = (c * jax.nn.sigmoid(c)).astype(BF16)

        vn.append(_head_layer_norm(z_ref[rows, 4 * gw:5 * gw], slng_ref[...], slnb_ref[...]))
        v_out_ref[t] = vn[t]
        s = sgb_ref[t:t + 1, :]
        for u in range(t + 1):
            s = s + sgw_ref[t * n_steps + u:t * n_steps + u + 1, :] * vn[u]
        y_ref[rows, 2 * gw:3 * gw] = (z_ref[rows, 3 * gw:4 * gw] * s).astype(BF16)

        sc = None
        for k in range(SHORT_WIDTH):
            term = short_row(t + k) * shortw_ref[k:k + 1, :]
            sc = term if sc is None else sc + term
        y_ref[rows, 3 * gw:4 * gw] = (z_ref[rows, 5 * gw:6 * gw] * sc).astype(BF16)

    o = _dot(y_ref[...], wout_ref[...])
    out_ref[...] = x + _rms_norm(o, gpost_ref[...])

    for i in range(POOL_BUF):
        pool_out_ref[i] = pool_row(i + n_steps)
    for i in range(CONV_WIDTH - 1):
        conv_out_ref[i] = conv_row(i + n_steps)
    for i in range(SHORT_WIDTH - 1):
        short_out_ref[i] = short_row(i + n_steps)


def _mlp_rows(x_ref, out_ref, acc_ref, gpre_ref, gpost_ref, wup_ref, wdown_ref):
    for r0 in range(0, x_ref.shape[0], FFN_ROWS):
        rows = slice(r0, r0 + FFN_ROWS)
        x = x_ref[rows, :]
        f = _rms_norm(x, gpre_ref[...]).astype(BF16)
        for c in range(D_FF // FF_CHUNK):
            cols = slice(c * FF_CHUNK, (c + 1) * FF_CHUNK)
            u = jnp.maximum(_dot(f, wup_ref[:, cols]), 0.0)
            part = _dot((u * u).astype(BF16), wdown_ref[cols, :])
            if c == 0:
                acc_ref[rows, :] = part
            else:
                acc_ref[rows, :] += part
        out_ref[rows, :] = x + _rms_norm(acc_ref[rows, :], gpost_ref[...])


def _ffn_kernel(layer, n_cast, x_ref, xs_ref, gpre_ref, gpost_ref, wup_ref, wdown_ref, *rest):
    cast_in, out_ref, outs_ref = rest[:n_cast], rest[n_cast], rest[n_cast + 1]
    cast_out, acc_ref = rest[n_cast + 2:-1], rest[-1]
    gpre_ref, gpost_ref = _layer_views(layer, gpre_ref, gpost_ref)
    i = pl.program_id(0)
    last = pl.num_programs(0) - 1

    @pl.when(i < last)
    def _():
        for src, dst in zip(cast_in, cast_out):
            dst[...] = src[...].astype(BF16)
        _mlp_rows(x_ref, out_ref, acc_ref, gpre_ref, gpost_ref, wup_ref, wdown_ref)

    @pl.when(i == last)
    def _():
        _mlp_rows(xs_ref, outs_ref, acc_ref, gpre_ref, gpost_ref, wup_ref, wdown_ref)


def _const_spec(shape):
    nd = len(shape)
    return pl.BlockSpec(shape, lambda *_: (0,) * nd, pipeline_mode=pl.Buffered(1))


def _whole_spec(shape):
    nd = len(shape)
    return pl.BlockSpec(shape, lambda *_: (0,) * nd)


def _compiler_params(semantics):
    return pltpu.CompilerParams(dimension_semantics=semantics, vmem_limit_bytes=VMEM_LIMIT_BYTES)


def _prompt_mixer(x, p, w_in, w_out, w_ffn_up, w_ffn_down, layer):
    b, s, d = x.shape
    tile = PROMPT_TILE
    consts = (w_in, w_out, POOL_INV_FIRST, POOL_INV_REST,
              p["gpre"], p["gpost"], p["wpool"], p["pscale"], p["conv_w"], p["conv_b"], p["cln_g"],
              p["cln_b"], p["sln_g"], p["sln_b"], p["sgu_w"], p["sgu_bias"], p["short_w"])
    gw = GROUP_WIDTH
    tiles_per_seq = s // tile
    steps = b * tiles_per_seq
    d_ff = w_ffn_up.shape[2]
    ff_slice = d_ff // steps
    assert ff_slice * steps == d_ff and ff_slice % LANES == 0
    step = lambda bi, j: bi * tiles_per_seq + j
    state_spec = lambda n: pl.BlockSpec((None, n, gw), lambda bi, j: (bi, 0, 0))
    return pl.pallas_call(
        functools.partial(_prompt_mixer_kernel, layer),
        grid=(b, tiles_per_seq),
        in_specs=[pl.BlockSpec((None, tile, d), lambda bi, j: (bi, j, 0))]
        + [_const_spec(c.shape) for c in consts]
        + [pl.BlockSpec((None, d, ff_slice), lambda bi, j: (layer, 0, step(bi, j))),
           pl.BlockSpec((None, ff_slice, d), lambda bi, j: (layer, step(bi, j), 0))],
        out_specs=[pl.BlockSpec((None, tile, d), lambda bi, j: (bi, j, 0)),
                   state_spec(POOL_BUF), state_spec(CONV_WIDTH - 1), state_spec(SHORT_WIDTH - 1),
                   pl.BlockSpec((d, ff_slice), lambda bi, j: (0, step(bi, j))),
                   pl.BlockSpec((ff_slice, d), lambda bi, j: (step(bi, j), 0))],
        out_shape=[jax.ShapeDtypeStruct((b, s, d), F32),
                   jax.ShapeDtypeStruct((b, POOL_BUF, gw), F32),
                   jax.ShapeDtypeStruct((b, CONV_WIDTH - 1, gw), F32),
                   jax.ShapeDtypeStruct((b, SHORT_WIDTH - 1, gw), F32),
                   jax.ShapeDtypeStruct((d, d_ff), BF16),
                   jax.ShapeDtypeStruct((d_ff, d), BF16)],
        scratch_shapes=[pltpu.VMEM((tile, IN_WIDTH), F32),
                        pltpu.VMEM((tile, d), BF16),
                        pltpu.VMEM((tile, gw), F32),
                        pltpu.VMEM((CHUNK, N_SUB * CHUNK), BF16),
                        pltpu.VMEM((tile, d), BF16),
                        pltpu.VMEM((tile, d), F32),
                        pltpu.VMEM((N_SLABS, POOL_HIST + tile, LANES), F32),
                        pltpu.VMEM((N_SLABS, CONV_HIST + tile, LANES), F32),
                        pltpu.VMEM((N_SLABS, SHORT_HIST + tile, LANES), F32)],
        compiler_params=_compiler_params(("arbitrary", "arbitrary")),
        name="prompt_mixer",
    )(x, *consts, w_ffn_up, w_ffn_down)


def _sample_mixer(x, pool_st, conv_st, short_st, p, w_in, w_out, layer):
    n, d = x.shape
    n_seq = pool_st.shape[2]
    n_steps = n // n_seq
    gw = GROUP_WIDTH
    states = (pool_st, conv_st, short_st)
    consts = (w_in, w_out, p["gpre"], p["gpost"], p["wpool"], p["pscale"], p["conv_w"], p["conv_b"],
              p["cln_g"], p["cln_b"], p["sln_g"], p["sln_b"], p["sgu_w4"], p["sgu_bias"], p["short_w"])
    state_spec = lambda st: pl.BlockSpec((None,) + st.shape[1:], lambda i: (layer, 0, 0, 0))
    out_shape = ([jax.ShapeDtypeStruct((n, d), F32)]
                 + [jax.ShapeDtypeStruct(st.shape[1:], F32) for st in states]
                 + [jax.ShapeDtypeStruct((n_steps, n_seq, gw), F32)])
    return pl.pallas_call(
        functools.partial(_sample_mixer_kernel, layer),
        grid=(1,),
        in_specs=[_whole_spec(x.shape)] + [state_spec(st) for st in states]
        + [_whole_spec(c.shape) for c in consts],
        out_specs=[_whole_spec(o.shape) for o in out_shape],
        out_shape=out_shape,
        scratch_shapes=[pltpu.VMEM((n, IN_WIDTH), F32), pltpu.VMEM((n, d), BF16)],
        compiler_params=_compiler_params(("arbitrary",)),
        name="sample_mixer",
    )(x, *states, *consts)


def _ffn(x, xs, p, w_up, w_down, layer, cast=()):
    n, d = x.shape
    ns = xs.shape[0]
    tile = FFN_TILE
    steps = n // tile
    assert steps * tile == n and ns <= tile and ns % FFN_ROWS == 0
    consts = (p["fpre"], p["fpost"], w_up, w_down)
    prompt_tile = lambda i: (jnp.minimum(i, steps - 1), 0)
    cast_in_specs, cast_out_specs, cast_shapes = [], [], []
    for w, w_layer in cast:
        rows = w.shape[1] // steps
        assert rows * steps == w.shape[1] and rows % (2 * SUBLANES) == 0
        cast_in_specs.append(pl.BlockSpec(
            (None, rows, w.shape[2]), lambda i, w_layer=w_layer: (w_layer,) + prompt_tile(i)))
        cast_out_specs.append(pl.BlockSpec((rows, w.shape[2]), prompt_tile))
        cast_shapes.append(jax.ShapeDtypeStruct(w.shape[1:], BF16))
    return pl.pallas_call(
        functools.partial(_ffn_kernel, layer, len(cast)),
        grid=(steps + 1,),
        in_specs=[pl.BlockSpec((tile, d), prompt_tile), _whole_spec(xs.shape)]
        + [_const_spec(c.shape) for c in consts] + cast_in_specs,
        out_specs=[pl.BlockSpec((tile, d), prompt_tile), _whole_spec(xs.shape)] + cast_out_specs,
        out_shape=[jax.ShapeDtypeStruct((n, d), F32), jax.ShapeDtypeStruct((ns, d), F32)] + cast_shapes,
        scratch_shapes=[pltpu.VMEM((tile, d), F32)],
        compiler_params=_compiler_params(("arbitrary",)),
        name="ffn",
    )(x, xs, *consts, *[w for w, _ in cast])


def _pool_inverse_counts():
    window = np.repeat(np.asarray(POOL_WINDOWS, np.float32), SUB_DIM)[None, :]
    first = 1.0 / np.minimum(window, np.arange(1, ROW_BLOCK + 1, dtype=np.float32)[:, None])
    return first.astype(np.float32), (1.0 / window).astype(np.float32)


POOL_INV_FIRST, POOL_INV_REST = _pool_inverse_counts()


def _stacked_params(n_steps, norm_mix_pre, norm_mix_post, norm_ffn_pre, norm_ffn_post, w_pool,
                    pool_scale, conv_w, conv_b, conv_ln_g, conv_ln_b, sgu_ln_g, sgu_ln_b, sgu_w,
                    sgu_b, short_w):
    depth = w_pool.shape[0]
    gw = GROUP_WIDTH
    head_of_lane = np.arange(gw) // SUB_DIM
    same_head = head_of_lane[:, None] == head_of_lane[None, :]
    wp = jnp.tile(w_pool.reshape(depth, gw, SUB_DIM), (1, 1, N_SUB))
    wpool = jnp.where(same_head[None], wp, 0.0).astype(BF16)
    bias = jnp.repeat(jnp.swapaxes(sgu_b, 1, 2), SUB_DIM, axis=2)
    causal = np.tril(np.ones((n_steps, n_steps), dtype=bool))
    w4 = jnp.where(causal[None, None], sgu_w[:, :, :n_steps, :n_steps], 0.0)
    w4 = jnp.repeat(jnp.transpose(w4, (0, 2, 3, 1)), SUB_DIM, axis=3)
    w4 = w4.reshape(depth, n_steps * n_steps, gw)
    return dict(gpre=norm_mix_pre, gpost=norm_mix_post, fpre=norm_ffn_pre, fpost=norm_ffn_post,
                wpool=wpool, pscale=pool_scale, conv_w=conv_w, conv_b=conv_b, cln_g=conv_ln_g,
                cln_b=conv_ln_b, sln_g=sgu_ln_g, sln_b=sgu_ln_b, sgu_w=sgu_w, sgu_bias=bias,
                sgu_w4=w4, short_w=short_w)


def kernel(x_prompt, x_sample, state_pool, state_conv, state_short, norm_mix_pre, norm_mix_post, norm_ffn_pre, norm_ffn_post, w_in, w_out, w_pool, pool_scale, conv_w, conv_b, conv_ln_g, conv_ln_b, sgu_ln_g, sgu_ln_b, sgu_w, sgu_b, short_w, w_ffn_up, w_ffn_down):
    depth = w_in.shape[0]
    bp, seq, d = x_prompt.shape
    n_seq, n_steps, _ = x_sample.shape
    assert seq % PROMPT_TILE == 0 and PROMPT_TILE % MATMUL_ROWS == 0 and MATMUL_ROWS == 2 * ROW_BLOCK
    assert ROW_BLOCK >= max(POOL_WINDOWS) and n_steps <= CHUNK and PAST_LEN % CHUNK == 0

    p = _stacked_params(n_steps, norm_mix_pre, norm_mix_post, norm_ffn_pre, norm_ffn_post, w_pool,
                        pool_scale, conv_w, conv_b, conv_ln_g, conv_ln_b, sgu_ln_g, sgu_ln_b, sgu_w,
                        sgu_b, short_w)
    yp = x_prompt
    ys = jnp.transpose(x_sample, (1, 0, 2)).reshape(n_steps * n_seq, d)
    hist_major = lambda a: jnp.transpose(a, (0, 2, 1, 3))
    pool_in, conv_in, short_in = hist_major(state_pool), hist_major(state_conv), hist_major(state_short)
    prompt_states = [[] for _ in range(3)]
    sample_states = [[] for _ in range(4)]
    w_in_b, w_out_b = w_in[0].astype(BF16), w_out[0].astype(BF16)
    for l in range(depth):
        yp, pool_p, conv_p, short_p, w_up_b, w_down_b = _prompt_mixer(
            yp, p, w_in_b, w_out_b, w_ffn_up, w_ffn_down, l)
        ys, pool_s, conv_s, short_s, v_s = _sample_mixer(
            ys, pool_in, conv_in, short_in, p, w_in_b, w_out_b, l)
        cast = ((w_in, l + 1), (w_out, l + 1)) if l + 1 < depth else ()
        yp, ys, *w_next = _ffn(yp.reshape(bp * seq, d), ys, p, w_up_b, w_down_b, l, cast)
        yp = yp.reshape(bp, seq, d)
        if w_next:
            w_in_b, w_out_b = w_next

        for lst, val in zip(prompt_states, (pool_p, conv_p, short_p)):
            lst.append(val)
        for lst, val in zip(sample_states, (pool_s, conv_s, short_s, v_s)):
            lst.append(val)

    ys = jnp.transpose(ys.reshape(n_steps, n_seq, d), (1, 0, 2))
    pool_p, conv_p, short_p = (jnp.stack(o) for o in prompt_states)
    pool_s, conv_s, short_s, v_s = (hist_major(jnp.stack(o)) for o in sample_states)
    return (yp, ys, pool_p, pool_s, conv_p, conv_s, short_p, short_s, v_s)
```

```python
import functools

import jax
import jax.numpy as jnp
import numpy as np
from jax import lax
from jax.experimental import pallas as pl
from jax.experimental.pallas import tpu as pltpu

D_MODEL = 1024
GROUP_WIDTH = 256
N_SUB = 4
SUB_DIM = 64
POOL_WINDOWS = (2, 4, 8, 16)
POOL_BUF = 15
CONV_WIDTH = 31
SHORT_WIDTH = 3
CHUNK = 128
D_FF = 4096
EPS = 1e-6
PAST_LEN = 16384
IN_WIDTH = 8 * GROUP_WIDTH

SUBLANES = 8
LANES = 128
N_SLABS = GROUP_WIDTH // LANES
POOL_HIST = 16
CONV_HIST = 32
SHORT_HIST = 8

ROW_BLOCK = CHUNK
MATMUL_ROWS = 1024
PROMPT_TILE = 1024
FFN_TILE = 1024
FFN_ROWS = 512
FF_CHUNK = 1024
VMEM_LIMIT_BYTES = 56 * 1024 * 1024

F32 = jnp.float32
BF16 = jnp.bfloat16


def _rms_norm(x, g):
    ms = jnp.mean(x * x, axis=-1, keepdims=True)
    return x * lax.rsqrt(ms + EPS) * g


def _dot(a, b):
    return jnp.dot(a, b, preferred_element_type=F32)


def _head_mean(x, low_head):
    s_low = jnp.sum(jnp.where(low_head, x, 0.0), axis=-1, keepdims=True)
    s_high = jnp.sum(jnp.where(low_head, 0.0, x), axis=-1, keepdims=True)
    return jnp.where(low_head, s_low, s_high) * (1.0 / SUB_DIM)


def _head_layer_norm(x, g, b):
    low_head = lax.broadcasted_iota(jnp.int32, (x.shape[0], LANES), 1) < SUB_DIM
    out = []
    for s in range(N_SLABS):
        lanes = slice(s * LANES, (s + 1) * LANES)
        xs = x[:, lanes]
        xc = xs - _head_mean(xs, low_head)
        var = _head_mean(xc * xc, low_head)
        out.append(xc * lax.rsqrt(var + EPS) * g[:, lanes] + b[:, lanes])
    return jnp.concatenate(out, axis=1)


def _lane_group(shape):
    return jnp.right_shift(lax.broadcasted_iota(jnp.int32, shape, 1), SUB_DIM.bit_length() - 1)


def _pool_window(shape):
    grp = _lane_group(shape)
    w = jnp.full(shape, POOL_WINDOWS[0], jnp.int32)
    for gi in range(1, N_SUB):
        w = jnp.where(grp == gi, POOL_WINDOWS[gi], w)
    return grp, w


def _gating_rhs(vn):
    grp = _lane_group(vn.shape)
    return jnp.concatenate([jnp.where(grp == h, vn, 0.0).astype(BF16) for h in range(N_SUB)], axis=0)


def _cols(group, slab):
    lo = group * GROUP_WIDTH + slab * LANES
    return slice(lo, lo + LANES)


def _trailing_sum(ext_ref, slab, start, n_rows, width):
    acc = ext_ref[slab, start:start + n_rows, :]
    for k in range(1, width):
        acc = acc + ext_ref[slab, start - k:start - k + n_rows, :]
    return acc


def _causal_taps(ext_ref, slab, w_ref, hist, n_taps, r):
    first = hist + r - (n_taps - 1)
    lanes = slice(slab * LANES, (slab + 1) * LANES)
    acc = None
    for k in range(n_taps):
        term = ext_ref[slab, first + k:first + k + ROW_BLOCK, :] * w_ref[k:k + 1, lanes]
        acc = term if acc is None else acc + term
    return acc


def _gate_rows(r, z_ref, y_ref, vn_ref, swcat_ref, sbias_ref):
    rows = slice(r, r + ROW_BLOCK)
    gw = GROUP_WIDTH
    sg = _dot(swcat_ref[...], _gating_rhs(vn_ref[rows, :])) + sbias_ref[...]
    y_ref[rows, 2 * gw:3 * gw] = (z_ref[rows, 3 * gw:4 * gw] * sg).astype(BF16)


def _mix_rows(r, seq_start, z_ref, y_ref, vn_ref, pool_ext, conv_ext, short_ext, wpool_ref,
              pscale_ref, pinv_first_ref, pinv_rest_ref, convw_ref, convb_ref, clng_ref, clnb_ref,
              slng_ref, slnb_ref, shortw_ref):
    rows = slice(r, r + ROW_BLOCK)
    gw = GROUP_WIDTH
    low_head = lax.broadcasted_iota(jnp.int32, (ROW_BLOCK, LANES), 1) < SUB_DIM

    d = []
    for s in range(N_SLABS):
        lanes = slice(s * LANES, (s + 1) * LANES)
        a = z_ref[rows, _cols(0, s)]
        base = POOL_HIST + r
        pool_ext[s, base:base + ROW_BLOCK, :] = a
        small, large = POOL_WINDOWS[2 * s], POOL_WINDOWS[2 * s + 1]
        assert large == 2 * small
        if small % SUBLANES == 0:
            run = _trailing_sum(pool_ext, s, base - small, ROW_BLOCK + small, small)
            s_small = run[small:]
            s_large = s_small + run[:ROW_BLOCK]
        else:
            s_small = _trailing_sum(pool_ext, s, base, ROW_BLOCK, small)
            s_large = s_small + _trailing_sum(pool_ext, s, base - small, ROW_BLOCK, small)
        win = jnp.where(low_head, s_small, s_large)
        inv = pinv_rest_ref[:, lanes]
        if seq_start is not None:
            inv = jnp.where(seq_start, pinv_first_ref[:, lanes], inv)
        d.append(win * inv - a)
    d = jnp.concatenate(d, axis=1).astype(BF16)
    y_ref[rows, 0:gw] = (_dot(d, wpool_ref[...]) * pscale_ref[...]).astype(BF16)

    c = []
    for s in range(N_SLABS):
        g = z_ref[rows, _cols(1, s)] * jax.nn.sigmoid(z_ref[rows, _cols(2, s)])
        conv_ext[s, CONV_HIST + r:CONV_HIST + r + ROW_BLOCK, :] = g
        c.append(_causal_taps(conv_ext, s, convw_ref, CONV_HIST, CONV_WIDTH, r))
    c = jnp.concatenate(c, axis=1) + convb_ref[...]
    c = _head_layer_norm(c, clng_ref[...], clnb_ref[...])
    y_ref[rows, gw:2 * gw] = (c * jax.nn.sigmoid(c)).astype(BF16)

    vn_ref[rows, :] = _head_layer_norm(z_ref[rows, 4 * gw:5 * gw], slng_ref[...], slnb_ref[...])

    for s in range(N_SLABS):
        ch = z_ref[rows, _cols(6, s)] * z_ref[rows, _cols(7, s)]
        short_ext[s, SHORT_HIST + r:SHORT_HIST + r + ROW_BLOCK, :] = ch
        sc = _causal_taps(short_ext, s, shortw_ref, SHORT_HIST, SHORT_WIDTH, r)
        y_ref[rows, _cols(3, s)] = (z_ref[rows, _cols(5, s)] * sc).astype(BF16)


def _layer_views(layer, *refs):
    return [r.at[pl.ds(layer, 1)] if len(r.shape) == 2 else r.at[layer] for r in refs]


def _prompt_mixer_kernel(layer, x_ref, win_ref, wout_ref, pinv_first_ref, pinv_rest_ref,
                         gpre_ref, gpost_ref, wpool_ref, pscale_ref, convw_ref, convb_ref, clng_ref,
                         clnb_ref, slng_ref, slnb_ref, sguw_ref, sbias_ref, shortw_ref,
                         wup_f32_ref, wdown_f32_ref,
                         out_ref, pool_out_ref, conv_out_ref, short_out_ref, wup_ref, wdown_ref,
                         z_ref, y_ref, vn_ref, swcat_ref, pool_ext, conv_ext, short_ext):
    (gpre_ref, gpost_ref, wpool_ref, pscale_ref, convw_ref, convb_ref, clng_ref, clnb_ref, slng_ref,
     slnb_ref, sguw_ref, sbias_ref, shortw_ref) = _layer_views(
         layer, gpre_ref, gpost_ref, wpool_ref, pscale_ref, convw_ref, convb_ref, clng_ref, clnb_ref,
         slng_ref, slnb_ref, sguw_ref, sbias_ref, shortw_ref)
    j = pl.program_id(1)
    tile = x_ref.shape[0]
    wup_ref[...] = wup_f32_ref[...].astype(BF16)
    wdown_ref[...] = wdown_f32_ref[...].astype(BF16)
    causal = (lax.broadcasted_iota(jnp.int32, (CHUNK, CHUNK), 0)
              >= lax.broadcasted_iota(jnp.int32, (CHUNK, CHUNK), 1))
    for hd in range(N_SUB):
        swcat_ref[:, hd * CHUNK:(hd + 1) * CHUNK] = jnp.where(causal, sguw_ref[hd], 0.0).astype(BF16)

    @pl.when(j == 0)
    def _():
        pool_ext[:, 0:POOL_HIST, :] = jnp.zeros((N_SLABS, POOL_HIST, LANES), F32)
        conv_ext[:, 0:CONV_HIST, :] = jnp.zeros((N_SLABS, CONV_HIST, LANES), F32)
        short_ext[:, 0:SHORT_HIST, :] = jnp.zeros((N_SLABS, SHORT_HIST, LANES), F32)

    def project_out(r0):
        rows = slice(r0, r0 + MATMUL_ROWS)
        o = _dot(y_ref[rows, :], wout_ref[...])
        out_ref[rows, :] = x_ref[rows, :] + _rms_norm(o, gpost_ref[...])

    for r0 in range(0, tile, MATMUL_ROWS):
        rows = slice(r0, r0 + MATMUL_ROWS)
        h = _rms_norm(x_ref[rows, :], gpre_ref[...]).astype(BF16)
        z_ref[rows, :] = _dot(h, win_ref[...])
        for r in range(r0, r0 + MATMUL_ROWS, ROW_BLOCK):
            _mix_rows(r, (j == 0) if r == 0 else None, z_ref, y_ref, vn_ref, pool_ext, conv_ext,
                      short_ext, wpool_ref, pscale_ref, pinv_first_ref, pinv_rest_ref, convw_ref,
                      convb_ref, clng_ref, clnb_ref, slng_ref, slnb_ref, shortw_ref)
            if r > 0:
                _gate_rows(r - ROW_BLOCK, z_ref, y_ref, vn_ref, swcat_ref, sbias_ref)
            if r == r0 and r0 > 0:
                project_out(r0 - MATMUL_ROWS)
    _gate_rows(tile - ROW_BLOCK, z_ref, y_ref, vn_ref, swcat_ref, sbias_ref)
    project_out(tile - MATMUL_ROWS)

    pool_ext[:, 0:POOL_HIST, :] = pool_ext[:, tile:tile + POOL_HIST, :]
    conv_ext[:, 0:CONV_HIST, :] = conv_ext[:, tile:tile + CONV_HIST, :]
    short_ext[:, 0:SHORT_HIST, :] = short_ext[:, tile:tile + SHORT_HIST, :]

    @pl.when(j == pl.num_programs(1) - 1)
    def _():
        for s in range(N_SLABS):
            lanes = slice(s * LANES, (s + 1) * LANES)
            pool_out_ref[:, lanes] = pool_ext[s, POOL_HIST - POOL_BUF:POOL_HIST, :]
            conv_out_ref[:, lanes] = conv_ext[s, CONV_HIST - (CONV_WIDTH - 1):CONV_HIST, :]
            short_out_ref[:, lanes] = short_ext[s, SHORT_HIST - (SHORT_WIDTH - 1):SHORT_HIST, :]


def _sample_mixer_kernel(layer, x_ref, pool_in_ref, conv_in_ref, short_in_ref, win_ref, wout_ref,
                         gpre_ref, gpost_ref, wpool_ref, pscale_ref, convw_ref, convb_ref, clng_ref,
                         clnb_ref, slng_ref, slnb_ref, sgw_ref, sgb_ref, shortw_ref,
                         out_ref, pool_out_ref, conv_out_ref, short_out_ref, v_out_ref,
                         z_ref, y_ref):
    (gpre_ref, gpost_ref, wpool_ref, pscale_ref, convw_ref, convb_ref, clng_ref, clnb_ref, slng_ref,
     slnb_ref, sgw_ref, sgb_ref, shortw_ref) = _layer_views(
         layer, gpre_ref, gpost_ref, wpool_ref, pscale_ref, convw_ref, convb_ref, clng_ref, clnb_ref,
         slng_ref, slnb_ref, sgw_ref, sgb_ref, shortw_ref)
    n_steps, n_seq = v_out_ref.shape[0], v_out_ref.shape[1]
    gw = GROUP_WIDTH
    x = x_ref[...]
    h = _rms_norm(x, gpre_ref[...]).astype(BF16)
    z_ref[...] = _dot(h, win_ref[...])
    grp, w = _pool_window((n_seq, gw))
    cnt = jnp.minimum(w, PAST_LEN + 1).astype(F32)

    def slab(t):
        return slice(t * n_seq, (t + 1) * n_seq)

    a_new = [z_ref[slab(t), 0:gw] for t in range(n_steps)]
    g_new = [z_ref[slab(t), gw:2 * gw] * jax.nn.sigmoid(z_ref[slab(t), 2 * gw:3 * gw])
             for t in range(n_steps)]
    ch_new = [z_ref[slab(t), 6 * gw:7 * gw] * z_ref[slab(t), 7 * gw:8 * gw]
              for t in range(n_steps)]

    def pool_row(i):
        return pool_in_ref[i] if i < POOL_BUF else a_new[i - POOL_BUF]

    def conv_row(i):
        return conv_in_ref[i] if i < CONV_WIDTH - 1 else g_new[i - (CONV_WIDTH - 1)]

    def short_row(i):
        return short_in_ref[i] if i < SHORT_WIDTH - 1 else ch_new[i - (SHORT_WIDTH - 1)]

    vn = []
    for t in range(n_steps):
        rows = slab(t)
        end = POOL_BUF + t
        acc = pool_row(end)
        sums = []
        for k in range(1, max(POOL_WINDOWS)):
            acc = acc + pool_row(end - k)
            if k + 1 in POOL_WINDOWS:
                sums.append(acc)
        win = sums[0]
        for gi in range(1, N_SUB):
            win = jnp.where(grp == gi, sums[gi], win)
        d = win / cnt - a_new[t]
        y_ref[rows, 0:gw] = (_dot(d.astype(BF16), wpool_ref[...]) * pscale_ref[...]).astype(BF16)

        c = None
        for k in range(CONV_WIDTH):
            term = conv_row(t + k) * convw_ref[k:k + 1, :]
            c = term if c is None else c + term
        c = _head_layer_norm(c + convb_ref[...], clng_ref[...], clnb_ref[...])
        y_ref[rows, gw:2 * gw] = (c * jax.nn.sigmoid(c)).astype(BF16)

        vn.append(_head_layer_norm(z_ref[rows, 4 * gw:5 * gw], slng_ref[...], slnb_ref[...]))
        v_out_ref[t] = vn[t]
        s = sgb_ref[t:t + 1, :]
        for u in range(t + 1):
            s = s + sgw_ref[t * n_steps + u:t * n_steps + u + 1, :] * vn[u]
        y_ref[rows, 2 * gw:3 * gw] = (z_ref[rows, 3 * gw:4 * gw] * s).astype(BF16)

        sc = None
        for k in range(SHORT_WIDTH):
            term = short_row(t + k) * shortw_ref[k:k + 1, :]
            sc = term if sc is None else sc + term
        y_ref[rows, 3 * gw:4 * gw] = (z_ref[rows, 5 * gw:6 * gw] * sc).astype(BF16)

    o = _dot(y_ref[...], wout_ref[...])
    out_ref[...] = x + _rms_norm(o, gpost_ref[...])

    for i in range(POOL_BUF):
        pool_out_ref[i] = pool_row(i + n_steps)
    for i in range(CONV_WIDTH - 1):
        conv_out_ref[i] = conv_row(i + n_steps)
    for i in range(SHORT_WIDTH - 1):
        short_out_ref[i] = short_row(i + n_steps)


def _mlp_rows(x_ref, out_ref, acc_ref, gpre_ref, gpost_ref, wup_ref, wdown_ref):
    for r0 in range(0, x_ref.shape[0], FFN_ROWS):
        rows = slice(r0, r0 + FFN_ROWS)
        x = x_ref[rows, :]
        f = _rms_norm(x, gpre_ref[...]).astype(BF16)
        for c in range(D_FF // FF_CHUNK):
            cols = slice(c * FF_CHUNK, (c + 1) * FF_CHUNK)
            u = jnp.maximum(_dot(f, wup_ref[:, cols]), 0.0)
            part = _dot((u * u).astype(BF16), wdown_ref[cols, :])
            if c == 0:
                acc_ref[rows, :] = part
            else:
                acc_ref[rows, :] += part
        out_ref[rows, :] = x + _rms_norm(acc_ref[rows, :], gpost_ref[...])


def _ffn_kernel(layer, n_cast, x_ref, xs_ref, gpre_ref, gpost_ref, wup_ref, wdown_ref, *rest):
    cast_in, out_ref, outs_ref = rest[:n_cast], rest[n_cast], rest[n_cast + 1]
    cast_out, acc_ref = rest[n_cast + 2:-1], rest[-1]
    gpre_ref, gpost_ref = _layer_views(layer, gpre_ref, gpost_ref)
    i = pl.program_id(0)
    last = pl.num_programs(0) - 1

    @pl.when(i < last)
    def _():
        for src, dst in zip(cast_in, cast_out):
            dst[...] = src[...].astype(BF16)
        _mlp_rows(x_ref, out_ref, acc_ref, gpre_ref, gpost_ref, wup_ref, wdown_ref)

    @pl.when(i == last)
    def _():
        _mlp_rows(xs_ref, outs_ref, acc_ref, gpre_ref, gpost_ref, wup_ref, wdown_ref)


def _const_spec(shape):
    nd = len(shape)
    return pl.BlockSpec(shape, lambda *_: (0,) * nd, pipeline_mode=pl.Buffered(1))


def _whole_spec(shape):
    nd = len(shape)
    return pl.BlockSpec(shape, lambda *_: (0,) * nd)


def _compiler_params(semantics):
    return pltpu.CompilerParams(dimension_semantics=semantics, vmem_limit_bytes=VMEM_LIMIT_BYTES)


def _prompt_mixer(x, p, w_in, w_out, w_ffn_up, w_ffn_down, layer):
    b, s, d = x.shape
    tile = PROMPT_TILE
    consts = (w_in, w_out, POOL_INV_FIRST, POOL_INV_REST,
              p["gpre"], p["gpost"], p["wpool"], p["pscale"], p["conv_w"], p["conv_b"], p["cln_g"],
              p["cln_b"], p["sln_g"], p["sln_b"], p["sgu_w"], p["sgu_bias"], p["short_w"])
    gw = GROUP_WIDTH
    tiles_per_seq = s // tile
    steps = b * tiles_per_seq
    d_ff = w_ffn_up.shape[2]
    ff_slice = d_ff // steps
    assert ff_slice * steps == d_ff and ff_slice % LANES == 0
    step = lambda bi, j: bi * tiles_per_seq + j
    state_spec = lambda n: pl.BlockSpec((None, n, gw), lambda bi, j: (bi, 0, 0))
    return pl.pallas_call(
        functools.partial(_prompt_mixer_kernel, layer),
        grid=(b, tiles_per_seq),
        in_specs=[pl.BlockSpec((None, tile, d), lambda bi, j: (bi, j, 0))]
        + [_const_spec(c.shape) for c in consts]
        + [pl.BlockSpec((None, d, ff_slice), lambda bi, j: (layer, 0, step(bi, j))),
           pl.BlockSpec((None, ff_slice, d), lambda bi, j: (layer, step(bi, j), 0))],
        out_specs=[pl.BlockSpec((None, tile, d), lambda bi, j: (bi, j, 0)),
                   state_spec(POOL_BUF), state_spec(CONV_WIDTH - 1), state_spec(SHORT_WIDTH - 1),
                   pl.BlockSpec((d, ff_slice), lambda bi, j: (0, step(bi, j))),
                   pl.BlockSpec((ff_slice, d), lambda bi, j: (step(bi, j), 0))],
        out_shape=[jax.ShapeDtypeStruct((b, s, d), F32),
                   jax.ShapeDtypeStruct((b, POOL_BUF, gw), F32),
                   jax.ShapeDtypeStruct((b, CONV_WIDTH - 1, gw), F32),
                   jax.ShapeDtypeStruct((b, SHORT_WIDTH - 1, gw), F32),
                   jax.ShapeDtypeStruct((d, d_ff), BF16),
                   jax.ShapeDtypeStruct((d_ff, d), BF16)],
        scratch_shapes=[pltpu.VMEM((tile, IN_WIDTH), F32),
                        pltpu.VMEM((tile, d), BF16),
                        pltpu.VMEM((tile, gw), F32),
                        pltpu.VMEM((CHUNK, N_SUB * CHUNK), BF16),
                        pltpu.VMEM((N_SLABS, POOL_HIST + tile, LANES), F32),
                        pltpu.VMEM((N_SLABS, CONV_HIST + tile, LANES), F32),
                        pltpu.VMEM((N_SLABS, SHORT_HIST + tile, LANES), F32)],
        compiler_params=_compiler_params(("arbitrary", "arbitrary")),
        name="prompt_mixer",
    )(x, *consts, w_ffn_up, w_ffn_down)


def _sample_mixer(x, pool_st, conv_st, short_st, p, w_in, w_out, layer):
    n, d = x.shape
    n_seq = pool_st.shape[2]
    n_steps = n // n_seq
    gw = GROUP_WIDTH
    states = (pool_st, conv_st, short_st)
    consts = (w_in, w_out, p["gpre"], p["gpost"], p["wpool"], p["pscale"], p["conv_w"], p["conv_b"],
              p["cln_g"], p["cln_b"], p["sln_g"], p["sln_b"], p["sgu_w4"], p["sgu_bias"], p["short_w"])
    state_spec = lambda st: pl.BlockSpec((None,) + st.shape[1:], lambda i: (layer, 0, 0, 0))
    out_shape = ([jax.ShapeDtypeStruct((n, d), F32)]
                 + [jax.ShapeDtypeStruct(st.shape[1:], F32) for st in states]
                 + [jax.ShapeDtypeStruct((n_steps, n_seq, gw), F32)])
    return pl.pallas_call(
        functools.partial(_sample_mixer_kernel, layer),
        grid=(1,),
        in_specs=[_whole_spec(x.shape)] + [state_spec(st) for st in states]
        + [_whole_spec(c.shape) for c in consts],
        out_specs=[_whole_spec(o.shape) for o in out_shape],
        out_shape=out_shape,
        scratch_shapes=[pltpu.VMEM((n, IN_WIDTH), F32), pltpu.VMEM((n, d), BF16)],
        compiler_params=_compiler_params(("arbitrary",)),
        name="sample_mixer",
    )(x, *states, *consts)


def _ffn(x, xs, p, w_up, w_down, layer, cast=()):
    n, d = x.shape
    ns = xs.shape[0]
    tile = FFN_TILE
    steps = n // tile
    assert steps * tile == n and ns <= tile and ns % FFN_ROWS == 0
    consts = (p["fpre"], p["fpost"], w_up, w_down)
    prompt_tile = lambda i: (jnp.minimum(i, steps - 1), 0)
    cast_in_specs, cast_out_specs, cast_shapes = [], [], []
    for w, w_layer in cast:
        rows = w.shape[1] // steps
        assert rows * steps == w.shape[1] and rows % (2 * SUBLANES) == 0
        cast_in_specs.append(pl.BlockSpec(
            (None, rows, w.shape[2]), lambda i, w_layer=w_layer: (w_layer,) + prompt_tile(i)))
        cast_out_specs.append(pl.BlockSpec((rows, w.shape[2]), prompt_tile))
        cast_shapes.append(jax.ShapeDtypeStruct(w.shape[1:], BF16))
    return pl.pallas_call(
        functools.partial(_ffn_kernel, layer, len(cast)),
        grid=(steps + 1,),
        in_specs=[pl.BlockSpec((tile, d), prompt_tile), _whole_spec(xs.shape)]
        + [_const_spec(c.shape) for c in consts] + cast_in_specs,
        out_specs=[pl.BlockSpec((tile, d), prompt_tile), _whole_spec(xs.shape)] + cast_out_specs,
        out_shape=[jax.ShapeDtypeStruct((n, d), F32), jax.ShapeDtypeStruct((ns, d), F32)] + cast_shapes,
        scratch_shapes=[pltpu.VMEM((tile, d), F32)],
        compiler_params=_compiler_params(("arbitrary",)),
        name="ffn",
    )(x, xs, *consts, *[w for w, _ in cast])


def _pool_inverse_counts():
    window = np.repeat(np.asarray(POOL_WINDOWS, np.float32), SUB_DIM)[None, :]
    first = 1.0 / np.minimum(window, np.arange(1, ROW_BLOCK + 1, dtype=np.float32)[:, None])
    return first.astype(np.float32), (1.0 / window).astype(np.float32)


POOL_INV_FIRST, POOL_INV_REST = _pool_inverse_counts()


def _stacked_params(n_steps, norm_mix_pre, norm_mix_post, norm_ffn_pre, norm_ffn_post, w_pool,
                    pool_scale, conv_w, conv_b, conv_ln_g, conv_ln_b, sgu_ln_g, sgu_ln_b, sgu_w,
                    sgu_b, short_w):
    depth = w_pool.shape[0]
    gw = GROUP_WIDTH
    head_of_lane = np.arange(gw) // SUB_DIM
    same_head = head_of_lane[:, None] == head_of_lane[None, :]
    wp = jnp.tile(w_pool.reshape(depth, gw, SUB_DIM), (1, 1, N_SUB))
    wpool = jnp.where(same_head[None], wp, 0.0).astype(BF16)
    bias = jnp.repeat(jnp.swapaxes(sgu_b, 1, 2), SUB_DIM, axis=2)
    causal = np.tril(np.ones((n_steps, n_steps), dtype=bool))
    w4 = jnp.where(causal[None, None], sgu_w[:, :, :n_steps, :n_steps], 0.0)
    w4 = jnp.repeat(jnp.transpose(w4, (0, 2, 3, 1)), SUB_DIM, axis=3)
    w4 = w4.reshape(depth, n_steps * n_steps, gw)
    return dict(gpre=norm_mix_pre, gpost=norm_mix_post, fpre=norm_ffn_pre, fpost=norm_ffn_post,
                wpool=wpool, pscale=pool_scale, conv_w=conv_w, conv_b=conv_b, cln_g=conv_ln_g,
                cln_b=conv_ln_b, sln_g=sgu_ln_g, sln_b=sgu_ln_b, sgu_w=sgu_w, sgu_bias=bias,
                sgu_w4=w4, short_w=short_w)


def kernel(x_prompt, x_sample, state_pool, state_conv, state_short, norm_mix_pre, norm_mix_post, norm_ffn_pre, norm_ffn_post, w_in, w_out, w_pool, pool_scale, conv_w, conv_b, conv_ln_g, conv_ln_b, sgu_ln_g, sgu_ln_b, sgu_w, sgu_b, short_w, w_ffn_up, w_ffn_down):
    depth = w_in.shape[0]
    bp, seq, d = x_prompt.shape
    n_seq, n_steps, _ = x_sample.shape
    assert seq % PROMPT_TILE == 0 and PROMPT_TILE % MATMUL_ROWS == 0 and MATMUL_ROWS % ROW_BLOCK == 0
    assert ROW_BLOCK >= max(POOL_WINDOWS) and n_steps <= CHUNK and PAST_LEN % CHUNK == 0

    p = _stacked_params(n_steps, norm_mix_pre, norm_mix_post, norm_ffn_pre, norm_ffn_post, w_pool,
                        pool_scale, conv_w, conv_b, conv_ln_g, conv_ln_b, sgu_ln_g, sgu_ln_b, sgu_w,
                        sgu_b, short_w)
    yp = x_prompt
    ys = jnp.transpose(x_sample, (1, 0, 2)).reshape(n_steps * n_seq, d)
    hist_major = lambda a: jnp.transpose(a, (0, 2, 1, 3))
    pool_in, conv_in, short_in = hist_major(state_pool), hist_major(state_conv), hist_major(state_short)
    prompt_states = [[] for _ in range(3)]
    sample_states = [[] for _ in range(4)]
    w_in_b, w_out_b = w_in[0].astype(BF16), w_out[0].astype(BF16)
    for l in range(depth):
        yp, pool_p, conv_p, short_p, w_up_b, w_down_b = _prompt_mixer(
            yp, p, w_in_b, w_out_b, w_ffn_up, w_ffn_down, l)
        ys, pool_s, conv_s, short_s, v_s = _sample_mixer(
            ys, pool_in, conv_in, short_in, p, w_in_b, w_out_b, l)
        cast = ((w_in, l + 1), (w_out, l + 1)) if l + 1 < depth else ()
        yp, ys, *w_next = _ffn(yp.reshape(bp * seq, d), ys, p, w_up_b, w_down_b, l, cast)
        yp = yp.reshape(bp, seq, d)
        if w_next:
            w_in_b, w_out_b = w_next

        for lst, val in zip(prompt_states, (pool_p, conv_p, short_p)):
            lst.append(val)
        for lst, val in zip(sample_states, (pool_s, conv_s, short_s, v_s)):
            lst.append(val)

    ys = jnp.transpose(ys.reshape(n_steps, n_seq, d), (1, 0, 2))
    pool_p, conv_p, short_p = (jnp.stack(o) for o in prompt_states)
    pool_s, conv_s, short_s, v_s = (hist_major(jnp.stack(o)) for o in sample_states)
    return (yp, ys, pool_p, pool_s, conv_p, conv_s, short_p, short_s, v_s)
```

```python
import functools

import jax
import jax.numpy as jnp
import numpy as np
from jax import lax
from jax.experimental import pallas as pl
from jax.experimental.pallas import tpu as pltpu

D_MODEL = 1024
GROUP_WIDTH = 256
N_SUB = 4
SUB_DIM = 64
POOL_WINDOWS = (2, 4, 8, 16)
POOL_BUF = 15
CONV_WIDTH = 31
SHORT_WIDTH = 3
CHUNK = 128
D_FF = 4096
EPS = 1e-6
PAST_LEN = 16384
IN_WIDTH = 8 * GROUP_WIDTH

SUBLANES = 8
LANES = 128
N_SLABS = GROUP_WIDTH // LANES
POOL_HIST = 16
CONV_HIST = 32
SHORT_HIST = 8

ROW_BLOCK = CHUNK
MATMUL_ROWS = 512
PROMPT_TILE = 1024
FFN_TILE = 512
FF_CHUNK = 512
VMEM_LIMIT_BYTES = 56 * 1024 * 1024

F32 = jnp.float32
BF16 = jnp.bfloat16


def _rms_norm(x, g):
    ms = jnp.mean(x * x, axis=-1, keepdims=True)
    return x * lax.rsqrt(ms + EPS) * g


def _dot(a, b):
    return jnp.dot(a, b, preferred_element_type=F32)


def _head_mean(x, low_head):
    s_low = jnp.sum(jnp.where(low_head, x, 0.0), axis=-1, keepdims=True)
    s_high = jnp.sum(jnp.where(low_head, 0.0, x), axis=-1, keepdims=True)
    return jnp.where(low_head, s_low, s_high) * (1.0 / SUB_DIM)


def _head_layer_norm(x, g, b):
    low_head = lax.broadcasted_iota(jnp.int32, (x.shape[0], LANES), 1) < SUB_DIM
    out = []
    for s in range(N_SLABS):
        lanes = slice(s * LANES, (s + 1) * LANES)
        xs = x[:, lanes]
        xc = xs - _head_mean(xs, low_head)
        var = _head_mean(xc * xc, low_head)
        out.append(xc * lax.rsqrt(var + EPS) * g[:, lanes] + b[:, lanes])
    return jnp.concatenate(out, axis=1)


def _lane_group(shape):
    return jnp.right_shift(lax.broadcasted_iota(jnp.int32, shape, 1), SUB_DIM.bit_length() - 1)


def _pool_window(shape):
    grp = _lane_group(shape)
    w = jnp.full(shape, POOL_WINDOWS[0], jnp.int32)
    for gi in range(1, N_SUB):
        w = jnp.where(grp == gi, POOL_WINDOWS[gi], w)
    return grp, w


def _gating_rhs(vn):
    grp = _lane_group(vn.shape)
    return jnp.concatenate([jnp.where(grp == h, vn, 0.0).astype(BF16) for h in range(N_SUB)], axis=0)


def _cols(group, slab):
    lo = group * GROUP_WIDTH + slab * LANES
    return slice(lo, lo + LANES)


def _trailing_sum(ext_ref, slab, start, n_rows, width):
    acc = ext_ref[slab, start:start + n_rows, :]
    for k in range(1, width):
        acc = acc + ext_ref[slab, start - k:start - k + n_rows, :]
    return acc


def _causal_taps(ext_ref, slab, w_ref, hist, n_taps, r):
    first = hist + r - (n_taps - 1)
    lanes = slice(slab * LANES, (slab + 1) * LANES)
    acc = None
    for k in range(n_taps):
        term = ext_ref[slab, first + k:first + k + ROW_BLOCK, :] * w_ref[k:k + 1, lanes]
        acc = term if acc is None else acc + term
    return acc


def _gate_rows(r, z_ref, y_ref, vn_ref, swcat_ref, sbias_ref):
    rows = slice(r, r + ROW_BLOCK)
    gw = GROUP_WIDTH
    sg = _dot(swcat_ref[...], _gating_rhs(vn_ref[rows, :])) + sbias_ref[...]
    y_ref[rows, 2 * gw:3 * gw] = (z_ref[rows, 3 * gw:4 * gw] * sg).astype(BF16)


def _mix_rows(r, seq_start, z_ref, y_ref, vn_ref, g_ref, pool_ext, short_ext, wpool_ref,
              pscale_ref, pinv_first_ref, pinv_rest_ref, slng_ref, slnb_ref, shortw_ref):
    rows = slice(r, r + ROW_BLOCK)
    gw = GROUP_WIDTH
    low_head = lax.broadcasted_iota(jnp.int32, (ROW_BLOCK, LANES), 1) < SUB_DIM

    d = []
    for s in range(N_SLABS):
        lanes = slice(s * LANES, (s + 1) * LANES)
        a = z_ref[rows, _cols(0, s)]
        base = POOL_HIST + r
        pool_ext[s, base:base + ROW_BLOCK, :] = a
        small, large = POOL_WINDOWS[2 * s], POOL_WINDOWS[2 * s + 1]
        assert large == 2 * small
        if small % SUBLANES == 0:
            run = _trailing_sum(pool_ext, s, base - small, ROW_BLOCK + small, small)
            s_small = run[small:]
            s_large = s_small + run[:ROW_BLOCK]
        else:
            s_small = _trailing_sum(pool_ext, s, base, ROW_BLOCK, small)
            s_large = s_small + _trailing_sum(pool_ext, s, base - small, ROW_BLOCK, small)
        win = jnp.where(low_head, s_small, s_large)
        inv = pinv_rest_ref[:, lanes]
        if seq_start is not None:
            inv = jnp.where(seq_start, pinv_first_ref[:, lanes], inv)
        d.append(win * inv - a)
    d = jnp.concatenate(d, axis=1).astype(BF16)
    y_ref[rows, 0:gw] = (_dot(d, wpool_ref[...]) * pscale_ref[...]).astype(BF16)

    g_ref[rows, :] = z_ref[rows, gw:2 * gw] * jax.nn.sigmoid(z_ref[rows, 2 * gw:3 * gw])

    vn_ref[rows, :] = _head_layer_norm(z_ref[rows, 4 * gw:5 * gw], slng_ref[...], slnb_ref[...])

    for s in range(N_SLABS):
        ch = z_ref[rows, _cols(6, s)] * z_ref[rows, _cols(7, s)]
        short_ext[s, SHORT_HIST + r:SHORT_HIST + r + ROW_BLOCK, :] = ch
        sc = _causal_taps(short_ext, s, shortw_ref, SHORT_HIST, SHORT_WIDTH, r)
        y_ref[rows, _cols(3, s)] = (z_ref[rows, _cols(5, s)] * sc).astype(BF16)


def _conv_rows(r, g_ref, yc_ref, conv_ext, convw_ref, convb_ref, clng_ref, clnb_ref):
    rows = slice(r, r + ROW_BLOCK)
    c = []
    for s in range(N_SLABS):
        conv_ext[s, CONV_HIST + r:CONV_HIST + r + ROW_BLOCK, :] = g_ref[rows, s * LANES:(s + 1) * LANES]
        c.append(_causal_taps(conv_ext, s, convw_ref, CONV_HIST, CONV_WIDTH, r))
    c = jnp.concatenate(c, axis=1) + convb_ref[...]
    c = _head_layer_norm(c, clng_ref[...], clnb_ref[...])
    yc_ref[rows, :] = (c * jax.nn.sigmoid(c)).astype(BF16)


def _layer_views(layer, *refs):
    return [r.at[pl.ds(layer, 1)] if len(r.shape) == 2 else r.at[layer] for r in refs]


def _prompt_mixer_kernel(layer, x_ref, win_ref, wout_ref, pinv_first_ref, pinv_rest_ref,
                         gpre_ref, wpool_ref, pscale_ref, slng_ref, slnb_ref, sguw_ref, sbias_ref,
                         shortw_ref, wup_f32_ref, wdown_f32_ref,
                         opart_ref, g_ref, pool_out_ref, conv_out_ref, short_out_ref, wup_ref,
                         wdown_ref, z_ref, y_ref, vn_ref, swcat_ref, pool_ext, short_ext):
    (gpre_ref, wpool_ref, pscale_ref, slng_ref, slnb_ref, sguw_ref, sbias_ref,
     shortw_ref) = _layer_views(layer, gpre_ref, wpool_ref, pscale_ref, slng_ref, slnb_ref, sguw_ref,
                                sbias_ref, shortw_ref)
    j = pl.program_id(1)
    tile = x_ref.shape[0]
    gw = GROUP_WIDTH
    wup_ref[...] = wup_f32_ref[...].astype(BF16)
    wdown_ref[...] = wdown_f32_ref[...].astype(BF16)
    causal = (lax.broadcasted_iota(jnp.int32, (CHUNK, CHUNK), 0)
              >= lax.broadcasted_iota(jnp.int32, (CHUNK, CHUNK), 1))
    for hd in range(N_SUB):
        swcat_ref[:, hd * CHUNK:(hd + 1) * CHUNK] = jnp.where(causal, sguw_ref[hd], 0.0).astype(BF16)

    @pl.when(j == 0)
    def _():
        pool_ext[:, 0:POOL_HIST, :] = jnp.zeros((N_SLABS, POOL_HIST, LANES), F32)
        short_ext[:, 0:SHORT_HIST, :] = jnp.zeros((N_SLABS, SHORT_HIST, LANES), F32)

    def project_out(r0):
        rows = slice(r0, r0 + MATMUL_ROWS)
        opart_ref[rows, :] = (_dot(y_ref[rows, 0:gw], wout_ref[0:gw, :])
                              + _dot(y_ref[rows, 2 * gw:4 * gw], wout_ref[2 * gw:4 * gw, :]))

    for r0 in range(0, tile, MATMUL_ROWS):
        rows = slice(r0, r0 + MATMUL_ROWS)
        h = _rms_norm(x_ref[rows, :], gpre_ref[...]).astype(BF16)
        z_ref[rows, :] = _dot(h, win_ref[...])
        for r in range(r0, r0 + MATMUL_ROWS, ROW_BLOCK):
            _mix_rows(r, (j == 0) if r == 0 else None, z_ref, y_ref, vn_ref, g_ref, pool_ext,
                      short_ext, wpool_ref, pscale_ref, pinv_first_ref, pinv_rest_ref, slng_ref,
                      slnb_ref, shortw_ref)
            if r > 0:
                _gate_rows(r - ROW_BLOCK, z_ref, y_ref, vn_ref, swcat_ref, sbias_ref)
            if r == r0 and r0 > 0:
                project_out(r0 - MATMUL_ROWS)
    _gate_rows(tile - ROW_BLOCK, z_ref, y_ref, vn_ref, swcat_ref, sbias_ref)
    project_out(tile - MATMUL_ROWS)

    pool_ext[:, 0:POOL_HIST, :] = pool_ext[:, tile:tile + POOL_HIST, :]
    short_ext[:, 0:SHORT_HIST, :] = short_ext[:, tile:tile + SHORT_HIST, :]

    @pl.when(j == pl.num_programs(1) - 1)
    def _():
        conv_out_ref[...] = g_ref[tile - (CONV_WIDTH - 1):tile, :]
        for s in range(N_SLABS):
            lanes = slice(s * LANES, (s + 1) * LANES)
            pool_out_ref[:, lanes] = pool_ext[s, POOL_HIST - POOL_BUF:POOL_HIST, :]
            short_out_ref[:, lanes] = short_ext[s, SHORT_HIST - (SHORT_WIDTH - 1):SHORT_HIST, :]


def _sample_mixer_kernel(layer, x_ref, pool_in_ref, conv_in_ref, short_in_ref, win_ref, wout_ref,
                         gpre_ref, gpost_ref, wpool_ref, pscale_ref, convw_ref, convb_ref, clng_ref,
                         clnb_ref, slng_ref, slnb_ref, sgw_ref, sgb_ref, shortw_ref,
                         out_ref, pool_out_ref, conv_out_ref, short_out_ref, v_out_ref,
                         z_ref, y_ref):
    (gpre_ref, gpost_ref, wpool_ref, pscale_ref, convw_ref, convb_ref, clng_ref, clnb_ref, slng_ref,
     slnb_ref, sgw_ref, sgb_ref, shortw_ref) = _layer_views(
         layer, gpre_ref, gpost_ref, wpool_ref, pscale_ref, convw_ref, convb_ref, clng_ref, clnb_ref,
         slng_ref, slnb_ref, sgw_ref, sgb_ref, shortw_ref)
    n_steps, n_seq = v_out_ref.shape[0], v_out_ref.shape[1]
    gw = GROUP_WIDTH
    x = x_ref[...]
    h = _rms_norm(x, gpre_ref[...]).astype(BF16)
    z_ref[...] = _dot(h, win_ref[...])
    grp, w = _pool_window((n_seq, gw))
    cnt = jnp.minimum(w, PAST_LEN + 1).astype(F32)

    def slab(t):
        return slice(t * n_seq, (t + 1) * n_seq)

    a_new = [z_ref[slab(t), 0:gw] for t in range(n_steps)]
    g_new = [z_ref[slab(t), gw:2 * gw] * jax.nn.sigmoid(z_ref[slab(t), 2 * gw:3 * gw])
             for t in range(n_steps)]
    ch_new = [z_ref[slab(t), 6 * gw:7 * gw] * z_ref[slab(t), 7 * gw:8 * gw]
              for t in range(n_steps)]

    def pool_row(i):
        return pool_in_ref[i] if i < POOL_BUF else a_new[i - POOL_BUF]

    def conv_row(i):
        return conv_in_ref[i] if i < CONV_WIDTH - 1 else g_new[i - (CONV_WIDTH - 1)]

    def short_row(i):
        return short_in_ref[i] if i < SHORT_WIDTH - 1 else ch_new[i - (SHORT_WIDTH - 1)]

    vn = []
    for t in range(n_steps):
        rows = slab(t)
        end = POOL_BUF + t
        acc = pool_row(end)
        sums = []
        for k in range(1, max(POOL_WINDOWS)):
            acc = acc + pool_row(end - k)
            if k + 1 in POOL_WINDOWS:
                sums.append(acc)
        win = sums[0]
        for gi in range(1, N_SUB):
            win = jnp.where(grp == gi, sums[gi], win)
        d = win / cnt - a_new[t]
        y_ref[rows, 0:gw] = (_dot(d.astype(BF16), wpool_ref[...]) * pscale_ref[...]).astype(BF16)

        c = None
        for k in range(CONV_WIDTH):
            term = conv_row(t + k) * convw_ref[k:k + 1, :]
            c = term if c is None else c + term
        c = _head_layer_norm(c + convb_ref[...], clng_ref[...], clnb_ref[...])
        y_ref[rows, gw:2 * gw] = (c * jax.nn.sigmoid(c)).astype(BF16)

        vn.append(_head_layer_norm(z_ref[rows, 4 * gw:5 * gw], slng_ref[...], slnb_ref[...]))
        v_out_ref[t] = vn[t]
        s = sgb_ref[t:t + 1, :]
        for u in range(t + 1):
            s = s + sgw_ref[t * n_steps + u:t * n_steps + u + 1, :] * vn[u]
        y_ref[rows, 2 * gw:3 * gw] = (z_ref[rows, 3 * gw:4 * gw] * s).astype(BF16)

        sc = None
        for k in range(SHORT_WIDTH):
            term = short_row(t + k) * shortw_ref[k:k + 1, :]
            sc = term if sc is None else sc + term
        y_ref[rows, 3 * gw:4 * gw] = (z_ref[rows, 5 * gw:6 * gw] * sc).astype(BF16)

    o = _dot(y_ref[...], wout_ref[...])
    out_ref[...] = x + _rms_norm(o, gpost_ref[...])

    for i in range(POOL_BUF):
        pool_out_ref[i] = pool_row(i + n_steps)
    for i in range(CONV_WIDTH - 1):
        conv_out_ref[i] = conv_row(i + n_steps)
    for i in range(SHORT_WIDTH - 1):
        short_out_ref[i] = short_row(i + n_steps)


def _mlp_chunk(c, f_ref, acc_ref, wup_ref, wdown_ref):
    cols = slice(c * FF_CHUNK, (c + 1) * FF_CHUNK)
    u = jnp.maximum(_dot(f_ref[...], wup_ref[:, cols]), 0.0)
    part = _dot((u * u).astype(BF16), wdown_ref[cols, :])
    if c == 0:
        acc_ref[...] = part
    else:
        acc_ref[...] += part


def _ffn_kernel(layer, n_cast, tiles_per_seq, g_ref, opart_ref, x_ref, xs_ref, wout_ref, convw_ref,
                convb_ref, clng_ref, clnb_ref, gpost_ref, fpre_ref, fpost_ref, wup_ref, wdown_ref,
                *rest):
    cast_in, out_ref, outs_ref = rest[:n_cast], rest[n_cast], rest[n_cast + 1]
    cast_out = rest[n_cast + 2:-5]
    conv_ext, yc_ref, f_ref, mid_ring, acc_ref = rest[-5:]
    (convw_ref, convb_ref, clng_ref, clnb_ref, gpost_ref, fpre_ref, fpost_ref) = _layer_views(
        layer, convw_ref, convb_ref, clng_ref, clnb_ref, gpost_ref, fpre_ref, fpost_ref)
    i = pl.program_id(0)
    last = pl.num_programs(0) - 1
    tile = x_ref.shape[0]
    gw = GROUP_WIDTH
    slot = lax.rem(i, 2)
    mid_new, mid_old = mid_ring.at[slot], mid_ring.at[1 - slot]
    j = lax.rem(jnp.minimum(i, last - 1), tiles_per_seq)
    n_chunks = D_FF // FF_CHUNK

    @pl.when(i == 0)
    def _():
        mid_ring[1] = jnp.zeros(mid_ring.shape[1:], F32)

    @pl.when(j == 0)
    def _():
        conv_ext[:, 0:CONV_HIST, :] = jnp.zeros((N_SLABS, CONV_HIST, LANES), F32)

    def mlp(x_rows_ref, out_rows_ref, between=()):
        f_ref[...] = _rms_norm(x_rows_ref[...], fpre_ref[...]).astype(BF16)
        for c in range(n_chunks):
            _mlp_chunk(c, f_ref, acc_ref, wup_ref, wdown_ref)
            if c < len(between):
                between[c]()
        out_rows_ref[...] = x_rows_ref[...] + _rms_norm(acc_ref[...], fpost_ref[...])

    def finish_mixing():
        o = opart_ref[...] + _dot(yc_ref[...], wout_ref[gw:2 * gw, :])
        mid_new[...] = x_ref[...] + _rms_norm(o, gpost_ref[...])

    @pl.when(i < last)
    def _():
        for src, dst in zip(cast_in, cast_out):
            dst[...] = src[...].astype(BF16)
        conv_chunks = [functools.partial(_conv_rows, r, g_ref, yc_ref, conv_ext, convw_ref, convb_ref,
                                         clng_ref, clnb_ref) for r in range(0, tile, ROW_BLOCK)]
        assert len(conv_chunks) + 1 <= n_chunks
        mlp(mid_old, out_ref, conv_chunks + [finish_mixing])
        conv_ext[:, 0:CONV_HIST, :] = conv_ext[:, tile:tile + CONV_HIST, :]

    @pl.when(i == last)
    def _():
        mlp(mid_old, out_ref)
        mlp(xs_ref, outs_ref)


def _const_spec(shape):
    nd = len(shape)
    return pl.BlockSpec(shape, lambda *_: (0,) * nd, pipeline_mode=pl.Buffered(1))


def _whole_spec(shape):
    nd = len(shape)
    return pl.BlockSpec(shape, lambda *_: (0,) * nd)


def _compiler_params(semantics):
    return pltpu.CompilerParams(dimension_semantics=semantics, vmem_limit_bytes=VMEM_LIMIT_BYTES)


def _prompt_mixer(x, p, w_in, w_out, w_ffn_up, w_ffn_down, layer):
    b, s, d = x.shape
    tile = PROMPT_TILE
    consts = (w_in, w_out, POOL_INV_FIRST, POOL_INV_REST,
              p["gpre"], p["wpool"], p["pscale"], p["sln_g"], p["sln_b"], p["sgu_w"], p["sgu_bias"],
              p["short_w"])
    gw = GROUP_WIDTH
    tiles_per_seq = s // tile
    steps = b * tiles_per_seq
    d_ff = w_ffn_up.shape[2]
    ff_slice = d_ff // steps
    assert ff_slice * steps == d_ff and ff_slice % LANES == 0
    step = lambda bi, j: bi * tiles_per_seq + j
    row_tile = lambda width: pl.BlockSpec((None, tile, width), lambda bi, j: (bi, j, 0))
    state_spec = lambda n: pl.BlockSpec((None, n, gw), lambda bi, j: (bi, 0, 0))
    return pl.pallas_call(
        functools.partial(_prompt_mixer_kernel, layer),
        grid=(b, tiles_per_seq),
        in_specs=[row_tile(d)] + [_const_spec(c.shape) for c in consts]
        + [pl.BlockSpec((None, d, ff_slice), lambda bi, j: (layer, 0, step(bi, j))),
           pl.BlockSpec((None, ff_slice, d), lambda bi, j: (layer, step(bi, j), 0))],
        out_specs=[row_tile(d), row_tile(gw),
                   state_spec(POOL_BUF), state_spec(CONV_WIDTH - 1), state_spec(SHORT_WIDTH - 1),
                   pl.BlockSpec((d, ff_slice), lambda bi, j: (0, step(bi, j))),
                   pl.BlockSpec((ff_slice, d), lambda bi, j: (step(bi, j), 0))],
        out_shape=[jax.ShapeDtypeStruct((b, s, d), F32),
                   jax.ShapeDtypeStruct((b, s, gw), F32),
                   jax.ShapeDtypeStruct((b, POOL_BUF, gw), F32),
                   jax.ShapeDtypeStruct((b, CONV_WIDTH - 1, gw), F32),
                   jax.ShapeDtypeStruct((b, SHORT_WIDTH - 1, gw), F32),
                   jax.ShapeDtypeStruct((d, d_ff), BF16),
                   jax.ShapeDtypeStruct((d_ff, d), BF16)],
        scratch_shapes=[pltpu.VMEM((tile, IN_WIDTH), F32),
                        pltpu.VMEM((tile, d), BF16),
                        pltpu.VMEM((tile, gw), F32),
                        pltpu.VMEM((CHUNK, N_SUB * CHUNK), BF16),
                        pltpu.VMEM((N_SLABS, POOL_HIST + tile, LANES), F32),
                        pltpu.VMEM((N_SLABS, SHORT_HIST + tile, LANES), F32)],
        compiler_params=_compiler_params(("arbitrary", "arbitrary")),
        name="prompt_mixer",
    )(x, *consts, w_ffn_up, w_ffn_down)


def _sample_mixer(x, pool_st, conv_st, short_st, p, w_in, w_out, layer):
    n, d = x.shape
    n_seq = pool_st.shape[2]
    n_steps = n // n_seq
    gw = GROUP_WIDTH
    states = (pool_st, conv_st, short_st)
    consts = (w_in, w_out, p["gpre"], p["gpost"], p["wpool"], p["pscale"], p["conv_w"], p["conv_b"],
              p["cln_g"], p["cln_b"], p["sln_g"], p["sln_b"], p["sgu_w4"], p["sgu_bias"], p["short_w"])
    state_spec = lambda st: pl.BlockSpec((None,) + st.shape[1:], lambda i: (layer, 0, 0, 0))
    out_shape = ([jax.ShapeDtypeStruct((n, d), F32)]
                 + [jax.ShapeDtypeStruct(st.shape[1:], F32) for st in states]
                 + [jax.ShapeDtypeStruct((n_steps, n_seq, gw), F32)])
    return pl.pallas_call(
        functools.partial(_sample_mixer_kernel, layer),
        grid=(1,),
        in_specs=[_whole_spec(x.shape)] + [state_spec(st) for st in states]
        + [_whole_spec(c.shape) for c in consts],
        out_specs=[_whole_spec(o.shape) for o in out_shape],
        out_shape=out_shape,
        scratch_shapes=[pltpu.VMEM((n, IN_WIDTH), F32), pltpu.VMEM((n, d), BF16)],
        compiler_params=_compiler_params(("arbitrary",)),
        name="sample_mixer",
    )(x, *states, *consts)


def _ffn(g, opart, x, xs, p, w_out, w_up, w_down, layer, seq, cast=()):
    n, d = x.shape
    ns = xs.shape[0]
    gw = GROUP_WIDTH
    tile = FFN_TILE
    steps = n // tile
    assert steps * tile == n and seq % tile == 0 and ns == tile
    consts = (w_out, p["conv_w"], p["conv_b"], p["cln_g"], p["cln_b"], p["gpost"], p["fpre"],
              p["fpost"], w_up, w_down)
    mixed_tile = lambda i: (jnp.minimum(i, steps - 1), 0)
    mlp_tile = lambda i: (jnp.clip(i - 1, 0, steps - 1), 0)
    cast_in_specs, cast_out_specs, cast_shapes = [], [], []
    for w, w_layer in cast:
        rows = w.shape[1] // steps
        assert rows * steps == w.shape[1] and rows % (2 * SUBLANES) == 0
        cast_in_specs.append(pl.BlockSpec(
            (None, rows, w.shape[2]), lambda i, w_layer=w_layer: (w_layer,) + mixed_tile(i)))
        cast_out_specs.append(pl.BlockSpec((rows, w.shape[2]), mixed_tile))
        cast_shapes.append(jax.ShapeDtypeStruct(w.shape[1:], BF16))
    return pl.pallas_call(
        functools.partial(_ffn_kernel, layer, len(cast), seq // tile),
        grid=(steps + 1,),
        in_specs=[pl.BlockSpec((tile, gw), mixed_tile), pl.BlockSpec((tile, d), mixed_tile),
                  pl.BlockSpec((tile, d), mixed_tile), _whole_spec(xs.shape)]
        + [_const_spec(c.shape) for c in consts] + cast_in_specs,
        out_specs=[pl.BlockSpec((tile, d), mlp_tile), _whole_spec(xs.shape)] + cast_out_specs,
        out_shape=[jax.ShapeDtypeStruct((n, d), F32), jax.ShapeDtypeStruct((ns, d), F32)] + cast_shapes,
        scratch_shapes=[pltpu.VMEM((N_SLABS, CONV_HIST + tile, LANES), F32),
                        pltpu.VMEM((tile, gw), BF16),
                        pltpu.VMEM((tile, d), BF16),
                        pltpu.VMEM((2, tile, d), F32),
                        pltpu.VMEM((tile, d), F32)],
        compiler_params=_compiler_params(("arbitrary",)),
        name="ffn",
    )(g, opart, x, xs, *consts, *[w for w, _ in cast])


def _pool_inverse_counts():
    window = np.repeat(np.asarray(POOL_WINDOWS, np.float32), SUB_DIM)[None, :]
    first = 1.0 / np.minimum(window, np.arange(1, ROW_BLOCK + 1, dtype=np.float32)[:, None])
    return first.astype(np.float32), (1.0 / window).astype(np.float32)


POOL_INV_FIRST, POOL_INV_REST = _pool_inverse_counts()


def _stacked_params(n_steps, norm_mix_pre, norm_mix_post, norm_ffn_pre, norm_ffn_post, w_pool,
                    pool_scale, conv_w, conv_b, conv_ln_g, conv_ln_b, sgu_ln_g, sgu_ln_b, sgu_w,
                    sgu_b, short_w):
    depth = w_pool.shape[0]
    gw = GROUP_WIDTH
    head_of_lane = np.arange(gw) // SUB_DIM
    same_head = head_of_lane[:, None] == head_of_lane[None, :]
    wp = jnp.tile(w_pool.reshape(depth, gw, SUB_DIM), (1, 1, N_SUB))
    wpool = jnp.where(same_head[None], wp, 0.0).astype(BF16)
    bias = jnp.repeat(jnp.swapaxes(sgu_b, 1, 2), SUB_DIM, axis=2)
    causal = np.tril(np.ones((n_steps, n_steps), dtype=bool))
    w4 = jnp.where(causal[None, None], sgu_w[:, :, :n_steps, :n_steps], 0.0)
    w4 = jnp.repeat(jnp.transpose(w4, (0, 2, 3, 1)), SUB_DIM, axis=3)
    w4 = w4.reshape(depth, n_steps * n_steps, gw)
    return dict(gpre=norm_mix_pre, gpost=norm_mix_post, fpre=norm_ffn_pre, fpost=norm_ffn_post,
                wpool=wpool, pscale=pool_scale, conv_w=conv_w, conv_b=conv_b, cln_g=conv_ln_g,
                cln_b=conv_ln_b, sln_g=sgu_ln_g, sln_b=sgu_ln_b, sgu_w=sgu_w, sgu_bias=bias,
                sgu_w4=w4, short_w=short_w)


def kernel(x_prompt, x_sample, state_pool, state_conv, state_short, norm_mix_pre, norm_mix_post, norm_ffn_pre, norm_ffn_post, w_in, w_out, w_pool, pool_scale, conv_w, conv_b, conv_ln_g, conv_ln_b, sgu_ln_g, sgu_ln_b, sgu_w, sgu_b, short_w, w_ffn_up, w_ffn_down):
    depth = w_in.shape[0]
    bp, seq, d = x_prompt.shape
    n_seq, n_steps, _ = x_sample.shape
    assert seq % PROMPT_TILE == 0 and PROMPT_TILE % MATMUL_ROWS == 0 and MATMUL_ROWS % ROW_BLOCK == 0
    assert ROW_BLOCK >= max(POOL_WINDOWS) and n_steps <= CHUNK and PAST_LEN % CHUNK == 0

    p = _stacked_params(n_steps, norm_mix_pre, norm_mix_post, norm_ffn_pre, norm_ffn_post, w_pool,
                        pool_scale, conv_w, conv_b, conv_ln_g, conv_ln_b, sgu_ln_g, sgu_ln_b, sgu_w,
                        sgu_b, short_w)
    yp = x_prompt
    ys = jnp.transpose(x_sample, (1, 0, 2)).reshape(n_steps * n_seq, d)
    hist_major = lambda a: jnp.transpose(a, (0, 2, 1, 3))
    pool_in, conv_in, short_in = hist_major(state_pool), hist_major(state_conv), hist_major(state_short)
    prompt_states = [[] for _ in range(3)]
    sample_states = [[] for _ in range(4)]
    w_in_b, w_out_b = w_in[0].astype(BF16), w_out[0].astype(BF16)
    for l in range(depth):
        opart, g, pool_p, conv_p, short_p, w_up_b, w_down_b = _prompt_mixer(
            yp, p, w_in_b, w_out_b, w_ffn_up, w_ffn_down, l)
        ys, pool_s, conv_s, short_s, v_s = _sample_mixer(
            ys, pool_in, conv_in, short_in, p, w_in_b, w_out_b, l)
        cast = ((w_in, l + 1), (w_out, l + 1)) if l + 1 < depth else ()
        flat = lambda a: a.reshape(bp * seq, a.shape[-1])
        yp, ys, *w_next = _ffn(flat(g), flat(opart), flat(yp), ys, p, w_out_b, w_up_b, w_down_b,
                               l, seq, cast)
        yp = yp.reshape(bp, seq, d)
        if w_next:
            w_in_b, w_out_b = w_next

        for lst, val in zip(prompt_states, (pool_p, conv_p, short_p)):
            lst.append(val)
        for lst, val in zip(sample_states, (pool_s, conv_s, short_s, v_s)):
            lst.append(val)

    ys = jnp.transpose(ys.reshape(n_steps, n_seq, d), (1, 0, 2))
    pool_p, conv_p, short_p = (jnp.stack(o) for o in prompt_states)
    pool_s, conv_s, short_s, v_s = (hist_major(jnp.stack(o)) for o in sample_states)
    return (yp, ys, pool_p, pool_s, conv_p, conv_s, short_p, short_s, v_s)
```

```python
import functools

import jax
import jax.numpy as jnp
import numpy as np
from jax import lax
from jax.experimental import pallas as pl
from jax.experimental.pallas import tpu as pltpu

D_MODEL = 1024
GROUP_WIDTH = 256
N_SUB = 4
SUB_DIM = 64
POOL_WINDOWS = (2, 4, 8, 16)
POOL_BUF = 15
CONV_WIDTH = 31
SHORT_WIDTH = 3
CHUNK = 128
D_FF = 4096
EPS = 1e-6
PAST_LEN = 16384
IN_WIDTH = 8 * GROUP_WIDTH

SUBLANES = 8
LANES = 128
N_SLABS = GROUP_WIDTH // LANES
POOL_HIST = 16
CONV_HIST = 32
SHORT_HIST = 8

ROW_BLOCK = CHUNK
MATMUL_ROWS = 512
PROMPT_TILE = 1024
FFN_TILE = 1024
FFN_ROWS = 512
FF_CHUNK = 1024
VMEM_LIMIT_BYTES = 56 * 1024 * 1024
N_RELAYOUT_OUTPUTS = 3

F32 = jnp.float32
BF16 = jnp.bfloat16


def _rms_norm(x, g):
    ms = jnp.mean(x * x, axis=-1, keepdims=True)
    return x * lax.rsqrt(ms + EPS) * g


def _dot(a, b):
    return jnp.dot(a, b, preferred_element_type=F32)


def _head_mean(x, low_head):
    s_low = jnp.sum(jnp.where(low_head, x, 0.0), axis=-1, keepdims=True)
    s_high = jnp.sum(jnp.where(low_head, 0.0, x), axis=-1, keepdims=True)
    return jnp.where(low_head, s_low, s_high) * (1.0 / SUB_DIM)


def _head_layer_norm(x, g, b):
    low_head = lax.broadcasted_iota(jnp.int32, (x.shape[0], LANES), 1) < SUB_DIM
    out = []
    for s in range(N_SLABS):
        lanes = slice(s * LANES, (s + 1) * LANES)
        xs = x[:, lanes]
        xc = xs - _head_mean(xs, low_head)
        var = _head_mean(xc * xc, low_head)
        out.append(xc * lax.rsqrt(var + EPS) * g[:, lanes] + b[:, lanes])
    return jnp.concatenate(out, axis=1)


def _lane_group(shape):
    return jnp.right_shift(lax.broadcasted_iota(jnp.int32, shape, 1), SUB_DIM.bit_length() - 1)


def _pool_window(shape):
    grp = _lane_group(shape)
    w = jnp.full(shape, POOL_WINDOWS[0], jnp.int32)
    for gi in range(1, N_SUB):
        w = jnp.where(grp == gi, POOL_WINDOWS[gi], w)
    return grp, w


def _gating_rhs(vn):
    grp = _lane_group(vn.shape)
    return jnp.concatenate([jnp.where(grp == h, vn, 0.0).astype(BF16) for h in range(N_SUB)], axis=0)


def _cols(group, slab):
    lo = group * GROUP_WIDTH + slab * LANES
    return slice(lo, lo + LANES)


def _trailing_sum(ext_ref, slab, start, n_rows, width):
    acc = ext_ref[slab, start:start + n_rows, :]
    for k in range(1, width):
        acc = acc + ext_ref[slab, start - k:start - k + n_rows, :]
    return acc


def _causal_taps(ext_ref, slab, w_ref, hist, n_taps, r):
    first = hist + r - (n_taps - 1)
    lanes = slice(slab * LANES, (slab + 1) * LANES)
    acc = None
    for k in range(n_taps):
        term = ext_ref[slab, first + k:first + k + ROW_BLOCK, :] * w_ref[k:k + 1, lanes]
        acc = term if acc is None else acc + term
    return acc


def _gate_rows(r, z_ref, y_ref, vn_ref, swcat_ref, sbias_ref):
    rows = slice(r, r + ROW_BLOCK)
    gw = GROUP_WIDTH
    sg = _dot(swcat_ref[...], _gating_rhs(vn_ref[rows, :])) + sbias_ref[...]
    y_ref[rows, 2 * gw:3 * gw] = (z_ref[rows, 3 * gw:4 * gw] * sg).astype(BF16)


def _mix_rows(r, seq_start, z_ref, y_ref, vn_ref, pool_ext, conv_ext, short_ext, wpool_ref,
              pscale_ref, pinv_first_ref, pinv_rest_ref, convw_ref, convb_ref, clng_ref, clnb_ref,
              slng_ref, slnb_ref, shortw_ref):
    rows = slice(r, r + ROW_BLOCK)
    gw = GROUP_WIDTH
    low_head = lax.broadcasted_iota(jnp.int32, (ROW_BLOCK, LANES), 1) < SUB_DIM

    d = []
    for s in range(N_SLABS):
        lanes = slice(s * LANES, (s + 1) * LANES)
        a = z_ref[rows, _cols(0, s)]
        base = POOL_HIST + r
        pool_ext[s, base:base + ROW_BLOCK, :] = a
        small, large = POOL_WINDOWS[2 * s], POOL_WINDOWS[2 * s + 1]
        assert large == 2 * small
        if small % SUBLANES == 0:
            run = _trailing_sum(pool_ext, s, base - small, ROW_BLOCK + small, small)
            s_small = run[small:]
            s_large = s_small + run[:ROW_BLOCK]
        else:
            s_small = _trailing_sum(pool_ext, s, base, ROW_BLOCK, small)
            s_large = s_small + _trailing_sum(pool_ext, s, base - small, ROW_BLOCK, small)
        win = jnp.where(low_head, s_small, s_large)
        inv = pinv_rest_ref[:, lanes]
        if seq_start is not None:
            inv = jnp.where(seq_start, pinv_first_ref[:, lanes], inv)
        d.append(win * inv - a)
    d = jnp.concatenate(d, axis=1).astype(BF16)
    y_ref[rows, 0:gw] = (_dot(d, wpool_ref[...]) * pscale_ref[...]).astype(BF16)

    c = []
    for s in range(N_SLABS):
        g = z_ref[rows, _cols(1, s)] * jax.nn.sigmoid(z_ref[rows, _cols(2, s)])
        conv_ext[s, CONV_HIST + r:CONV_HIST + r + ROW_BLOCK, :] = g
        c.append(_causal_taps(conv_ext, s, convw_ref, CONV_HIST, CONV_WIDTH, r))
    c = jnp.concatenate(c, axis=1) + convb_ref[...]
    c = _head_layer_norm(c, clng_ref[...], clnb_ref[...])
    y_ref[rows, gw:2 * gw] = (c * jax.nn.sigmoid(c)).astype(BF16)

    vn_ref[rows, :] = _head_layer_norm(z_ref[rows, 4 * gw:5 * gw], slng_ref[...], slnb_ref[...])

    for s in range(N_SLABS):
        ch = z_ref[rows, _cols(6, s)] * z_ref[rows, _cols(7, s)]
        short_ext[s, SHORT_HIST + r:SHORT_HIST + r + ROW_BLOCK, :] = ch
        sc = _causal_taps(short_ext, s, shortw_ref, SHORT_HIST, SHORT_WIDTH, r)
        y_ref[rows, _cols(3, s)] = (z_ref[rows, _cols(5, s)] * sc).astype(BF16)


def _relayout_copies(natural_ref, layer, major_ref, sem, to_major):
    copies = []
    for k in range(major_ref.shape[0]):
        natural, major = natural_ref.at[layer, :, k, :], major_ref.at[k]
        src, dst = (natural, major) if to_major else (major, natural)
        copies.append(pltpu.make_async_copy(src, dst, sem))
    return copies


def _layer_views(layer, *refs):
    return [r.at[pl.ds(layer, 1)] if len(r.shape) == 2 else r.at[layer] for r in refs]


def _prompt_mixer_kernel(layer, x_ref, win_ref, wout_ref, pinv_first_ref, pinv_rest_ref,
                         gpre_ref, gpost_ref, wpool_ref, pscale_ref, convw_ref, convb_ref, clng_ref,
                         clnb_ref, slng_ref, slnb_ref, sguw_ref, sbias_ref, shortw_ref,
                         wup_f32_ref, wdown_f32_ref, conv_nat_ref, short_nat_ref,
                         out_ref, pool_out_ref, conv_out_ref, short_out_ref, wup_ref, wdown_ref,
                         conv_maj_ref, short_maj_ref,
                         z_ref, y_ref, vn_ref, swcat_ref, pool_ext, conv_ext, short_ext, sems):
    (gpre_ref, gpost_ref, wpool_ref, pscale_ref, convw_ref, convb_ref, clng_ref, clnb_ref, slng_ref,
     slnb_ref, sguw_ref, sbias_ref, shortw_ref) = _layer_views(
         layer, gpre_ref, gpost_ref, wpool_ref, pscale_ref, convw_ref, convb_ref, clng_ref, clnb_ref,
         slng_ref, slnb_ref, sguw_ref, sbias_ref, shortw_ref)
    j = pl.program_id(1)
    tile = x_ref.shape[0]
    state_pairs = ((conv_nat_ref, conv_maj_ref), (short_nat_ref, short_maj_ref))
    relayout = [c for s, (nat, maj) in enumerate(state_pairs)
                for c in _relayout_copies(nat, layer, maj, sems.at[s], to_major=True)]

    @pl.when((pl.program_id(0) == 0) & (j == 0))
    def _():
        for c in relayout:
            c.start()

    wup_ref[...] = wup_f32_ref[...].astype(BF16)
    wdown_ref[...] = wdown_f32_ref[...].astype(BF16)
    causal = (lax.broadcasted_iota(jnp.int32, (CHUNK, CHUNK), 0)
              >= lax.broadcasted_iota(jnp.int32, (CHUNK, CHUNK), 1))
    for hd in range(N_SUB):
        swcat_ref[:, hd * CHUNK:(hd + 1) * CHUNK] = jnp.where(causal, sguw_ref[hd], 0.0).astype(BF16)

    @pl.when(j == 0)
    def _():
        pool_ext[:, 0:POOL_HIST, :] = jnp.zeros((N_SLABS, POOL_HIST, LANES), F32)
        conv_ext[:, 0:CONV_HIST, :] = jnp.zeros((N_SLABS, CONV_HIST, LANES), F32)
        short_ext[:, 0:SHORT_HIST, :] = jnp.zeros((N_SLABS, SHORT_HIST, LANES), F32)

    def project_out(r0):
        rows = slice(r0, r0 + MATMUL_ROWS)
        o = _dot(y_ref[rows, :], wout_ref[...])
        out_ref[rows, :] = x_ref[rows, :] + _rms_norm(o, gpost_ref[...])

    for r0 in range(0, tile, MATMUL_ROWS):
        rows = slice(r0, r0 + MATMUL_ROWS)
        h = _rms_norm(x_ref[rows, :], gpre_ref[...]).astype(BF16)
        z_ref[rows, :] = _dot(h, win_ref[...])
        for r in range(r0, r0 + MATMUL_ROWS, ROW_BLOCK):
            _mix_rows(r, (j == 0) if r == 0 else None, z_ref, y_ref, vn_ref, pool_ext, conv_ext,
                      short_ext, wpool_ref, pscale_ref, pinv_first_ref, pinv_rest_ref, convw_ref,
                      convb_ref, clng_ref, clnb_ref, slng_ref, slnb_ref, shortw_ref)
            if r > 0:
                _gate_rows(r - ROW_BLOCK, z_ref, y_ref, vn_ref, swcat_ref, sbias_ref)
            if r == r0 and r0 > 0:
                project_out(r0 - MATMUL_ROWS)
    _gate_rows(tile - ROW_BLOCK, z_ref, y_ref, vn_ref, swcat_ref, sbias_ref)
    project_out(tile - MATMUL_ROWS)

    pool_ext[:, 0:POOL_HIST, :] = pool_ext[:, tile:tile + POOL_HIST, :]
    conv_ext[:, 0:CONV_HIST, :] = conv_ext[:, tile:tile + CONV_HIST, :]
    short_ext[:, 0:SHORT_HIST, :] = short_ext[:, tile:tile + SHORT_HIST, :]

    @pl.when(j == pl.num_programs(1) - 1)
    def _():
        for s in range(N_SLABS):
            lanes = slice(s * LANES, (s + 1) * LANES)
            pool_out_ref[:, lanes] = pool_ext[s, POOL_HIST - POOL_BUF:POOL_HIST, :]
            conv_out_ref[:, lanes] = conv_ext[s, CONV_HIST - (CONV_WIDTH - 1):CONV_HIST, :]
            short_out_ref[:, lanes] = short_ext[s, SHORT_HIST - (SHORT_WIDTH - 1):SHORT_HIST, :]

    @pl.when((pl.program_id(0) == pl.num_programs(0) - 1) & (j == pl.num_programs(1) - 1))
    def _():
        for c in relayout:
            c.wait()


def _sample_mixer_kernel(layer, x_ref, pool_in_ref, conv_in_ref, short_in_ref, win_ref, wout_ref,
                         gpre_ref, gpost_ref, wpool_ref, pscale_ref, convw_ref, convb_ref, clng_ref,
                         clnb_ref, slng_ref, slnb_ref, sgw_ref, sgb_ref, shortw_ref,
                         out_ref, pool_out_ref, conv_out_ref, short_out_ref, v_out_ref,
                         z_ref, y_ref):
    (gpre_ref, gpost_ref, wpool_ref, pscale_ref, convw_ref, convb_ref, clng_ref, clnb_ref, slng_ref,
     slnb_ref, sgw_ref, sgb_ref, shortw_ref) = _layer_views(
         layer, gpre_ref, gpost_ref, wpool_ref, pscale_ref, convw_ref, convb_ref, clng_ref, clnb_ref,
         slng_ref, slnb_ref, sgw_ref, sgb_ref, shortw_ref)
    n_steps, n_seq = v_out_ref.shape[0], v_out_ref.shape[1]
    gw = GROUP_WIDTH
    x = x_ref[...]
    h = _rms_norm(x, gpre_ref[...]).astype(BF16)
    z_ref[...] = _dot(h, win_ref[...])
    grp, w = _pool_window((n_seq, gw))
    cnt = jnp.minimum(w, PAST_LEN + 1).astype(F32)

    def slab(t):
        return slice(t * n_seq, (t + 1) * n_seq)

    a_new = [z_ref[slab(t), 0:gw] for t in range(n_steps)]
    g_new = [z_ref[slab(t), gw:2 * gw] * jax.nn.sigmoid(z_ref[slab(t), 2 * gw:3 * gw])
             for t in range(n_steps)]
    ch_new = [z_ref[slab(t), 6 * gw:7 * gw] * z_ref[slab(t), 7 * gw:8 * gw]
              for t in range(n_steps)]

    def pool_row(i):
        return pool_in_ref[i] if i < POOL_BUF else a_new[i - POOL_BUF]

    def conv_row(i):
        return conv_in_ref[i] if i < CONV_WIDTH - 1 else g_new[i - (CONV_WIDTH - 1)]

    def short_row(i):
        return short_in_ref[i] if i < SHORT_WIDTH - 1 else ch_new[i - (SHORT_WIDTH - 1)]

    vn = []
    for t in range(n_steps):
        rows = slab(t)
        end = POOL_BUF + t
        acc = pool_row(end)
        sums = []
        for k in range(1, max(POOL_WINDOWS)):
            acc = acc + pool_row(end - k)
            if k + 1 in POOL_WINDOWS:
                sums.append(acc)
        win = sums[0]
        for gi in range(1, N_SUB):
            win = jnp.where(grp == gi, sums[gi], win)
        d = win / cnt - a_new[t]
        y_ref[rows, 0:gw] = (_dot(d.astype(BF16), wpool_ref[...]) * pscale_ref[...]).astype(BF16)

        c = None
        for k in range(CONV_WIDTH):
            term = conv_row(t + k) * convw_ref[k:k + 1, :]
            c = term if c is None else c + term
        c = _head_layer_norm(c + convb_ref[...], clng_ref[...], clnb_ref[...])
        y_ref[rows, gw:2 * gw] = (c * jax.nn.sigmoid(c)).astype(BF16)

        vn.append(_head_layer_norm(z_ref[rows, 4 * gw:5 * gw], slng_ref[...], slnb_ref[...]))
        v_out_ref[t] = vn[t]
        s = sgb_ref[t:t + 1, :]
        for u in range(t + 1):
            s = s + sgw_ref[t * n_steps + u:t * n_steps + u + 1, :] * vn[u]
        y_ref[rows, 2 * gw:3 * gw] = (z_ref[rows, 3 * gw:4 * gw] * s).astype(BF16)

        sc = None
        for k in range(SHORT_WIDTH):
            term = short_row(t + k) * shortw_ref[k:k + 1, :]
            sc = term if sc is None else sc + term
        y_ref[rows, 3 * gw:4 * gw] = (z_ref[rows, 5 * gw:6 * gw] * sc).astype(BF16)

    o = _dot(y_ref[...], wout_ref[...])
    out_ref[...] = x + _rms_norm(o, gpost_ref[...])

    for i in range(POOL_BUF):
        pool_out_ref[i] = pool_row(i + n_steps)
    for i in range(CONV_WIDTH - 1):
        conv_out_ref[i] = conv_row(i + n_steps)
    for i in range(SHORT_WIDTH - 1):
        short_out_ref[i] = short_row(i + n_steps)


def _mlp_rows(x_ref, out_ref, acc_ref, gpre_ref, gpost_ref, wup_ref, wdown_ref):
    for r0 in range(0, x_ref.shape[0], FFN_ROWS):
        rows = slice(r0, r0 + FFN_ROWS)
        x = x_ref[rows, :]
        f = _rms_norm(x, gpre_ref[...]).astype(BF16)
        for c in range(D_FF // FF_CHUNK):
            cols = slice(c * FF_CHUNK, (c + 1) * FF_CHUNK)
            u = jnp.maximum(_dot(f, wup_ref[:, cols]), 0.0)
            part = _dot((u * u).astype(BF16), wdown_ref[cols, :])
            if c == 0:
                acc_ref[rows, :] = part
            else:
                acc_ref[rows, :] += part
        out_ref[rows, :] = x + _rms_norm(acc_ref[rows, :], gpost_ref[...])


def _ffn_kernel(layer, n_cast, n_major, x_ref, xs_ref, gpre_ref, gpost_ref, wup_ref, wdown_ref,
                *rest):
    cast_in, rest = rest[:n_cast], rest[n_cast:]
    major_in, rest = rest[:n_major], rest[n_major:]
    (out_ref, outs_ref), rest = rest[:2], rest[2:]
    cast_out, rest = rest[:n_cast], rest[n_cast:]
    n_kinds = N_RELAYOUT_OUTPUTS if n_major else 0
    natural_out, rest = rest[:n_kinds], rest[n_kinds:]
    acc_ref, sems = rest[0], rest[1:]
    gpre_ref, gpost_ref = _layer_views(layer, gpre_ref, gpost_ref)
    i = pl.program_id(0)
    last = pl.num_programs(0) - 1
    relayout = []
    for idx, major in enumerate(major_in):
        src_layer, kind = divmod(idx, n_kinds)
        relayout += _relayout_copies(natural_out[kind], src_layer, major, sems[0].at[kind],
                                     to_major=False)

    @pl.when(i == 0)
    def _():
        for c in relayout:
            c.start()

    @pl.when(i < last)
    def _():
        for src, dst in zip(cast_in, cast_out):
            dst[...] = src[...].astype(BF16)
        _mlp_rows(x_ref, out_ref, acc_ref, gpre_ref, gpost_ref, wup_ref, wdown_ref)

    @pl.when(i == last)
    def _():
        _mlp_rows(xs_ref, outs_ref, acc_ref, gpre_ref, gpost_ref, wup_ref, wdown_ref)
        for c in relayout:
            c.wait()


def _const_spec(shape):
    nd = len(shape)
    return pl.BlockSpec(shape, lambda *_: (0,) * nd, pipeline_mode=pl.Buffered(1))


def _whole_spec(shape):
    nd = len(shape)
    return pl.BlockSpec(shape, lambda *_: (0,) * nd)


def _any_spec():
    return pl.BlockSpec(memory_space=pl.ANY)


def _compiler_params(semantics):
    return pltpu.CompilerParams(dimension_semantics=semantics, vmem_limit_bytes=VMEM_LIMIT_BYTES)


def _prompt_mixer(x, p, w_in, w_out, w_ffn_up, w_ffn_down, sample_states, layer):
    b, s, d = x.shape
    tile = PROMPT_TILE
    consts = (w_in, w_out, POOL_INV_FIRST, POOL_INV_REST,
              p["gpre"], p["gpost"], p["wpool"], p["pscale"], p["conv_w"], p["conv_b"], p["cln_g"],
              p["cln_b"], p["sln_g"], p["sln_b"], p["sgu_w"], p["sgu_bias"], p["short_w"])
    gw = GROUP_WIDTH
    tiles_per_seq = s // tile
    steps = b * tiles_per_seq
    d_ff = w_ffn_up.shape[2]
    ff_slice = d_ff // steps
    assert ff_slice * steps == d_ff and ff_slice % LANES == 0
    step = lambda bi, j: bi * tiles_per_seq + j
    state_spec = lambda n: pl.BlockSpec((None, n, gw), lambda bi, j: (bi, 0, 0))
    return pl.pallas_call(
        functools.partial(_prompt_mixer_kernel, layer),
        grid=(b, tiles_per_seq),
        in_specs=[pl.BlockSpec((None, tile, d), lambda bi, j: (bi, j, 0))]
        + [_const_spec(c.shape) for c in consts]
        + [pl.BlockSpec((None, d, ff_slice), lambda bi, j: (layer, 0, step(bi, j))),
           pl.BlockSpec((None, ff_slice, d), lambda bi, j: (layer, step(bi, j), 0))]
        + [_any_spec() for _ in sample_states],
        out_specs=[pl.BlockSpec((None, tile, d), lambda bi, j: (bi, j, 0)),
                   state_spec(POOL_BUF), state_spec(CONV_WIDTH - 1), state_spec(SHORT_WIDTH - 1),
                   pl.BlockSpec((d, ff_slice), lambda bi, j: (0, step(bi, j))),
                   pl.BlockSpec((ff_slice, d), lambda bi, j: (step(bi, j), 0))]
        + [_any_spec() for _ in sample_states],
        out_shape=[jax.ShapeDtypeStruct((b, s, d), F32),
                   jax.ShapeDtypeStruct((b, POOL_BUF, gw), F32),
                   jax.ShapeDtypeStruct((b, CONV_WIDTH - 1, gw), F32),
                   jax.ShapeDtypeStruct((b, SHORT_WIDTH - 1, gw), F32),
                   jax.ShapeDtypeStruct((d, d_ff), BF16),
                   jax.ShapeDtypeStruct((d_ff, d), BF16)]
        + [jax.ShapeDtypeStruct((st.shape[2], st.shape[1], st.shape[3]), F32) for st in sample_states],
        scratch_shapes=[pltpu.VMEM((tile, IN_WIDTH), F32),
                        pltpu.VMEM((tile, d), BF16),
                        pltpu.VMEM((tile, gw), F32),
                        pltpu.VMEM((CHUNK, N_SUB * CHUNK), BF16),
                        pltpu.VMEM((N_SLABS, POOL_HIST + tile, LANES), F32),
                        pltpu.VMEM((N_SLABS, CONV_HIST + tile, LANES), F32),
                        pltpu.VMEM((N_SLABS, SHORT_HIST + tile, LANES), F32),
                        pltpu.SemaphoreType.DMA((len(sample_states),))],
        compiler_params=_compiler_params(("arbitrary", "arbitrary")),
        name="prompt_mixer",
    )(x, *consts, w_ffn_up, w_ffn_down, *sample_states)


def _sample_mixer(x, pool_st, conv_st, short_st, p, w_in, w_out, layer):
    n, d = x.shape
    n_seq = pool_st.shape[-2]
    n_steps = n // n_seq
    gw = GROUP_WIDTH
    states = (pool_st, conv_st, short_st)
    consts = (w_in, w_out, p["gpre"], p["gpost"], p["wpool"], p["pscale"], p["conv_w"], p["conv_b"],
              p["cln_g"], p["cln_b"], p["sln_g"], p["sln_b"], p["sgu_w4"], p["sgu_bias"], p["short_w"])
    state_spec = lambda st: (_whole_spec(st.shape) if st.ndim == 3 else
                             pl.BlockSpec((None,) + st.shape[1:], lambda i: (layer, 0, 0, 0)))
    out_shape = ([jax.ShapeDtypeStruct((n, d), F32)]
                 + [jax.ShapeDtypeStruct(st.shape[-3:], F32) for st in states]
                 + [jax.ShapeDtypeStruct((n_steps, n_seq, gw), F32)])
    return pl.pallas_call(
        functools.partial(_sample_mixer_kernel, layer),
        grid=(1,),
        in_specs=[_whole_spec(x.shape)] + [state_spec(st) for st in states]
        + [_whole_spec(c.shape) for c in consts],
        out_specs=[_whole_spec(o.shape) for o in out_shape],
        out_shape=out_shape,
        scratch_shapes=[pltpu.VMEM((n, IN_WIDTH), F32), pltpu.VMEM((n, d), BF16)],
        compiler_params=_compiler_params(("arbitrary",)),
        name="sample_mixer",
    )(x, *states, *consts)


def _ffn(x, xs, p, w_up, w_down, layer, cast=(), sample_out=()):
    n, d = x.shape
    ns = xs.shape[0]
    tile = FFN_TILE
    steps = n // tile
    assert steps * tile == n and ns <= tile and ns % FFN_ROWS == 0
    consts = (p["fpre"], p["fpost"], w_up, w_down)
    prompt_tile = lambda i: (jnp.minimum(i, steps - 1), 0)
    cast_in_specs, cast_out_specs, cast_shapes = [], [], []
    for w, w_layer in cast:
        rows = w.shape[1] // steps
        assert rows * steps == w.shape[1] and rows % (2 * SUBLANES) == 0
        cast_in_specs.append(pl.BlockSpec(
            (None, rows, w.shape[2]), lambda i, w_layer=w_layer: (w_layer,) + prompt_tile(i)))
        cast_out_specs.append(pl.BlockSpec((rows, w.shape[2]), prompt_tile))
        cast_shapes.append(jax.ShapeDtypeStruct(w.shape[1:], BF16))
    major = [a for per_layer in sample_out for a in per_layer]
    assert all(len(per_layer) == N_RELAYOUT_OUTPUTS for per_layer in sample_out)
    natural_shapes = [jax.ShapeDtypeStruct((len(sample_out), a.shape[1], a.shape[0], a.shape[2]), F32)
                      for a in (sample_out[0] if sample_out else ())]
    return pl.pallas_call(
        functools.partial(_ffn_kernel, layer, len(cast), len(major)),
        grid=(steps + 1,),
        in_specs=[pl.BlockSpec((tile, d), prompt_tile), _whole_spec(xs.shape)]
        + [_const_spec(c.shape) for c in consts] + cast_in_specs + [_any_spec() for _ in major],
        out_specs=[pl.BlockSpec((tile, d), prompt_tile), _whole_spec(xs.shape)] + cast_out_specs
        + [_any_spec() for _ in natural_shapes],
        out_shape=[jax.ShapeDtypeStruct((n, d), F32), jax.ShapeDtypeStruct((ns, d), F32)] + cast_shapes
        + natural_shapes,
        scratch_shapes=[pltpu.VMEM((tile, d), F32)]
        + ([pltpu.SemaphoreType.DMA((N_RELAYOUT_OUTPUTS,))] if major else []),
        compiler_params=_compiler_params(("arbitrary",)),
        name="ffn",
    )(x, xs, *consts, *[w for w, _ in cast], *major)


def _pool_inverse_counts():
    window = np.repeat(np.asarray(POOL_WINDOWS, np.float32), SUB_DIM)[None, :]
    first = 1.0 / np.minimum(window, np.arange(1, ROW_BLOCK + 1, dtype=np.float32)[:, None])
    return first.astype(np.float32), (1.0 / window).astype(np.float32)


POOL_INV_FIRST, POOL_INV_REST = _pool_inverse_counts()


def _stacked_params(n_steps, norm_mix_pre, norm_mix_post, norm_ffn_pre, norm_ffn_post, w_pool,
                    pool_scale, conv_w, conv_b, conv_ln_g, conv_ln_b, sgu_ln_g, sgu_ln_b, sgu_w,
                    sgu_b, short_w):
    depth = w_pool.shape[0]
    gw = GROUP_WIDTH
    head_of_lane = np.arange(gw) // SUB_DIM
    same_head = head_of_lane[:, None] == head_of_lane[None, :]
    wp = jnp.tile(w_pool.reshape(depth, gw, SUB_DIM), (1, 1, N_SUB))
    wpool = jnp.where(same_head[None], wp, 0.0).astype(BF16)
    bias = jnp.repeat(jnp.swapaxes(sgu_b, 1, 2), SUB_DIM, axis=2)
    causal = np.tril(np.ones((n_steps, n_steps), dtype=bool))
    w4 = jnp.where(causal[None, None], sgu_w[:, :, :n_steps, :n_steps], 0.0)
    w4 = jnp.repeat(jnp.transpose(w4, (0, 2, 3, 1)), SUB_DIM, axis=3)
    w4 = w4.reshape(depth, n_steps * n_steps, gw)
    return dict(gpre=norm_mix_pre, gpost=norm_mix_post, fpre=norm_ffn_pre, fpost=norm_ffn_post,
                wpool=wpool, pscale=pool_scale, conv_w=conv_w, conv_b=conv_b, cln_g=conv_ln_g,
                cln_b=conv_ln_b, sln_g=sgu_ln_g, sln_b=sgu_ln_b, sgu_w=sgu_w, sgu_bias=bias,
                sgu_w4=w4, short_w=short_w)


def kernel(x_prompt, x_sample, state_pool, state_conv, state_short, norm_mix_pre, norm_mix_post, norm_ffn_pre, norm_ffn_post, w_in, w_out, w_pool, pool_scale, conv_w, conv_b, conv_ln_g, conv_ln_b, sgu_ln_g, sgu_ln_b, sgu_w, sgu_b, short_w, w_ffn_up, w_ffn_down):
    depth = w_in.shape[0]
    bp, seq, d = x_prompt.shape
    n_seq, n_steps, _ = x_sample.shape
    assert seq % PROMPT_TILE == 0 and PROMPT_TILE % MATMUL_ROWS == 0 and MATMUL_ROWS % ROW_BLOCK == 0
    assert ROW_BLOCK >= max(POOL_WINDOWS) and n_steps <= CHUNK and PAST_LEN % CHUNK == 0

    p = _stacked_params(n_steps, norm_mix_pre, norm_mix_post, norm_ffn_pre, norm_ffn_post, w_pool,
                        pool_scale, conv_w, conv_b, conv_ln_g, conv_ln_b, sgu_ln_g, sgu_ln_b, sgu_w,
                        sgu_b, short_w)
    yp = x_prompt
    ys = jnp.transpose(x_sample, (1, 0, 2)).reshape(n_steps * n_seq, d)
    hist_major = lambda a: jnp.transpose(a, (0, 2, 1, 3))
    pool_in = hist_major(state_pool)
    prompt_states = [[] for _ in range(3)]
    pool_out, sample_out = [], []
    w_in_b, w_out_b = w_in[0].astype(BF16), w_out[0].astype(BF16)
    for l in range(depth):
        yp, pool_p, conv_p, short_p, w_up_b, w_down_b, conv_in, short_in = _prompt_mixer(
            yp, p, w_in_b, w_out_b, w_ffn_up, w_ffn_down, (state_conv, state_short), l)
        ys, pool_s, *layer_out = _sample_mixer(ys, pool_in, conv_in, short_in, p, w_in_b, w_out_b, l)
        pool_out.append(pool_s)
        sample_out.append(layer_out)
        for lst, val in zip(prompt_states, (pool_p, conv_p, short_p)):
            lst.append(val)
        last = l + 1 == depth
        cast = () if last else ((w_in, l + 1), (w_out, l + 1))
        yp, ys, *extra = _ffn(yp.reshape(bp * seq, d), ys, p, w_up_b, w_down_b, l, cast,
                              sample_out if last else ())
        yp = yp.reshape(bp, seq, d)
        if not last:
            w_in_b, w_out_b = extra
    conv_s, short_s, v_s = extra
    pool_s = hist_major(jnp.stack(pool_out))

    ys = jnp.transpose(ys.reshape(n_steps, n_seq, d), (1, 0, 2))
    pool_p, conv_p, short_p = (jnp.stack(o) for o in prompt_states)
    return (yp, ys, pool_p, pool_s, conv_p, conv_s, short_p, short_s, v_s)
```

```python
import functools

import jax
import jax.numpy as jnp
import numpy as np
from jax import lax
from jax.experimental import pallas as pl
from jax.experimental.pallas import tpu as pltpu

D_MODEL = 1024
GROUP_WIDTH = 256
N_SUB = 4
SUB_DIM = 64
POOL_WINDOWS = (2, 4, 8, 16)
POOL_BUF = 15
CONV_WIDTH = 31
SHORT_WIDTH = 3
CHUNK = 128
D_FF = 4096
EPS = 1e-6
PAST_LEN = 16384
IN_WIDTH = 8 * GROUP_WIDTH

SUBLANES = 8
LANES = 128
N_SLABS = GROUP_WIDTH // LANES
POOL_HIST = 16
CONV_HIST = 32
SHORT_HIST = 8

ROW_BLOCK = CHUNK
MATMUL_ROWS = 512
PROMPT_TILE = 1024
SAMPLE_SEQS = 64
FFN_TILE = 1024
FFN_ROWS = 512
FF_CHUNK = 1024
VMEM_LIMIT_BYTES = 56 * 1024 * 1024

F32 = jnp.float32
BF16 = jnp.bfloat16


def _rms_norm(x, g):
    ms = jnp.mean(x * x, axis=-1, keepdims=True)
    return x * lax.rsqrt(ms + EPS) * g


def _dot(a, b):
    return jnp.dot(a, b, preferred_element_type=F32)


def _head_mean(x, low_head):
    s_low = jnp.sum(jnp.where(low_head, x, 0.0), axis=-1, keepdims=True)
    s_high = jnp.sum(jnp.where(low_head, 0.0, x), axis=-1, keepdims=True)
    return jnp.where(low_head, s_low, s_high) * (1.0 / SUB_DIM)


def _head_layer_norm(x, g, b):
    low_head = lax.broadcasted_iota(jnp.int32, (x.shape[0], LANES), 1) < SUB_DIM
    out = []
    for s in range(N_SLABS):
        lanes = slice(s * LANES, (s + 1) * LANES)
        xs = x[:, lanes]
        xc = xs - _head_mean(xs, low_head)
        var = _head_mean(xc * xc, low_head)
        out.append(xc * lax.rsqrt(var + EPS) * g[:, lanes] + b[:, lanes])
    return jnp.concatenate(out, axis=1)


def _lane_group(shape):
    return jnp.right_shift(lax.broadcasted_iota(jnp.int32, shape, 1), SUB_DIM.bit_length() - 1)


def _pool_window(shape):
    grp = _lane_group(shape)
    w = jnp.full(shape, POOL_WINDOWS[0], jnp.int32)
    for gi in range(1, N_SUB):
        w = jnp.where(grp == gi, POOL_WINDOWS[gi], w)
    return grp, w


def _gating_rhs(vn):
    grp = _lane_group(vn.shape)
    return jnp.concatenate([jnp.where(grp == h, vn, 0.0).astype(BF16) for h in range(N_SUB)], axis=0)


def _cols(group, slab):
    lo = group * GROUP_WIDTH + slab * LANES
    return slice(lo, lo + LANES)


def _trailing_sum(ext_ref, slab, start, n_rows, width):
    acc = ext_ref[slab, start:start + n_rows, :]
    for k in range(1, width):
        acc = acc + ext_ref[slab, start - k:start - k + n_rows, :]
    return acc


def _causal_taps(ext_ref, slab, w_ref, hist, n_taps, r):
    first = hist + r - (n_taps - 1)
    lanes = slice(slab * LANES, (slab + 1) * LANES)
    acc = None
    for k in range(n_taps):
        term = ext_ref[slab, first + k:first + k + ROW_BLOCK, :] * w_ref[k:k + 1, lanes]
        acc = term if acc is None else acc + term
    return acc


def _gate_rows(r, z_ref, y_ref, vn_ref, swcat_ref, sbias_ref):
    rows = slice(r, r + ROW_BLOCK)
    gw = GROUP_WIDTH
    sg = _dot(swcat_ref[...], _gating_rhs(vn_ref[rows, :])) + sbias_ref[...]
    y_ref[rows, 2 * gw:3 * gw] = (z_ref[rows, 3 * gw:4 * gw] * sg).astype(BF16)


def _mix_rows(r, seq_start, z_ref, y_ref, vn_ref, pool_ext, conv_ext, short_ext, wpool_ref,
              pscale_ref, pinv_first_ref, pinv_rest_ref, convw_ref, convb_ref, clng_ref, clnb_ref,
              slng_ref, slnb_ref, shortw_ref):
    rows = slice(r, r + ROW_BLOCK)
    gw = GROUP_WIDTH
    low_head = lax.broadcasted_iota(jnp.int32, (ROW_BLOCK, LANES), 1) < SUB_DIM

    d = []
    for s in range(N_SLABS):
        lanes = slice(s * LANES, (s + 1) * LANES)
        a = z_ref[rows, _cols(0, s)]
        base = POOL_HIST + r
        pool_ext[s, base:base + ROW_BLOCK, :] = a
        small, large = POOL_WINDOWS[2 * s], POOL_WINDOWS[2 * s + 1]
        assert large == 2 * small
        if small % SUBLANES == 0:
            run = _trailing_sum(pool_ext, s, base - small, ROW_BLOCK + small, small)
            s_small = run[small:]
            s_large = s_small + run[:ROW_BLOCK]
        else:
            s_small = _trailing_sum(pool_ext, s, base, ROW_BLOCK, small)
            s_large = s_small + _trailing_sum(pool_ext, s, base - small, ROW_BLOCK, small)
        win = jnp.where(low_head, s_small, s_large)
        inv = pinv_rest_ref[:, lanes]
        if seq_start is not None:
            inv = jnp.where(seq_start, pinv_first_ref[:, lanes], inv)
        d.append(win * inv - a)
    d = jnp.concatenate(d, axis=1).astype(BF16)
    y_ref[rows, 0:gw] = (_dot(d, wpool_ref[...]) * pscale_ref[...]).astype(BF16)

    c = []
    for s in range(N_SLABS):
        g = z_ref[rows, _cols(1, s)] * jax.nn.sigmoid(z_ref[rows, _cols(2, s)])
        conv_ext[s, CONV_HIST + r:CONV_HIST + r + ROW_BLOCK, :] = g
        c.append(_causal_taps(conv_ext, s, convw_ref, CONV_HIST, CONV_WIDTH, r))
    c = jnp.concatenate(c, axis=1) + convb_ref[...]
    c = _head_layer_norm(c, clng_ref[...], clnb_ref[...])
    y_ref[rows, gw:2 * gw] = (c * jax.nn.sigmoid(c)).astype(BF16)

    vn_ref[rows, :] = _head_layer_norm(z_ref[rows, 4 * gw:5 * gw], slng_ref[...], slnb_ref[...])

    for s in range(N_SLABS):
        ch = z_ref[rows, _cols(6, s)] * z_ref[rows, _cols(7, s)]
        short_ext[s, SHORT_HIST + r:SHORT_HIST + r + ROW_BLOCK, :] = ch
        sc = _causal_taps(short_ext, s, shortw_ref, SHORT_HIST, SHORT_WIDTH, r)
        y_ref[rows, _cols(3, s)] = (z_ref[rows, _cols(5, s)] * sc).astype(BF16)


def _layer_views(layer, *refs):
    return [r.at[pl.ds(layer, 1)] if len(r.shape) == 2 else r.at[layer] for r in refs]


def _prompt_mixer_kernel(layer, x_ref, win_ref, wout_ref, pinv_first_ref, pinv_rest_ref,
                         gpre_ref, gpost_ref, wpool_ref, pscale_ref, convw_ref, convb_ref, clng_ref,
                         clnb_ref, slng_ref, slnb_ref, sguw_ref, sbias_ref, shortw_ref,
                         wup_f32_ref, wdown_f32_ref,
                         out_ref, pool_out_ref, conv_out_ref, short_out_ref, wup_ref, wdown_ref,
                         z_ref, y_ref, vn_ref, swcat_ref, pool_ext, conv_ext, short_ext):
    (gpre_ref, gpost_ref, wpool_ref, pscale_ref, convw_ref, convb_ref, clng_ref, clnb_ref, slng_ref,
     slnb_ref, sguw_ref, sbias_ref, shortw_ref) = _layer_views(
         layer, gpre_ref, gpost_ref, wpool_ref, pscale_ref, convw_ref, convb_ref, clng_ref, clnb_ref,
         slng_ref, slnb_ref, sguw_ref, sbias_ref, shortw_ref)
    j = pl.program_id(1)
    tile = x_ref.shape[0]
    wup_ref[...] = wup_f32_ref[...].astype(BF16)
    wdown_ref[...] = wdown_f32_ref[...].astype(BF16)
    causal = (lax.broadcasted_iota(jnp.int32, (CHUNK, CHUNK), 0)
              >= lax.broadcasted_iota(jnp.int32, (CHUNK, CHUNK), 1))
    for hd in range(N_SUB):
        swcat_ref[:, hd * CHUNK:(hd + 1) * CHUNK] = jnp.where(causal, sguw_ref[hd], 0.0).astype(BF16)

    @pl.when(j == 0)
    def _():
        pool_ext[:, 0:POOL_HIST, :] = jnp.zeros((N_SLABS, POOL_HIST, LANES), F32)
        conv_ext[:, 0:CONV_HIST, :] = jnp.zeros((N_SLABS, CONV_HIST, LANES), F32)
        short_ext[:, 0:SHORT_HIST, :] = jnp.zeros((N_SLABS, SHORT_HIST, LANES), F32)

    def project_out(r0):
        rows = slice(r0, r0 + MATMUL_ROWS)
        o = _dot(y_ref[rows, :], wout_ref[...])
        out_ref[rows, :] = x_ref[rows, :] + _rms_norm(o, gpost_ref[...])

    for r0 in range(0, tile, MATMUL_ROWS):
        rows = slice(r0, r0 + MATMUL_ROWS)
        h = _rms_norm(x_ref[rows, :], gpre_ref[...]).astype(BF16)
        z_ref[rows, :] = _dot(h, win_ref[...])
        for r in range(r0, r0 + MATMUL_ROWS, ROW_BLOCK):
            _mix_rows(r, (j == 0) if r == 0 else None, z_ref, y_ref, vn_ref, pool_ext, conv_ext,
                      short_ext, wpool_ref, pscale_ref, pinv_first_ref, pinv_rest_ref, convw_ref,
                      convb_ref, clng_ref, clnb_ref, slng_ref, slnb_ref, shortw_ref)
            if r > 0:
                _gate_rows(r - ROW_BLOCK, z_ref, y_ref, vn_ref, swcat_ref, sbias_ref)
            if r == r0 and r0 > 0:
                project_out(r0 - MATMUL_ROWS)
    _gate_rows(tile - ROW_BLOCK, z_ref, y_ref, vn_ref, swcat_ref, sbias_ref)
    project_out(tile - MATMUL_ROWS)

    pool_ext[:, 0:POOL_HIST, :] = pool_ext[:, tile:tile + POOL_HIST, :]
    conv_ext[:, 0:CONV_HIST, :] = conv_ext[:, tile:tile + CONV_HIST, :]
    short_ext[:, 0:SHORT_HIST, :] = short_ext[:, tile:tile + SHORT_HIST, :]

    @pl.when(j == pl.num_programs(1) - 1)
    def _():
        for s in range(N_SLABS):
            lanes = slice(s * LANES, (s + 1) * LANES)
            pool_out_ref[:, lanes] = pool_ext[s, POOL_HIST - POOL_BUF:POOL_HIST, :]
            conv_out_ref[:, lanes] = conv_ext[s, CONV_HIST - (CONV_WIDTH - 1):CONV_HIST, :]
            short_out_ref[:, lanes] = short_ext[s, SHORT_HIST - (SHORT_WIDTH - 1):SHORT_HIST, :]


def _sample_mixer_kernel(layer, x_ref, pool_in_ref, conv_in_ref, short_in_ref, win_ref, wout_ref,
                         gpre_ref, gpost_ref, wpool_ref, pscale_ref, convw_ref, convb_ref, clng_ref,
                         clnb_ref, slng_ref, slnb_ref, sgw_ref, sgb_ref, shortw_ref,
                         out_ref, pool_out_ref, conv_out_ref, short_out_ref, v_out_ref,
                         z_ref, y_ref):
    (gpre_ref, gpost_ref, wpool_ref, pscale_ref, convw_ref, convb_ref, clng_ref, clnb_ref, slng_ref,
     slnb_ref, sgw_ref, sgb_ref, shortw_ref) = _layer_views(
         layer, gpre_ref, gpost_ref, wpool_ref, pscale_ref, convw_ref, convb_ref, clng_ref, clnb_ref,
         slng_ref, slnb_ref, sgw_ref, sgb_ref, shortw_ref)
    n_steps, n_seq = v_out_ref.shape[0], v_out_ref.shape[1]
    gw = GROUP_WIDTH
    x = x_ref[...].reshape(n_steps * n_seq, x_ref.shape[2])
    h = _rms_norm(x, gpre_ref[...]).astype(BF16)
    z_ref[...] = _dot(h, win_ref[...])
    grp, w = _pool_window((n_seq, gw))
    cnt = jnp.minimum(w, PAST_LEN + 1).astype(F32)

    def slab(t):
        return slice(t * n_seq, (t + 1) * n_seq)

    a_new = [z_ref[slab(t), 0:gw] for t in range(n_steps)]
    g_new = [z_ref[slab(t), gw:2 * gw] * jax.nn.sigmoid(z_ref[slab(t), 2 * gw:3 * gw])
             for t in range(n_steps)]
    ch_new = [z_ref[slab(t), 6 * gw:7 * gw] * z_ref[slab(t), 7 * gw:8 * gw]
              for t in range(n_steps)]

    def pool_row(i):
        return pool_in_ref[i] if i < POOL_BUF else a_new[i - POOL_BUF]

    def conv_row(i):
        return conv_in_ref[i] if i < CONV_WIDTH - 1 else g_new[i - (CONV_WIDTH - 1)]

    def short_row(i):
        return short_in_ref[i] if i < SHORT_WIDTH - 1 else ch_new[i - (SHORT_WIDTH - 1)]

    vn = []
    for t in range(n_steps):
        rows = slab(t)
        end = POOL_BUF + t
        acc = pool_row(end)
        sums = []
        for k in range(1, max(POOL_WINDOWS)):
            acc = acc + pool_row(end - k)
            if k + 1 in POOL_WINDOWS:
                sums.append(acc)
        win = sums[0]
        for gi in range(1, N_SUB):
            win = jnp.where(grp == gi, sums[gi], win)
        d = win / cnt - a_new[t]
        y_ref[rows, 0:gw] = (_dot(d.astype(BF16), wpool_ref[...]) * pscale_ref[...]).astype(BF16)

        c = None
        for k in range(CONV_WIDTH):
            term = conv_row(t + k) * convw_ref[k:k + 1, :]
            c = term if c is None else c + term
        c = _head_layer_norm(c + convb_ref[...], clng_ref[...], clnb_ref[...])
        y_ref[rows, gw:2 * gw] = (c * jax.nn.sigmoid(c)).astype(BF16)

        vn.append(_head_layer_norm(z_ref[rows, 4 * gw:5 * gw], slng_ref[...], slnb_ref[...]))
        v_out_ref[t] = vn[t]
        s = sgb_ref[t:t + 1, :]
        for u in range(t + 1):
            s = s + sgw_ref[t * n_steps + u:t * n_steps + u + 1, :] * vn[u]
        y_ref[rows, 2 * gw:3 * gw] = (z_ref[rows, 3 * gw:4 * gw] * s).astype(BF16)

        sc = None
        for k in range(SHORT_WIDTH):
            term = short_row(t + k) * shortw_ref[k:k + 1, :]
            sc = term if sc is None else sc + term
        y_ref[rows, 3 * gw:4 * gw] = (z_ref[rows, 5 * gw:6 * gw] * sc).astype(BF16)

    o = _dot(y_ref[...], wout_ref[...])
    out_ref[...] = (x + _rms_norm(o, gpost_ref[...])).reshape(out_ref.shape)

    for i in range(POOL_BUF):
        pool_out_ref[i] = pool_row(i + n_steps)
    for i in range(CONV_WIDTH - 1):
        conv_out_ref[i] = conv_row(i + n_steps)
    for i in range(SHORT_WIDTH - 1):
        short_out_ref[i] = short_row(i + n_steps)


def _mlp_rows(x_ref, out_ref, acc_ref, gpre_ref, gpost_ref, wup_ref, wdown_ref):
    for r0 in range(0, x_ref.shape[0], FFN_ROWS):
        rows = slice(r0, r0 + FFN_ROWS)
        x = x_ref[rows, :]
        f = _rms_norm(x, gpre_ref[...]).astype(BF16)
        for c in range(D_FF // FF_CHUNK):
            cols = slice(c * FF_CHUNK, (c + 1) * FF_CHUNK)
            u = jnp.maximum(_dot(f, wup_ref[:, cols]), 0.0)
            part = _dot((u * u).astype(BF16), wdown_ref[cols, :])
            if c == 0:
                acc_ref[rows, :] = part
            else:
                acc_ref[rows, :] += part
        out_ref[rows, :] = x + _rms_norm(acc_ref[rows, :], gpost_ref[...])


def _ffn_kernel(layer, n_cast, x_ref, xs_ref, gpre_ref, gpost_ref, wup_ref, wdown_ref, *rest):
    cast_in, out_ref, outs_ref = rest[:n_cast], rest[n_cast], rest[n_cast + 1]
    cast_out, acc_ref = rest[n_cast + 2:-1], rest[-1]
    gpre_ref, gpost_ref = _layer_views(layer, gpre_ref, gpost_ref)
    i = pl.program_id(0)
    last = pl.num_programs(0) - 1

    @pl.when(i < last)
    def _():
        for src, dst in zip(cast_in, cast_out):
            dst[...] = src[...].astype(BF16)
        _mlp_rows(x_ref, out_ref, acc_ref, gpre_ref, gpost_ref, wup_ref, wdown_ref)

    @pl.when(i == last)
    def _():
        _mlp_rows(xs_ref, outs_ref, acc_ref, gpre_ref, gpost_ref, wup_ref, wdown_ref)


def _const_spec(shape):
    nd = len(shape)
    return pl.BlockSpec(shape, lambda *_: (0,) * nd, pipeline_mode=pl.Buffered(1))


def _whole_spec(shape):
    nd = len(shape)
    return pl.BlockSpec(shape, lambda *_: (0,) * nd)


def _compiler_params(semantics):
    return pltpu.CompilerParams(dimension_semantics=semantics, vmem_limit_bytes=VMEM_LIMIT_BYTES)


def _prompt_mixer(x, p, w_in, w_out, w_ffn_up, w_ffn_down, layer):
    b, s, d = x.shape
    tile = PROMPT_TILE
    consts = (w_in, w_out, POOL_INV_FIRST, POOL_INV_REST,
              p["gpre"], p["gpost"], p["wpool"], p["pscale"], p["conv_w"], p["conv_b"], p["cln_g"],
              p["cln_b"], p["sln_g"], p["sln_b"], p["sgu_w"], p["sgu_bias"], p["short_w"])
    gw = GROUP_WIDTH
    tiles_per_seq = s // tile
    steps = b * tiles_per_seq
    d_ff = w_ffn_up.shape[2]
    ff_slice = d_ff // steps
    assert ff_slice * steps == d_ff and ff_slice % LANES == 0
    step = lambda bi, j: bi * tiles_per_seq + j
    state_spec = lambda n: pl.BlockSpec((None, n, gw), lambda bi, j: (bi, 0, 0))
    return pl.pallas_call(
        functools.partial(_prompt_mixer_kernel, layer),
        grid=(b, tiles_per_seq),
        in_specs=[pl.BlockSpec((None, tile, d), lambda bi, j: (bi, j, 0))]
        + [_const_spec(c.shape) for c in consts]
        + [pl.BlockSpec((None, d, ff_slice), lambda bi, j: (layer, 0, step(bi, j))),
           pl.BlockSpec((None, ff_slice, d), lambda bi, j: (layer, step(bi, j), 0))],
        out_specs=[pl.BlockSpec((None, tile, d), lambda bi, j: (bi, j, 0)),
                   state_spec(POOL_BUF), state_spec(CONV_WIDTH - 1), state_spec(SHORT_WIDTH - 1),
                   pl.BlockSpec((d, ff_slice), lambda bi, j: (0, step(bi, j))),
                   pl.BlockSpec((ff_slice, d), lambda bi, j: (step(bi, j), 0))],
        out_shape=[jax.ShapeDtypeStruct((b, s, d), F32),
                   jax.ShapeDtypeStruct((b, POOL_BUF, gw), F32),
                   jax.ShapeDtypeStruct((b, CONV_WIDTH - 1, gw), F32),
                   jax.ShapeDtypeStruct((b, SHORT_WIDTH - 1, gw), F32),
                   jax.ShapeDtypeStruct((d, d_ff), BF16),
                   jax.ShapeDtypeStruct((d_ff, d), BF16)],
        scratch_shapes=[pltpu.VMEM((tile, IN_WIDTH), F32),
                        pltpu.VMEM((tile, d), BF16),
                        pltpu.VMEM((tile, gw), F32),
                        pltpu.VMEM((CHUNK, N_SUB * CHUNK), BF16),
                        pltpu.VMEM((N_SLABS, POOL_HIST + tile, LANES), F32),
                        pltpu.VMEM((N_SLABS, CONV_HIST + tile, LANES), F32),
                        pltpu.VMEM((N_SLABS, SHORT_HIST + tile, LANES), F32)],
        compiler_params=_compiler_params(("arbitrary", "arbitrary")),
        name="prompt_mixer",
    )(x, *consts, w_ffn_up, w_ffn_down)


def _sample_mixer(x, pool_st, conv_st, short_st, p, w_in, w_out, layer):
    n_steps, n_seq, d = x.shape
    group = SAMPLE_SEQS
    assert n_seq % group == 0 and group % SUBLANES == 0
    gw = GROUP_WIDTH
    states = (pool_st, conv_st, short_st)
    consts = (w_in, w_out, p["gpre"], p["gpost"], p["wpool"], p["pscale"], p["conv_w"], p["conv_b"],
              p["cln_g"], p["cln_b"], p["sln_g"], p["sln_b"], p["sgu_w4"], p["sgu_bias"], p["short_w"])
    group_spec = lambda a: pl.BlockSpec((a.shape[0], group, a.shape[2]), lambda i: (0, i, 0))
    state_spec = lambda st: pl.BlockSpec((None, st.shape[1], group, gw), lambda i: (layer, 0, i, 0))
    out_shape = ([jax.ShapeDtypeStruct((n_steps, n_seq, d), F32)]
                 + [jax.ShapeDtypeStruct(st.shape[1:], F32) for st in states]
                 + [jax.ShapeDtypeStruct((n_steps, n_seq, gw), F32)])
    rows = n_steps * group
    return pl.pallas_call(
        functools.partial(_sample_mixer_kernel, layer),
        grid=(n_seq // group,),
        in_specs=[group_spec(x)] + [state_spec(st) for st in states]
        + [_const_spec(c.shape) for c in consts],
        out_specs=[group_spec(o) for o in out_shape],
        out_shape=out_shape,
        scratch_shapes=[pltpu.VMEM((rows, IN_WIDTH), F32), pltpu.VMEM((rows, d), BF16)],
        compiler_params=_compiler_params(("arbitrary",)),
        name="sample_mixer",
    )(x, *states, *consts)


def _ffn(x, xs, p, w_up, w_down, layer, cast=()):
    n, d = x.shape
    ns = xs.shape[0]
    tile = FFN_TILE
    steps = n // tile
    assert steps * tile == n and ns <= tile and ns % FFN_ROWS == 0
    consts = (p["fpre"], p["fpost"], w_up, w_down)
    prompt_tile = lambda i: (jnp.minimum(i, steps - 1), 0)
    cast_in_specs, cast_out_specs, cast_shapes = [], [], []
    for w, w_layer in cast:
        rows = w.shape[1] // steps
        assert rows * steps == w.shape[1] and rows % (2 * SUBLANES) == 0
        cast_in_specs.append(pl.BlockSpec(
            (None, rows, w.shape[2]), lambda i, w_layer=w_layer: (w_layer,) + prompt_tile(i)))
        cast_out_specs.append(pl.BlockSpec((rows, w.shape[2]), prompt_tile))
        cast_shapes.append(jax.ShapeDtypeStruct(w.shape[1:], BF16))
    return pl.pallas_call(
        functools.partial(_ffn_kernel, layer, len(cast)),
        grid=(steps + 1,),
        in_specs=[pl.BlockSpec((tile, d), prompt_tile), _whole_spec(xs.shape)]
        + [_const_spec(c.shape) for c in consts] + cast_in_specs,
        out_specs=[pl.BlockSpec((tile, d), prompt_tile), _whole_spec(xs.shape)] + cast_out_specs,
        out_shape=[jax.ShapeDtypeStruct((n, d), F32), jax.ShapeDtypeStruct((ns, d), F32)] + cast_shapes,
        scratch_shapes=[pltpu.VMEM((tile, d), F32)],
        compiler_params=_compiler_params(("arbitrary",)),
        name="ffn",
    )(x, xs, *consts, *[w for w, _ in cast])


def _pool_inverse_counts():
    window = np.repeat(np.asarray(POOL_WINDOWS, np.float32), SUB_DIM)[None, :]
    first = 1.0 / np.minimum(window, np.arange(1, ROW_BLOCK + 1, dtype=np.float32)[:, None])
    return first.astype(np.float32), (1.0 / window).astype(np.float32)


POOL_INV_FIRST, POOL_INV_REST = _pool_inverse_counts()


def _stacked_params(n_steps, norm_mix_pre, norm_mix_post, norm_ffn_pre, norm_ffn_post, w_pool,
                    pool_scale, conv_w, conv_b, conv_ln_g, conv_ln_b, sgu_ln_g, sgu_ln_b, sgu_w,
                    sgu_b, short_w):
    depth = w_pool.shape[0]
    gw = GROUP_WIDTH
    head_of_lane = np.arange(gw) // SUB_DIM
    same_head = head_of_lane[:, None] == head_of_lane[None, :]
    wp = jnp.tile(w_pool.reshape(depth, gw, SUB_DIM), (1, 1, N_SUB))
    wpool = jnp.where(same_head[None], wp, 0.0).astype(BF16)
    bias = jnp.repeat(jnp.swapaxes(sgu_b, 1, 2), SUB_DIM, axis=2)
    causal = np.tril(np.ones((n_steps, n_steps), dtype=bool))
    w4 = jnp.where(causal[None, None], sgu_w[:, :, :n_steps, :n_steps], 0.0)
    w4 = jnp.repeat(jnp.transpose(w4, (0, 2, 3, 1)), SUB_DIM, axis=3)
    w4 = w4.reshape(depth, n_steps * n_steps, gw)
    return dict(gpre=norm_mix_pre, gpost=norm_mix_post, fpre=norm_ffn_pre, fpost=norm_ffn_post,
                wpool=wpool, pscale=pool_scale, conv_w=conv_w, conv_b=conv_b, cln_g=conv_ln_g,
                cln_b=conv_ln_b, sln_g=sgu_ln_g, sln_b=sgu_ln_b, sgu_w=sgu_w, sgu_bias=bias,
                sgu_w4=w4, short_w=short_w)


def kernel(x_prompt, x_sample, state_pool, state_conv, state_short, norm_mix_pre, norm_mix_post, norm_ffn_pre, norm_ffn_post, w_in, w_out, w_pool, pool_scale, conv_w, conv_b, conv_ln_g, conv_ln_b, sgu_ln_g, sgu_ln_b, sgu_w, sgu_b, short_w, w_ffn_up, w_ffn_down):
    depth = w_in.shape[0]
    bp, seq, d = x_prompt.shape
    n_seq, n_steps, _ = x_sample.shape
    assert seq % PROMPT_TILE == 0 and PROMPT_TILE % MATMUL_ROWS == 0 and MATMUL_ROWS % ROW_BLOCK == 0
    assert ROW_BLOCK >= max(POOL_WINDOWS) and n_steps <= CHUNK and PAST_LEN % CHUNK == 0

    p = _stacked_params(n_steps, norm_mix_pre, norm_mix_post, norm_ffn_pre, norm_ffn_post, w_pool,
                        pool_scale, conv_w, conv_b, conv_ln_g, conv_ln_b, sgu_ln_g, sgu_ln_b, sgu_w,
                        sgu_b, short_w)
    yp = x_prompt
    ys = jnp.transpose(x_sample, (1, 0, 2))
    hist_major = lambda a: jnp.transpose(a, (0, 2, 1, 3))
    pool_in, conv_in, short_in = hist_major(state_pool), hist_major(state_conv), hist_major(state_short)
    prompt_states = [[] for _ in range(3)]
    sample_states = [[] for _ in range(4)]
    w_in_b, w_out_b = w_in[0].astype(BF16), w_out[0].astype(BF16)
    for l in range(depth):
        yp, pool_p, conv_p, short_p, w_up_b, w_down_b = _prompt_mixer(
            yp, p, w_in_b, w_out_b, w_ffn_up, w_ffn_down, l)
        ys, pool_s, conv_s, short_s, v_s = _sample_mixer(
            ys, pool_in, conv_in, short_in, p, w_in_b, w_out_b, l)
        cast = ((w_in, l + 1), (w_out, l + 1)) if l + 1 < depth else ()
        yp, ys, *w_next = _ffn(yp.reshape(bp * seq, d), ys.reshape(n_steps * n_seq, d), p, w_up_b,
                               w_down_b, l, cast)
        yp, ys = yp.reshape(bp, seq, d), ys.reshape(n_steps, n_seq, d)
        if w_next:
            w_in_b, w_out_b = w_next

        for lst, val in zip(prompt_states, (pool_p, conv_p, short_p)):
            lst.append(val)
        for lst, val in zip(sample_states, (pool_s, conv_s, short_s, v_s)):
            lst.append(val)

    ys = jnp.transpose(ys, (1, 0, 2))
    pool_p, conv_p, short_p = (jnp.stack(o) for o in prompt_states)
    pool_s, conv_s, short_s, v_s = (hist_major(jnp.stack(o)) for o in sample_states)
    return (yp, ys, pool_p, pool_s, conv_p, conv_s, short_p, short_s, v_s)
```

```python
import functools

import jax
import jax.numpy as jnp
import numpy as np
from jax import lax
from jax.experimental import pallas as pl
from jax.experimental.pallas import tpu as pltpu

D_MODEL = 1024
GROUP_WIDTH = 256
N_SUB = 4
SUB_DIM = 64
POOL_WINDOWS = (2, 4, 8, 16)
POOL_BUF = 15
CONV_WIDTH = 31
SHORT_WIDTH = 3
CHUNK = 128
D_FF = 4096
EPS = 1e-6
PAST_LEN = 16384
IN_WIDTH = 8 * GROUP_WIDTH

SUBLANES = 8
LANES = 128
N_SLABS = GROUP_WIDTH // LANES
POOL_HIST = 16
CONV_HIST = 32
SHORT_HIST = 8

ROW_BLOCK = CHUNK
MATMUL_ROWS = 512
PROMPT_TILE = 1024
SAMPLE_SEQS = 64
FFN_TILE = 1024
FFN_ROWS = 512
FF_CHUNK = 1024
VMEM_LIMIT_BYTES = 56 * 1024 * 1024

F32 = jnp.float32
BF16 = jnp.bfloat16


def _rms_norm(x, g):
    ms = jnp.mean(x * x, axis=-1, keepdims=True)
    return x * lax.rsqrt(ms + EPS) * g


def _dot(a, b):
    return jnp.dot(a, b, preferred_element_type=F32)


def _head_mean(x, low_head):
    s_low = jnp.sum(jnp.where(low_head, x, 0.0), axis=-1, keepdims=True)
    s_high = jnp.sum(jnp.where(low_head, 0.0, x), axis=-1, keepdims=True)
    return jnp.where(low_head, s_low, s_high) * (1.0 / SUB_DIM)


def _head_layer_norm(x, g, b):
    low_head = lax.broadcasted_iota(jnp.int32, (x.shape[0], LANES), 1) < SUB_DIM
    out = []
    for s in range(N_SLABS):
        lanes = slice(s * LANES, (s + 1) * LANES)
        xs = x[:, lanes]
        xc = xs - _head_mean(xs, low_head)
        var = _head_mean(xc * xc, low_head)
        out.append(xc * lax.rsqrt(var + EPS) * g[:, lanes] + b[:, lanes])
    return jnp.concatenate(out, axis=1)


def _lane_group(shape):
    return jnp.right_shift(lax.broadcasted_iota(jnp.int32, shape, 1), SUB_DIM.bit_length() - 1)


def _pool_window(shape):
    grp = _lane_group(shape)
    w = jnp.full(shape, POOL_WINDOWS[0], jnp.int32)
    for gi in range(1, N_SUB):
        w = jnp.where(grp == gi, POOL_WINDOWS[gi], w)
    return grp, w


def _gating_rhs(vn):
    grp = _lane_group(vn.shape)
    return jnp.concatenate([jnp.where(grp == h, vn, 0.0).astype(BF16) for h in range(N_SUB)], axis=0)


def _cols(group, slab):
    lo = group * GROUP_WIDTH + slab * LANES
    return slice(lo, lo + LANES)


def _trailing_sum(ext_ref, slab, start, n_rows, width):
    acc = ext_ref[slab, start:start + n_rows, :]
    for k in range(1, width):
        acc = acc + ext_ref[slab, start - k:start - k + n_rows, :]
    return acc


def _causal_taps(ext_ref, slab, w_ref, hist, n_taps, r):
    first = hist + r - (n_taps - 1)
    lanes = slice(slab * LANES, (slab + 1) * LANES)
    acc = None
    for k in range(n_taps):
        term = ext_ref[slab, first + k:first + k + ROW_BLOCK, :] * w_ref[k:k + 1, lanes]
        acc = term if acc is None else acc + term
    return acc


def _gate_rows(r, z_ref, y_ref, vn_ref, swcat_ref, sbias_ref):
    rows = slice(r, r + ROW_BLOCK)
    gw = GROUP_WIDTH
    sg = _dot(swcat_ref[...], _gating_rhs(vn_ref[rows, :])) + sbias_ref[...]
    y_ref[rows, 2 * gw:3 * gw] = (z_ref[rows, 3 * gw:4 * gw] * sg).astype(BF16)


def _mix_rows(r, seq_start, z_ref, y_ref, vn_ref, pool_ext, conv_ext, short_ext, wpool_ref,
              pscale_ref, pinv_first_ref, pinv_rest_ref, convw_ref, convb_ref, clng_ref, clnb_ref,
              slng_ref, slnb_ref, shortw_ref):
    rows = slice(r, r + ROW_BLOCK)
    gw = GROUP_WIDTH
    low_head = lax.broadcasted_iota(jnp.int32, (ROW_BLOCK, LANES), 1) < SUB_DIM

    d = []
    for s in range(N_SLABS):
        lanes = slice(s * LANES, (s + 1) * LANES)
        a = z_ref[rows, _cols(0, s)]
        base = POOL_HIST + r
        pool_ext[s, base:base + ROW_BLOCK, :] = a
        small, large = POOL_WINDOWS[2 * s], POOL_WINDOWS[2 * s + 1]
        assert large == 2 * small
        if small % SUBLANES == 0:
            run = _trailing_sum(pool_ext, s, base - small, ROW_BLOCK + small, small)
            s_small = run[small:]
            s_large = s_small + run[:ROW_BLOCK]
        else:
            s_small = _trailing_sum(pool_ext, s, base, ROW_BLOCK, small)
            s_large = s_small + _trailing_sum(pool_ext, s, base - small, ROW_BLOCK, small)
        win = jnp.where(low_head, s_small, s_large)
        inv = pinv_rest_ref[:, lanes]
        if seq_start is not None:
            inv = jnp.where(seq_start, pinv_first_ref[:, lanes], inv)
        d.append(win * inv - a)
    d = jnp.concatenate(d, axis=1).astype(BF16)
    y_ref[rows, 0:gw] = (_dot(d, wpool_ref[...]) * pscale_ref[...]).astype(BF16)

    c = []
    for s in range(N_SLABS):
        g = z_ref[rows, _cols(1, s)] * jax.nn.sigmoid(z_ref[rows, _cols(2, s)])
        conv_ext[s, CONV_HIST + r:CONV_HIST + r + ROW_BLOCK, :] = g
        c.append(_causal_taps(conv_ext, s, convw_ref, CONV_HIST, CONV_WIDTH, r))
    c = jnp.concatenate(c, axis=1) + convb_ref[...]
    c = _head_layer_norm(c, clng_ref[...], clnb_ref[...])
    y_ref[rows, gw:2 * gw] = (c * jax.nn.sigmoid(c)).astype(BF16)

    vn_ref[rows, :] = _head_layer_norm(z_ref[rows, 4 * gw:5 * gw], slng_ref[...], slnb_ref[...])

    for s in range(N_SLABS):
        ch = z_ref[rows, _cols(6, s)] * z_ref[rows, _cols(7, s)]
        short_ext[s, SHORT_HIST + r:SHORT_HIST + r + ROW_BLOCK, :] = ch
        sc = _causal_taps(short_ext, s, shortw_ref, SHORT_HIST, SHORT_WIDTH, r)
        y_ref[rows, _cols(3, s)] = (z_ref[rows, _cols(5, s)] * sc).astype(BF16)


def _layer_views(layer, *refs):
    return [r.at[pl.ds(layer, 1)] if len(r.shape) == 2 else r.at[layer] for r in refs]


def _prompt_mixer_kernel(layer, x_ref, win_ref, wout_ref, pinv_first_ref, pinv_rest_ref,
                         gpre_ref, gpost_ref, wpool_ref, pscale_ref, convw_ref, convb_ref, clng_ref,
                         clnb_ref, slng_ref, slnb_ref, sguw_ref, sbias_ref, shortw_ref,
                         wup_f32_ref, wdown_f32_ref,
                         out_ref, pool_out_ref, conv_out_ref, short_out_ref, wup_ref, wdown_ref,
                         z_ref, y_ref, vn_ref, swcat_ref, pool_ext, conv_ext, short_ext):
    (gpre_ref, gpost_ref, wpool_ref, pscale_ref, convw_ref, convb_ref, clng_ref, clnb_ref, slng_ref,
     slnb_ref, sguw_ref, sbias_ref, shortw_ref) = _layer_views(
         layer, gpre_ref, gpost_ref, wpool_ref, pscale_ref, convw_ref, convb_ref, clng_ref, clnb_ref,
         slng_ref, slnb_ref, sguw_ref, sbias_ref, shortw_ref)
    j = pl.program_id(1)
    tile = x_ref.shape[0]
    wup_ref[...] = wup_f32_ref[...].astype(BF16)
    wdown_ref[...] = wdown_f32_ref[...].astype(BF16)
    causal = (lax.broadcasted_iota(jnp.int32, (CHUNK, CHUNK), 0)
              >= lax.broadcasted_iota(jnp.int32, (CHUNK, CHUNK), 1))
    for hd in range(N_SUB):
        swcat_ref[:, hd * CHUNK:(hd + 1) * CHUNK] = jnp.where(causal, sguw_ref[hd], 0.0).astype(BF16)

    @pl.when(j == 0)
    def _():
        pool_ext[:, 0:POOL_HIST, :] = jnp.zeros((N_SLABS, POOL_HIST, LANES), F32)
        conv_ext[:, 0:CONV_HIST, :] = jnp.zeros((N_SLABS, CONV_HIST, LANES), F32)
        short_ext[:, 0:SHORT_HIST, :] = jnp.zeros((N_SLABS, SHORT_HIST, LANES), F32)

    def project_out(r0):
        rows = slice(r0, r0 + MATMUL_ROWS)
        o = _dot(y_ref[rows, :], wout_ref[...])
        out_ref[rows, :] = x_ref[rows, :] + _rms_norm(o, gpost_ref[...])

    for r0 in range(0, tile, MATMUL_ROWS):
        rows = slice(r0, r0 + MATMUL_ROWS)
        h = _rms_norm(x_ref[rows, :], gpre_ref[...]).astype(BF16)
        z_ref[rows, :] = _dot(h, win_ref[...])
        for r in range(r0, r0 + MATMUL_ROWS, ROW_BLOCK):
            _mix_rows(r, (j == 0) if r == 0 else None, z_ref, y_ref, vn_ref, pool_ext, conv_ext,
                      short_ext, wpool_ref, pscale_ref, pinv_first_ref, pinv_rest_ref, convw_ref,
                      convb_ref, clng_ref, clnb_ref, slng_ref, slnb_ref, shortw_ref)
            if r > 0:
                _gate_rows(r - ROW_BLOCK, z_ref, y_ref, vn_ref, swcat_ref, sbias_ref)
            if r == r0 and r0 > 0:
                project_out(r0 - MATMUL_ROWS)
    _gate_rows(tile - ROW_BLOCK, z_ref, y_ref, vn_ref, swcat_ref, sbias_ref)
    project_out(tile - MATMUL_ROWS)

    pool_ext[:, 0:POOL_HIST, :] = pool_ext[:, tile:tile + POOL_HIST, :]
    conv_ext[:, 0:CONV_HIST, :] = conv_ext[:, tile:tile + CONV_HIST, :]
    short_ext[:, 0:SHORT_HIST, :] = short_ext[:, tile:tile + SHORT_HIST, :]

    @pl.when(j == pl.num_programs(1) - 1)
    def _():
        for s in range(N_SLABS):
            lanes = slice(s * LANES, (s + 1) * LANES)
            pool_out_ref[:, lanes] = pool_ext[s, POOL_HIST - POOL_BUF:POOL_HIST, :]
            conv_out_ref[:, lanes] = conv_ext[s, CONV_HIST - (CONV_WIDTH - 1):CONV_HIST, :]
            short_out_ref[:, lanes] = short_ext[s, SHORT_HIST - (SHORT_WIDTH - 1):SHORT_HIST, :]


def _sample_mixer_kernel(layer, x_ref, pool_in_ref, conv_in_ref, short_in_ref, win_ref, wout_ref,
                         gpre_ref, gpost_ref, wpool_ref, pscale_ref, convw_ref, convb_ref, clng_ref,
                         clnb_ref, slng_ref, slnb_ref, sgw_ref, sgb_ref, shortw_ref,
                         out_ref, pool_out_ref, conv_out_ref, short_out_ref, v_out_ref,
                         z_ref, y_ref):
    (gpre_ref, gpost_ref, wpool_ref, pscale_ref, convw_ref, convb_ref, clng_ref, clnb_ref, slng_ref,
     slnb_ref, sgw_ref, sgb_ref, shortw_ref) = _layer_views(
         layer, gpre_ref, gpost_ref, wpool_ref, pscale_ref, convw_ref, convb_ref, clng_ref, clnb_ref,
         slng_ref, slnb_ref, sgw_ref, sgb_ref, shortw_ref)
    n_steps, n_seq = v_out_ref.shape[0], v_out_ref.shape[1]
    gw = GROUP_WIDTH
    x = x_ref[...].reshape(n_steps * n_seq, x_ref.shape[2])
    h = _rms_norm(x, gpre_ref[...]).astype(BF16)
    z_ref[...] = _dot(h, win_ref[...])
    grp, w = _pool_window((n_seq, gw))
    cnt = jnp.minimum(w, PAST_LEN + 1).astype(F32)

    def slab(t):
        return slice(t * n_seq, (t + 1) * n_seq)

    a_new = [z_ref[slab(t), 0:gw] for t in range(n_steps)]
    g_new = [z_ref[slab(t), gw:2 * gw] * jax.nn.sigmoid(z_ref[slab(t), 2 * gw:3 * gw])
             for t in range(n_steps)]
    ch_new = [z_ref[slab(t), 6 * gw:7 * gw] * z_ref[slab(t), 7 * gw:8 * gw]
              for t in range(n_steps)]

    def pool_row(i):
        return pool_in_ref[i] if i < POOL_BUF else a_new[i - POOL_BUF]

    def conv_row(i):
        return conv_in_ref[i] if i < CONV_WIDTH - 1 else g_new[i - (CONV_WIDTH - 1)]

    def short_row(i):
        return short_in_ref[i] if i < SHORT_WIDTH - 1 else ch_new[i - (SHORT_WIDTH - 1)]

    vn = []
    for t in range(n_steps):
        rows = slab(t)
        end = POOL_BUF + t
        acc = pool_row(end)
        sums = []
        for k in range(1, max(POOL_WINDOWS)):
            acc = acc + pool_row(end - k)
            if k + 1 in POOL_WINDOWS:
                sums.append(acc)
        win = sums[0]
        for gi in range(1, N_SUB):
            win = jnp.where(grp == gi, sums[gi], win)
        d = win / cnt - a_new[t]
        y_ref[rows, 0:gw] = (_dot(d.astype(BF16), wpool_ref[...]) * pscale_ref[...]).astype(BF16)

        c = None
        for k in range(CONV_WIDTH):
            term = conv_row(t + k) * convw_ref[k:k + 1, :]
            c = term if c is None else c + term
        c = _head_layer_norm(c + convb_ref[...], clng_ref[...], clnb_ref[...])
        y_ref[rows, gw:2 * gw] = (c * jax.nn.sigmoid(c)).astype(BF16)

        vn.append(_head_layer_norm(z_ref[rows, 4 * gw:5 * gw], slng_ref[...], slnb_ref[...]))
        v_out_ref[t] = vn[t]
        s = sgb_ref[t:t + 1, :]
        for u in range(t + 1):
            s = s + sgw_ref[t * n_steps + u:t * n_steps + u + 1, :] * vn[u]
        y_ref[rows, 2 * gw:3 * gw] = (z_ref[rows, 3 * gw:4 * gw] * s).astype(BF16)

        sc = None
        for k in range(SHORT_WIDTH):
            term = short_row(t + k) * shortw_ref[k:k + 1, :]
            sc = term if sc is None else sc + term
        y_ref[rows, 3 * gw:4 * gw] = (z_ref[rows, 5 * gw:6 * gw] * sc).astype(BF16)

    o = _dot(y_ref[...], wout_ref[...])
    out_ref[...] = (x + _rms_norm(o, gpost_ref[...])).reshape(out_ref.shape)

    for i in range(POOL_BUF):
        pool_out_ref[i] = pool_row(i + n_steps)
    for i in range(CONV_WIDTH - 1):
        conv_out_ref[i] = conv_row(i + n_steps)
    for i in range(SHORT_WIDTH - 1):
        short_out_ref[i] = short_row(i + n_steps)


def _mlp_rows(x_ref, out_ref, acc_ref, gpre_ref, gpost_ref, wup_ref, wdown_ref):
    for r0 in range(0, x_ref.shape[0], FFN_ROWS):
        rows = slice(r0, r0 + FFN_ROWS)
        x = x_ref[rows, :]
        f = _rms_norm(x, gpre_ref[...]).astype(BF16)
        for c in range(D_FF // FF_CHUNK):
            cols = slice(c * FF_CHUNK, (c + 1) * FF_CHUNK)
            u = jnp.maximum(_dot(f, wup_ref[:, cols]), 0.0)
            part = _dot((u * u).astype(BF16), wdown_ref[cols, :])
            if c == 0:
                acc_ref[rows, :] = part
            else:
                acc_ref[rows, :] += part
        out_ref[rows, :] = x + _rms_norm(acc_ref[rows, :], gpost_ref[...])


def _ffn_kernel(layer, n_cast, n_layers, n_kinds, x_ref, xs_ref, gpre_ref, gpost_ref, wup_ref,
                wdown_ref, *rest):
    cast_in, rest = rest[:n_cast], rest[n_cast:]
    per_layer, rest = rest[:n_layers * n_kinds], rest[n_layers * n_kinds:]
    (out_ref, outs_ref), rest = rest[:2], rest[2:]
    cast_out, rest = rest[:n_cast], rest[n_cast:]
    stacked, rest = rest[:n_kinds], rest[n_kinds:]
    acc_ref, sems = rest[0], rest[1:]
    gpre_ref, gpost_ref = _layer_views(layer, gpre_ref, gpost_ref)
    i = pl.program_id(0)
    last = pl.num_programs(0) - 1
    stack_copies = [pltpu.make_async_copy(src, stacked[idx % n_kinds].at[idx // n_kinds],
                                          sems[0].at[idx % n_kinds])
                    for idx, src in enumerate(per_layer)]

    @pl.when(i == 0)
    def _():
        for c in stack_copies:
            c.start()

    @pl.when(i < last)
    def _():
        for src, dst in zip(cast_in, cast_out):
            dst[...] = src[...].astype(BF16)
        _mlp_rows(x_ref, out_ref, acc_ref, gpre_ref, gpost_ref, wup_ref, wdown_ref)

    @pl.when(i == last)
    def _():
        _mlp_rows(xs_ref, outs_ref, acc_ref, gpre_ref, gpost_ref, wup_ref, wdown_ref)
        for c in stack_copies:
            c.wait()


def _const_spec(shape):
    nd = len(shape)
    return pl.BlockSpec(shape, lambda *_: (0,) * nd, pipeline_mode=pl.Buffered(1))


def _whole_spec(shape):
    nd = len(shape)
    return pl.BlockSpec(shape, lambda *_: (0,) * nd)


def _any_spec():
    return pl.BlockSpec(memory_space=pl.ANY)


def _compiler_params(semantics):
    return pltpu.CompilerParams(dimension_semantics=semantics, vmem_limit_bytes=VMEM_LIMIT_BYTES)


def _prompt_mixer(x, p, w_in, w_out, w_ffn_up, w_ffn_down, layer):
    b, s, d = x.shape
    tile = PROMPT_TILE
    consts = (w_in, w_out, POOL_INV_FIRST, POOL_INV_REST,
              p["gpre"], p["gpost"], p["wpool"], p["pscale"], p["conv_w"], p["conv_b"], p["cln_g"],
              p["cln_b"], p["sln_g"], p["sln_b"], p["sgu_w"], p["sgu_bias"], p["short_w"])
    gw = GROUP_WIDTH
    tiles_per_seq = s // tile
    steps = b * tiles_per_seq
    d_ff = w_ffn_up.shape[2]
    ff_slice = d_ff // steps
    assert ff_slice * steps == d_ff and ff_slice % LANES == 0
    step = lambda bi, j: bi * tiles_per_seq + j
    state_spec = lambda n: pl.BlockSpec((None, n, gw), lambda bi, j: (bi, 0, 0))
    return pl.pallas_call(
        functools.partial(_prompt_mixer_kernel, layer),
        grid=(b, tiles_per_seq),
        in_specs=[pl.BlockSpec((None, tile, d), lambda bi, j: (bi, j, 0))]
        + [_const_spec(c.shape) for c in consts]
        + [pl.BlockSpec((None, d, ff_slice), lambda bi, j: (layer, 0, step(bi, j))),
           pl.BlockSpec((None, ff_slice, d), lambda bi, j: (layer, step(bi, j), 0))],
        out_specs=[pl.BlockSpec((None, tile, d), lambda bi, j: (bi, j, 0)),
                   state_spec(POOL_BUF), state_spec(CONV_WIDTH - 1), state_spec(SHORT_WIDTH - 1),
                   pl.BlockSpec((d, ff_slice), lambda bi, j: (0, step(bi, j))),
                   pl.BlockSpec((ff_slice, d), lambda bi, j: (step(bi, j), 0))],
        out_shape=[jax.ShapeDtypeStruct((b, s, d), F32),
                   jax.ShapeDtypeStruct((b, POOL_BUF, gw), F32),
                   jax.ShapeDtypeStruct((b, CONV_WIDTH - 1, gw), F32),
                   jax.ShapeDtypeStruct((b, SHORT_WIDTH - 1, gw), F32),
                   jax.ShapeDtypeStruct((d, d_ff), BF16),
                   jax.ShapeDtypeStruct((d_ff, d), BF16)],
        scratch_shapes=[pltpu.VMEM((tile, IN_WIDTH), F32),
                        pltpu.VMEM((tile, d), BF16),
                        pltpu.VMEM((tile, gw), F32),
                        pltpu.VMEM((CHUNK, N_SUB * CHUNK), BF16),
                        pltpu.VMEM((N_SLABS, POOL_HIST + tile, LANES), F32),
                        pltpu.VMEM((N_SLABS, CONV_HIST + tile, LANES), F32),
                        pltpu.VMEM((N_SLABS, SHORT_HIST + tile, LANES), F32)],
        compiler_params=_compiler_params(("arbitrary", "arbitrary")),
        name="prompt_mixer",
    )(x, *consts, w_ffn_up, w_ffn_down)


def _sample_mixer(x, pool_st, conv_st, short_st, p, w_in, w_out, layer):
    n_steps, n_seq, d = x.shape
    group = SAMPLE_SEQS
    assert n_seq % group == 0 and group % SUBLANES == 0
    gw = GROUP_WIDTH
    states = (pool_st, conv_st, short_st)
    consts = (w_in, w_out, p["gpre"], p["gpost"], p["wpool"], p["pscale"], p["conv_w"], p["conv_b"],
              p["cln_g"], p["cln_b"], p["sln_g"], p["sln_b"], p["sgu_w4"], p["sgu_bias"], p["short_w"])
    group_spec = lambda a: pl.BlockSpec((a.shape[0], group, a.shape[2]), lambda i: (0, i, 0))
    state_spec = lambda st: pl.BlockSpec((None, st.shape[1], group, gw), lambda i: (layer, 0, i, 0))
    out_shape = ([jax.ShapeDtypeStruct((n_steps, n_seq, d), F32)]
                 + [jax.ShapeDtypeStruct(st.shape[1:], F32) for st in states]
                 + [jax.ShapeDtypeStruct((n_steps, n_seq, gw), F32)])
    rows = n_steps * group
    return pl.pallas_call(
        functools.partial(_sample_mixer_kernel, layer),
        grid=(n_seq // group,),
        in_specs=[group_spec(x)] + [state_spec(st) for st in states]
        + [_const_spec(c.shape) for c in consts],
        out_specs=[group_spec(o) for o in out_shape],
        out_shape=out_shape,
        scratch_shapes=[pltpu.VMEM((rows, IN_WIDTH), F32), pltpu.VMEM((rows, d), BF16)],
        compiler_params=_compiler_params(("arbitrary",)),
        name="sample_mixer",
    )(x, *states, *consts)


def _ffn(x, xs, p, w_up, w_down, layer, cast=(), to_stack=()):
    n, d = x.shape
    ns = xs.shape[0]
    tile = FFN_TILE
    steps = n // tile
    assert steps * tile == n and ns <= tile and ns % FFN_ROWS == 0
    consts = (p["fpre"], p["fpost"], w_up, w_down)
    prompt_tile = lambda i: (jnp.minimum(i, steps - 1), 0)
    cast_in_specs, cast_out_specs, cast_shapes = [], [], []
    for w, w_layer in cast:
        rows = w.shape[1] // steps
        assert rows * steps == w.shape[1] and rows % (2 * SUBLANES) == 0
        cast_in_specs.append(pl.BlockSpec(
            (None, rows, w.shape[2]), lambda i, w_layer=w_layer: (w_layer,) + prompt_tile(i)))
        cast_out_specs.append(pl.BlockSpec((rows, w.shape[2]), prompt_tile))
        cast_shapes.append(jax.ShapeDtypeStruct(w.shape[1:], BF16))
    n_layers = len(to_stack)
    kinds = to_stack[0] if to_stack else ()
    assert all([a.shape for a in arrays] == [a.shape for a in kinds] for arrays in to_stack)
    per_layer = [a for arrays in to_stack for a in arrays]
    stacked_shapes = [jax.ShapeDtypeStruct((n_layers,) + a.shape, a.dtype) for a in kinds]
    return pl.pallas_call(
        functools.partial(_ffn_kernel, layer, len(cast), n_layers, len(kinds)),
        grid=(steps + 1,),
        in_specs=[pl.BlockSpec((tile, d), prompt_tile), _whole_spec(xs.shape)]
        + [_const_spec(c.shape) for c in consts] + cast_in_specs + [_any_spec() for _ in per_layer],
        out_specs=[pl.BlockSpec((tile, d), prompt_tile), _whole_spec(xs.shape)] + cast_out_specs
        + [_any_spec() for _ in kinds],
        out_shape=[jax.ShapeDtypeStruct((n, d), F32), jax.ShapeDtypeStruct((ns, d), F32)] + cast_shapes
        + stacked_shapes,
        scratch_shapes=[pltpu.VMEM((tile, d), F32)]
        + ([pltpu.SemaphoreType.DMA((len(kinds),))] if kinds else []),
        compiler_params=_compiler_params(("arbitrary",)),
        name="ffn",
    )(x, xs, *consts, *[w for w, _ in cast], *per_layer)


def _pool_inverse_counts():
    window = np.repeat(np.asarray(POOL_WINDOWS, np.float32), SUB_DIM)[None, :]
    first = 1.0 / np.minimum(window, np.arange(1, ROW_BLOCK + 1, dtype=np.float32)[:, None])
    return first.astype(np.float32), (1.0 / window).astype(np.float32)


POOL_INV_FIRST, POOL_INV_REST = _pool_inverse_counts()


def _stacked_params(n_steps, norm_mix_pre, norm_mix_post, norm_ffn_pre, norm_ffn_post, w_pool,
                    pool_scale, conv_w, conv_b, conv_ln_g, conv_ln_b, sgu_ln_g, sgu_ln_b, sgu_w,
                    sgu_b, short_w):
    depth = w_pool.shape[0]
    gw = GROUP_WIDTH
    head_of_lane = np.arange(gw) // SUB_DIM
    same_head = head_of_lane[:, None] == head_of_lane[None, :]
    wp = jnp.tile(w_pool.reshape(depth, gw, SUB_DIM), (1, 1, N_SUB))
    wpool = jnp.where(same_head[None], wp, 0.0).astype(BF16)
    bias = jnp.repeat(jnp.swapaxes(sgu_b, 1, 2), SUB_DIM, axis=2)
    causal = np.tril(np.ones((n_steps, n_steps), dtype=bool))
    w4 = jnp.where(causal[None, None], sgu_w[:, :, :n_steps, :n_steps], 0.0)
    w4 = jnp.repeat(jnp.transpose(w4, (0, 2, 3, 1)), SUB_DIM, axis=3)
    w4 = w4.reshape(depth, n_steps * n_steps, gw)
    return dict(gpre=norm_mix_pre, gpost=norm_mix_post, fpre=norm_ffn_pre, fpost=norm_ffn_post,
                wpool=wpool, pscale=pool_scale, conv_w=conv_w, conv_b=conv_b, cln_g=conv_ln_g,
                cln_b=conv_ln_b, sln_g=sgu_ln_g, sln_b=sgu_ln_b, sgu_w=sgu_w, sgu_bias=bias,
                sgu_w4=w4, short_w=short_w)


def kernel(x_prompt, x_sample, state_pool, state_conv, state_short, norm_mix_pre, norm_mix_post, norm_ffn_pre, norm_ffn_post, w_in, w_out, w_pool, pool_scale, conv_w, conv_b, conv_ln_g, conv_ln_b, sgu_ln_g, sgu_ln_b, sgu_w, sgu_b, short_w, w_ffn_up, w_ffn_down):
    depth = w_in.shape[0]
    bp, seq, d = x_prompt.shape
    n_seq, n_steps, _ = x_sample.shape
    assert seq % PROMPT_TILE == 0 and PROMPT_TILE % MATMUL_ROWS == 0 and MATMUL_ROWS % ROW_BLOCK == 0
    assert ROW_BLOCK >= max(POOL_WINDOWS) and n_steps <= CHUNK and PAST_LEN % CHUNK == 0

    p = _stacked_params(n_steps, norm_mix_pre, norm_mix_post, norm_ffn_pre, norm_ffn_post, w_pool,
                        pool_scale, conv_w, conv_b, conv_ln_g, conv_ln_b, sgu_ln_g, sgu_ln_b, sgu_w,
                        sgu_b, short_w)
    yp = x_prompt
    ys = jnp.transpose(x_sample, (1, 0, 2))
    hist_major = lambda a: jnp.transpose(a, (0, 2, 1, 3))
    pool_in, conv_in, short_in = hist_major(state_pool), hist_major(state_conv), hist_major(state_short)
    prompt_states = [[] for _ in range(3)]
    sample_out = []
    w_in_b, w_out_b = w_in[0].astype(BF16), w_out[0].astype(BF16)
    for l in range(depth):
        yp, pool_p, conv_p, short_p, w_up_b, w_down_b = _prompt_mixer(
            yp, p, w_in_b, w_out_b, w_ffn_up, w_ffn_down, l)
        ys, *layer_out = _sample_mixer(ys, pool_in, conv_in, short_in, p, w_in_b, w_out_b, l)
        sample_out.append(layer_out)
        for lst, val in zip(prompt_states, (pool_p, conv_p, short_p)):
            lst.append(val)
        last = l + 1 == depth
        cast = () if last else ((w_in, l + 1), (w_out, l + 1))
        yp, ys, *extra = _ffn(yp.reshape(bp * seq, d), ys.reshape(n_steps * n_seq, d), p, w_up_b,
                              w_down_b, l, cast, sample_out if last else ())
        yp, ys = yp.reshape(bp, seq, d), ys.reshape(n_steps, n_seq, d)
        if not last:
            w_in_b, w_out_b = extra

    ys = jnp.transpose(ys, (1, 0, 2))
    pool_p, conv_p, short_p = (jnp.stack(o) for o in prompt_states)
    pool_s, conv_s, short_s, v_s = (hist_major(o) for o in extra)
    return (yp, ys, pool_p, pool_s, conv_p, conv_s, short_p, short_s, v_s)
```

```python
import functools

import jax
import jax.numpy as jnp
import numpy as np
from jax import lax
from jax.experimental import pallas as pl
from jax.experimental.pallas import tpu as pltpu

D_MODEL = 1024
GROUP_WIDTH = 256
N_SUB = 4
SUB_DIM = 64
POOL_WINDOWS = (2, 4, 8, 16)
POOL_BUF = 15
CONV_WIDTH = 31
SHORT_WIDTH = 3
CHUNK = 128
D_FF = 4096
EPS = 1e-6
PAST_LEN = 16384
IN_WIDTH = 8 * GROUP_WIDTH

SUBLANES = 8
LANES = 128
N_SLABS = GROUP_WIDTH // LANES
POOL_HIST = 16
CONV_HIST = 32
SHORT_HIST = 8

ROW_BLOCK = CHUNK
MATMUL_ROWS = 512
PROMPT_TILE = 1024
SAMPLE_SEQS = 64
FFN_TILE = 1024
VMEM_LIMIT_BYTES = 56 * 1024 * 1024

F32 = jnp.float32
BF16 = jnp.bfloat16


def _rms_norm(x, g):
    ms = jnp.mean(x * x, axis=-1, keepdims=True)
    return x * lax.rsqrt(ms + EPS) * g


def _dot(a, b):
    return jnp.dot(a, b, preferred_element_type=F32)


def _head_mean(x, low_head):
    s_low = jnp.sum(jnp.where(low_head, x, 0.0), axis=-1, keepdims=True)
    s_high = jnp.sum(jnp.where(low_head, 0.0, x), axis=-1, keepdims=True)
    return jnp.where(low_head, s_low, s_high) * (1.0 / SUB_DIM)


def _head_layer_norm(x, g, b):
    low_head = lax.broadcasted_iota(jnp.int32, (x.shape[0], LANES), 1) < SUB_DIM
    out = []
    for s in range(N_SLABS):
        lanes = slice(s * LANES, (s + 1) * LANES)
        xs = x[:, lanes]
        xc = xs - _head_mean(xs, low_head)
        var = _head_mean(xc * xc, low_head)
        out.append(xc * lax.rsqrt(var + EPS) * g[:, lanes] + b[:, lanes])
    return jnp.concatenate(out, axis=1)


def _lane_group(shape):
    return jnp.right_shift(lax.broadcasted_iota(jnp.int32, shape, 1), SUB_DIM.bit_length() - 1)


def _pool_window(shape):
    grp = _lane_group(shape)
    w = jnp.full(shape, POOL_WINDOWS[0], jnp.int32)
    for gi in range(1, N_SUB):
        w = jnp.where(grp == gi, POOL_WINDOWS[gi], w)
    return grp, w


def _gating_rhs(vn):
    grp = _lane_group(vn.shape)
    return jnp.concatenate([jnp.where(grp == h, vn, 0.0).astype(BF16) for h in range(N_SUB)], axis=0)


def _cols(group, slab):
    lo = group * GROUP_WIDTH + slab * LANES
    return slice(lo, lo + LANES)


def _trailing_sum(ext_ref, slab, start, n_rows, width):
    acc = ext_ref[slab, start:start + n_rows, :]
    for k in range(1, width):
        acc = acc + ext_ref[slab, start - k:start - k + n_rows, :]
    return acc


def _causal_taps(ext_ref, slab, w_ref, hist, n_taps, r):
    first = hist + r - (n_taps - 1)
    lanes = slice(slab * LANES, (slab + 1) * LANES)
    acc = None
    for k in range(n_taps):
        term = ext_ref[slab, first + k:first + k + ROW_BLOCK, :] * w_ref[k:k + 1, lanes]
        acc = term if acc is None else acc + term
    return acc


def _gate_rows(r, z_ref, y_ref, vn_ref, swcat_ref, sbias_ref):
    rows = slice(r, r + ROW_BLOCK)
    gw = GROUP_WIDTH
    sg = _dot(swcat_ref[...], _gating_rhs(vn_ref[rows, :])) + sbias_ref[...]
    y_ref[rows, 2 * gw:3 * gw] = (z_ref[rows, 3 * gw:4 * gw] * sg).astype(BF16)


def _mix_rows(r, seq_start, z_ref, y_ref, vn_ref, pool_ext, conv_ext, short_ext, wpool_ref,
              pscale_ref, pinv_first_ref, pinv_rest_ref, convw_ref, convb_ref, clng_ref, clnb_ref,
              slng_ref, slnb_ref, shortw_ref):
    rows = slice(r, r + ROW_BLOCK)
    gw = GROUP_WIDTH
    low_head = lax.broadcasted_iota(jnp.int32, (ROW_BLOCK, LANES), 1) < SUB_DIM

    d = []
    for s in range(N_SLABS):
        lanes = slice(s * LANES, (s + 1) * LANES)
        a = z_ref[rows, _cols(0, s)]
        base = POOL_HIST + r
        pool_ext[s, base:base + ROW_BLOCK, :] = a
        small, large = POOL_WINDOWS[2 * s], POOL_WINDOWS[2 * s + 1]
        assert large == 2 * small
        if small % SUBLANES == 0:
            run = _trailing_sum(pool_ext, s, base - small, ROW_BLOCK + small, small)
            s_small = run[small:]
            s_large = s_small + run[:ROW_BLOCK]
        else:
            s_small = _trailing_sum(pool_ext, s, base, ROW_BLOCK, small)
            s_large = s_small + _trailing_sum(pool_ext, s, base - small, ROW_BLOCK, small)
        win = jnp.where(low_head, s_small, s_large)
        inv = pinv_rest_ref[:, lanes]
        if seq_start is not None:
            inv = jnp.where(seq_start, pinv_first_ref[:, lanes], inv)
        d.append(win * inv - a)
    d = jnp.concatenate(d, axis=1).astype(BF16)
    y_ref[rows, 0:gw] = (_dot(d, wpool_ref[...]) * pscale_ref[...]).astype(BF16)

    c = []
    for s in range(N_SLABS):
        g = z_ref[rows, _cols(1, s)] * jax.nn.sigmoid(z_ref[rows, _cols(2, s)])
        conv_ext[s, CONV_HIST + r:CONV_HIST + r + ROW_BLOCK, :] = g
        c.append(_causal_taps(conv_ext, s, convw_ref, CONV_HIST, CONV_WIDTH, r))
    c = jnp.concatenate(c, axis=1) + convb_ref[...]
    c = _head_layer_norm(c, clng_ref[...], clnb_ref[...])
    y_ref[rows, gw:2 * gw] = (c * jax.nn.sigmoid(c)).astype(BF16)

    vn_ref[rows, :] = _head_layer_norm(z_ref[rows, 4 * gw:5 * gw], slng_ref[...], slnb_ref[...])

    for s in range(N_SLABS):
        ch = z_ref[rows, _cols(6, s)] * z_ref[rows, _cols(7, s)]
        short_ext[s, SHORT_HIST + r:SHORT_HIST + r + ROW_BLOCK, :] = ch
        sc = _causal_taps(short_ext, s, shortw_ref, SHORT_HIST, SHORT_WIDTH, r)
        y_ref[rows, _cols(3, s)] = (z_ref[rows, _cols(5, s)] * sc).astype(BF16)


def _layer_views(layer, *refs):
    return [r.at[pl.ds(layer, 1)] if len(r.shape) == 2 else r.at[layer] for r in refs]


def _prompt_mixer_kernel(layer, x_ref, win_ref, wout_ref, pinv_first_ref, pinv_rest_ref,
                         gpre_ref, gpost_ref, wpool_ref, pscale_ref, convw_ref, convb_ref, clng_ref,
                         clnb_ref, slng_ref, slnb_ref, sguw_ref, sbias_ref, shortw_ref,
                         wup_f32_ref, wdown_f32_ref,
                         out_ref, pool_out_ref, conv_out_ref, short_out_ref, wup_ref, wdown_ref,
                         z_ref, y_ref, vn_ref, swcat_ref, pool_ext, conv_ext, short_ext):
    (gpre_ref, gpost_ref, wpool_ref, pscale_ref, convw_ref, convb_ref, clng_ref, clnb_ref, slng_ref,
     slnb_ref, sguw_ref, sbias_ref, shortw_ref) = _layer_views(
         layer, gpre_ref, gpost_ref, wpool_ref, pscale_ref, convw_ref, convb_ref, clng_ref, clnb_ref,
         slng_ref, slnb_ref, sguw_ref, sbias_ref, shortw_ref)
    j = pl.program_id(1)
    tile = x_ref.shape[0]
    wup_ref[...] = wup_f32_ref[...].astype(BF16)
    wdown_ref[...] = wdown_f32_ref[...].astype(BF16)
    causal = (lax.broadcasted_iota(jnp.int32, (CHUNK, CHUNK), 0)
              >= lax.broadcasted_iota(jnp.int32, (CHUNK, CHUNK), 1))
    for hd in range(N_SUB):
        swcat_ref[:, hd * CHUNK:(hd + 1) * CHUNK] = jnp.where(causal, sguw_ref[hd], 0.0).astype(BF16)

    @pl.when(j == 0)
    def _():
        pool_ext[:, 0:POOL_HIST, :] = jnp.zeros((N_SLABS, POOL_HIST, LANES), F32)
        conv_ext[:, 0:CONV_HIST, :] = jnp.zeros((N_SLABS, CONV_HIST, LANES), F32)
        short_ext[:, 0:SHORT_HIST, :] = jnp.zeros((N_SLABS, SHORT_HIST, LANES), F32)

    def project_out(r0):
        rows = slice(r0, r0 + MATMUL_ROWS)
        o = _dot(y_ref[rows, :], wout_ref[...])
        out_ref[rows, :] = x_ref[rows, :] + _rms_norm(o, gpost_ref[...])

    for r0 in range(0, tile, MATMUL_ROWS):
        rows = slice(r0, r0 + MATMUL_ROWS)
        h = _rms_norm(x_ref[rows, :], gpre_ref[...]).astype(BF16)
        z_ref[rows, :] = _dot(h, win_ref[...])
        for r in range(r0, r0 + MATMUL_ROWS, ROW_BLOCK):
            _mix_rows(r, (j == 0) if r == 0 else None, z_ref, y_ref, vn_ref, pool_ext, conv_ext,
                      short_ext, wpool_ref, pscale_ref, pinv_first_ref, pinv_rest_ref, convw_ref,
                      convb_ref, clng_ref, clnb_ref, slng_ref, slnb_ref, shortw_ref)
            if r > 0:
                _gate_rows(r - ROW_BLOCK, z_ref, y_ref, vn_ref, swcat_ref, sbias_ref)
            if r == r0 and r0 > 0:
                project_out(r0 - MATMUL_ROWS)
    _gate_rows(tile - ROW_BLOCK, z_ref, y_ref, vn_ref, swcat_ref, sbias_ref)
    project_out(tile - MATMUL_ROWS)

    pool_ext[:, 0:POOL_HIST, :] = pool_ext[:, tile:tile + POOL_HIST, :]
    conv_ext[:, 0:CONV_HIST, :] = conv_ext[:, tile:tile + CONV_HIST, :]
    short_ext[:, 0:SHORT_HIST, :] = short_ext[:, tile:tile + SHORT_HIST, :]

    @pl.when(j == pl.num_programs(1) - 1)
    def _():
        for s in range(N_SLABS):
            lanes = slice(s * LANES, (s + 1) * LANES)
            pool_out_ref[:, lanes] = pool_ext[s, POOL_HIST - POOL_BUF:POOL_HIST, :]
            conv_out_ref[:, lanes] = conv_ext[s, CONV_HIST - (CONV_WIDTH - 1):CONV_HIST, :]
            short_out_ref[:, lanes] = short_ext[s, SHORT_HIST - (SHORT_WIDTH - 1):SHORT_HIST, :]


def _sample_mixer_kernel(layer, x_ref, pool_in_ref, conv_in_ref, short_in_ref, win_ref, wout_ref,
                         gpre_ref, gpost_ref, wpool_ref, pscale_ref, convw_ref, convb_ref, clng_ref,
                         clnb_ref, slng_ref, slnb_ref, sgw_ref, sgb_ref, shortw_ref,
                         out_ref, pool_out_ref, conv_out_ref, short_out_ref, v_out_ref,
                         z_ref, y_ref):
    (gpre_ref, gpost_ref, wpool_ref, pscale_ref, convw_ref, convb_ref, clng_ref, clnb_ref, slng_ref,
     slnb_ref, sgw_ref, sgb_ref, shortw_ref) = _layer_views(
         layer, gpre_ref, gpost_ref, wpool_ref, pscale_ref, convw_ref, convb_ref, clng_ref, clnb_ref,
         slng_ref, slnb_ref, sgw_ref, sgb_ref, shortw_ref)
    n_steps, n_seq = v_out_ref.shape[0], v_out_ref.shape[1]
    gw = GROUP_WIDTH
    x = x_ref[...].reshape(n_steps * n_seq, x_ref.shape[2])
    h = _rms_norm(x, gpre_ref[...]).astype(BF16)
    z_ref[...] = _dot(h, win_ref[...])
    grp, w = _pool_window((n_seq, gw))
    cnt = jnp.minimum(w, PAST_LEN + 1).astype(F32)

    def slab(t):
        return slice(t * n_seq, (t + 1) * n_seq)

    a_new = [z_ref[slab(t), 0:gw] for t in range(n_steps)]
    g_new = [z_ref[slab(t), gw:2 * gw] * jax.nn.sigmoid(z_ref[slab(t), 2 * gw:3 * gw])
             for t in range(n_steps)]
    ch_new = [z_ref[slab(t), 6 * gw:7 * gw] * z_ref[slab(t), 7 * gw:8 * gw]
              for t in range(n_steps)]

    def pool_row(i):
        return pool_in_ref[i] if i < POOL_BUF else a_new[i - POOL_BUF]

    def conv_row(i):
        return conv_in_ref[i] if i < CONV_WIDTH - 1 else g_new[i - (CONV_WIDTH - 1)]

    def short_row(i):
        return short_in_ref[i] if i < SHORT_WIDTH - 1 else ch_new[i - (SHORT_WIDTH - 1)]

    vn = []
    for t in range(n_steps):
        rows = slab(t)
        end = POOL_BUF + t
        acc = pool_row(end)
        sums = []
        for k in range(1, max(POOL_WINDOWS)):
            acc = acc + pool_row(end - k)
            if k + 1 in POOL_WINDOWS:
                sums.append(acc)
        win = sums[0]
        for gi in range(1, N_SUB):
            win = jnp.where(grp == gi, sums[gi], win)
        d = win / cnt - a_new[t]
        y_ref[rows, 0:gw] = (_dot(d.astype(BF16), wpool_ref[...]) * pscale_ref[...]).astype(BF16)

        c = None
        for k in range(CONV_WIDTH):
            term = conv_row(t + k) * convw_ref[k:k + 1, :]
            c = term if c is None else c + term
        c = _head_layer_norm(c + convb_ref[...], clng_ref[...], clnb_ref[...])
        y_ref[rows, gw:2 * gw] = (c * jax.nn.sigmoid(c)).astype(BF16)

        vn.append(_head_layer_norm(z_ref[rows, 4 * gw:5 * gw], slng_ref[...], slnb_ref[...]))
        v_out_ref[t] = vn[t]
        s = sgb_ref[t:t + 1, :]
        for u in range(t + 1):
            s = s + sgw_ref[t * n_steps + u:t * n_steps + u + 1, :] * vn[u]
        y_ref[rows, 2 * gw:3 * gw] = (z_ref[rows, 3 * gw:4 * gw] * s).astype(BF16)

        sc = None
        for k in range(SHORT_WIDTH):
            term = short_row(t + k) * shortw_ref[k:k + 1, :]
            sc = term if sc is None else sc + term
        y_ref[rows, 3 * gw:4 * gw] = (z_ref[rows, 5 * gw:6 * gw] * sc).astype(BF16)

    o = _dot(y_ref[...], wout_ref[...])
    out_ref[...] = (x + _rms_norm(o, gpost_ref[...])).reshape(out_ref.shape)

    for i in range(POOL_BUF):
        pool_out_ref[i] = pool_row(i + n_steps)
    for i in range(CONV_WIDTH - 1):
        conv_out_ref[i] = conv_row(i + n_steps)
    for i in range(SHORT_WIDTH - 1):
        short_out_ref[i] = short_row(i + n_steps)


def _mlp(x_ref, out_ref, gpre_ref, gpost_ref, wup_ref, wdown_ref):
    x = x_ref[...]
    f = _rms_norm(x, gpre_ref[...]).astype(BF16)
    u = jnp.maximum(_dot(f, wup_ref[...]), 0.0)
    o = _dot((u * u).astype(BF16), wdown_ref[...])
    out_ref[...] = x + _rms_norm(o, gpost_ref[...])


def _ffn_kernel(layer, n_cast, x_ref, xs_ref, gpre_ref, gpost_ref, wup_ref, wdown_ref, *rest):
    cast_in, out_ref, outs_ref = rest[:n_cast], rest[n_cast], rest[n_cast + 1]
    cast_out = rest[n_cast + 2:]
    gpre_ref, gpost_ref = _layer_views(layer, gpre_ref, gpost_ref)
    i = pl.program_id(0)
    last = pl.num_programs(0) - 1

    @pl.when(i < last)
    def _():
        for src, dst in zip(cast_in, cast_out):
            dst[...] = src[...].astype(BF16)
        _mlp(x_ref, out_ref, gpre_ref, gpost_ref, wup_ref, wdown_ref)

    @pl.when(i == last)
    def _():
        _mlp(xs_ref, outs_ref, gpre_ref, gpost_ref, wup_ref, wdown_ref)


def _const_spec(shape):
    nd = len(shape)
    return pl.BlockSpec(shape, lambda *_: (0,) * nd, pipeline_mode=pl.Buffered(1))


def _whole_spec(shape):
    nd = len(shape)
    return pl.BlockSpec(shape, lambda *_: (0,) * nd)


def _compiler_params(semantics):
    return pltpu.CompilerParams(dimension_semantics=semantics, vmem_limit_bytes=VMEM_LIMIT_BYTES)


def _prompt_mixer(x, p, w_in, w_out, w_ffn_up, w_ffn_down, layer):
    b, s, d = x.shape
    tile = PROMPT_TILE
    consts = (w_in, w_out, POOL_INV_FIRST, POOL_INV_REST,
              p["gpre"], p["gpost"], p["wpool"], p["pscale"], p["conv_w"], p["conv_b"], p["cln_g"],
              p["cln_b"], p["sln_g"], p["sln_b"], p["sgu_w"], p["sgu_bias"], p["short_w"])
    gw = GROUP_WIDTH
    tiles_per_seq = s // tile
    steps = b * tiles_per_seq
    d_ff = w_ffn_up.shape[2]
    ff_slice = d_ff // steps
    assert ff_slice * steps == d_ff and ff_slice % LANES == 0
    step = lambda bi, j: bi * tiles_per_seq + j
    state_spec = lambda n: pl.BlockSpec((None, n, gw), lambda bi, j: (bi, 0, 0))
    return pl.pallas_call(
        functools.partial(_prompt_mixer_kernel, layer),
        grid=(b, tiles_per_seq),
        in_specs=[pl.BlockSpec((None, tile, d), lambda bi, j: (bi, j, 0))]
        + [_const_spec(c.shape) for c in consts]
        + [pl.BlockSpec((None, d, ff_slice), lambda bi, j: (layer, 0, step(bi, j))),
           pl.BlockSpec((None, ff_slice, d), lambda bi, j: (layer, step(bi, j), 0))],
        out_specs=[pl.BlockSpec((None, tile, d), lambda bi, j: (bi, j, 0)),
                   state_spec(POOL_BUF), state_spec(CONV_WIDTH - 1), state_spec(SHORT_WIDTH - 1),
                   pl.BlockSpec((d, ff_slice), lambda bi, j: (0, step(bi, j))),
                   pl.BlockSpec((ff_slice, d), lambda bi, j: (step(bi, j), 0))],
        out_shape=[jax.ShapeDtypeStruct((b, s, d), F32),
                   jax.ShapeDtypeStruct((b, POOL_BUF, gw), F32),
                   jax.ShapeDtypeStruct((b, CONV_WIDTH - 1, gw), F32),
                   jax.ShapeDtypeStruct((b, SHORT_WIDTH - 1, gw), F32),
                   jax.ShapeDtypeStruct((d, d_ff), BF16),
                   jax.ShapeDtypeStruct((d_ff, d), BF16)],
        scratch_shapes=[pltpu.VMEM((tile, IN_WIDTH), F32),
                        pltpu.VMEM((tile, d), BF16),
                        pltpu.VMEM((tile, gw), F32),
                        pltpu.VMEM((CHUNK, N_SUB * CHUNK), BF16),
                        pltpu.VMEM((N_SLABS, POOL_HIST + tile, LANES), F32),
                        pltpu.VMEM((N_SLABS, CONV_HIST + tile, LANES), F32),
                        pltpu.VMEM((N_SLABS, SHORT_HIST + tile, LANES), F32)],
        compiler_params=_compiler_params(("arbitrary", "arbitrary")),
        name="prompt_mixer",
    )(x, *consts, w_ffn_up, w_ffn_down)


def _sample_mixer(x, pool_st, conv_st, short_st, p, w_in, w_out, layer):
    n_steps, n_seq, d = x.shape
    group = SAMPLE_SEQS
    assert n_seq % group == 0 and group % SUBLANES == 0
    gw = GROUP_WIDTH
    states = (pool_st, conv_st, short_st)
    consts = (w_in, w_out, p["gpre"], p["gpost"], p["wpool"], p["pscale"], p["conv_w"], p["conv_b"],
              p["cln_g"], p["cln_b"], p["sln_g"], p["sln_b"], p["sgu_w4"], p["sgu_bias"], p["short_w"])
    group_spec = lambda a: pl.BlockSpec((a.shape[0], group, a.shape[2]), lambda i: (0, i, 0))
    state_spec = lambda st: pl.BlockSpec((None, st.shape[1], group, gw), lambda i: (layer, 0, i, 0))
    out_shape = ([jax.ShapeDtypeStruct((n_steps, n_seq, d), F32)]
                 + [jax.ShapeDtypeStruct(st.shape[1:], F32) for st in states]
                 + [jax.ShapeDtypeStruct((n_steps, n_seq, gw), F32)])
    rows = n_steps * group
    return pl.pallas_call(
        functools.partial(_sample_mixer_kernel, layer),
        grid=(n_seq // group,),
        in_specs=[group_spec(x)] + [state_spec(st) for st in states]
        + [_const_spec(c.shape) for c in consts],
        out_specs=[group_spec(o) for o in out_shape],
        out_shape=out_shape,
        scratch_shapes=[pltpu.VMEM((rows, IN_WIDTH), F32), pltpu.VMEM((rows, d), BF16)],
        compiler_params=_compiler_params(("arbitrary",)),
        name="sample_mixer",
    )(x, *states, *consts)


def _ffn(x, xs, p, w_up, w_down, layer, cast=()):
    n, d = x.shape
    ns = xs.shape[0]
    tile = FFN_TILE
    steps = n // tile
    assert steps * tile == n and ns <= tile
    consts = (p["fpre"], p["fpost"], w_up, w_down)
    prompt_tile = lambda i: (jnp.minimum(i, steps - 1), 0)
    cast_in_specs, cast_out_specs, cast_shapes = [], [], []
    for w, w_layer in cast:
        rows = w.shape[1] // steps
        assert rows * steps == w.shape[1] and rows % (2 * SUBLANES) == 0
        cast_in_specs.append(pl.BlockSpec(
            (None, rows, w.shape[2]), lambda i, w_layer=w_layer: (w_layer,) + prompt_tile(i)))
        cast_out_specs.append(pl.BlockSpec((rows, w.shape[2]), prompt_tile))
        cast_shapes.append(jax.ShapeDtypeStruct(w.shape[1:], BF16))
    return pl.pallas_call(
        functools.partial(_ffn_kernel, layer, len(cast)),
        grid=(steps + 1,),
        in_specs=[pl.BlockSpec((tile, d), prompt_tile), _whole_spec(xs.shape)]
        + [_const_spec(c.shape) for c in consts] + cast_in_specs,
        out_specs=[pl.BlockSpec((tile, d), prompt_tile), _whole_spec(xs.shape)] + cast_out_specs,
        out_shape=[jax.ShapeDtypeStruct((n, d), F32), jax.ShapeDtypeStruct((ns, d), F32)] + cast_shapes,
        compiler_params=_compiler_params(("arbitrary",)),
        name="ffn",
    )(x, xs, *consts, *[w for w, _ in cast])


def _pool_inverse_counts():
    window = np.repeat(np.asarray(POOL_WINDOWS, np.float32), SUB_DIM)[None, :]
    first = 1.0 / np.minimum(window, np.arange(1, ROW_BLOCK + 1, dtype=np.float32)[:, None])
    return first.astype(np.float32), (1.0 / window).astype(np.float32)


POOL_INV_FIRST, POOL_INV_REST = _pool_inverse_counts()


def _stacked_params(n_steps, norm_mix_pre, norm_mix_post, norm_ffn_pre, norm_ffn_post, w_pool,
                    pool_scale, conv_w, conv_b, conv_ln_g, conv_ln_b, sgu_ln_g, sgu_ln_b, sgu_w,
                    sgu_b, short_w):
    depth = w_pool.shape[0]
    gw = GROUP_WIDTH
    head_of_lane = np.arange(gw) // SUB_DIM
    same_head = head_of_lane[:, None] == head_of_lane[None, :]
    wp = jnp.tile(w_pool.reshape(depth, gw, SUB_DIM), (1, 1, N_SUB))
    wpool = jnp.where(same_head[None], wp, 0.0).astype(BF16)
    bias = jnp.repeat(jnp.swapaxes(sgu_b, 1, 2), SUB_DIM, axis=2)
    causal = np.tril(np.ones((n_steps, n_steps), dtype=bool))
    w4 = jnp.where(causal[None, None], sgu_w[:, :, :n_steps, :n_steps], 0.0)
    w4 = jnp.repeat(jnp.transpose(w4, (0, 2, 3, 1)), SUB_DIM, axis=3)
    w4 = w4.reshape(depth, n_steps * n_steps, gw)
    return dict(gpre=norm_mix_pre, gpost=norm_mix_post, fpre=norm_ffn_pre, fpost=norm_ffn_post,
                wpool=wpool, pscale=pool_scale, conv_w=conv_w, conv_b=conv_b, cln_g=conv_ln_g,
                cln_b=conv_ln_b, sln_g=sgu_ln_g, sln_b=sgu_ln_b, sgu_w=sgu_w, sgu_bias=bias,
                sgu_w4=w4, short_w=short_w)


def kernel(x_prompt, x_sample, state_pool, state_conv, state_short, norm_mix_pre, norm_mix_post, norm_ffn_pre, norm_ffn_post, w_in, w_out, w_pool, pool_scale, conv_w, conv_b, conv_ln_g, conv_ln_b, sgu_ln_g, sgu_ln_b, sgu_w, sgu_b, short_w, w_ffn_up, w_ffn_down):
    depth = w_in.shape[0]
    bp, seq, d = x_prompt.shape
    n_seq, n_steps, _ = x_sample.shape
    assert seq % PROMPT_TILE == 0 and PROMPT_TILE % MATMUL_ROWS == 0 and MATMUL_ROWS % ROW_BLOCK == 0
    assert ROW_BLOCK >= max(POOL_WINDOWS) and n_steps <= CHUNK and PAST_LEN % CHUNK == 0

    p = _stacked_params(n_steps, norm_mix_pre, norm_mix_post, norm_ffn_pre, norm_ffn_post, w_pool,
                        pool_scale, conv_w, conv_b, conv_ln_g, conv_ln_b, sgu_ln_g, sgu_ln_b, sgu_w,
                        sgu_b, short_w)
    yp = x_prompt
    ys = jnp.transpose(x_sample, (1, 0, 2))
    hist_major = lambda a: jnp.transpose(a, (0, 2, 1, 3))
    pool_in, conv_in, short_in = hist_major(state_pool), hist_major(state_conv), hist_major(state_short)
    prompt_states = [[] for _ in range(3)]
    sample_states = [[] for _ in range(4)]
    w_in_b, w_out_b = w_in[0].astype(BF16), w_out[0].astype(BF16)
    for l in range(depth):
        yp, pool_p, conv_p, short_p, w_up_b, w_down_b = _prompt_mixer(
            yp, p, w_in_b, w_out_b, w_ffn_up, w_ffn_down, l)
        ys, pool_s, conv_s, short_s, v_s = _sample_mixer(
            ys, pool_in, conv_in, short_in, p, w_in_b, w_out_b, l)
        cast = ((w_in, l + 1), (w_out, l + 1)) if l + 1 < depth else ()
        yp, ys, *w_next = _ffn(yp.reshape(bp * seq, d), ys.reshape(n_steps * n_seq, d), p, w_up_b,
                               w_down_b, l, cast)
        yp, ys = yp.reshape(bp, seq, d), ys.reshape(n_steps, n_seq, d)
        if w_next:
            w_in_b, w_out_b = w_next

        for lst, val in zip(prompt_states, (pool_p, conv_p, short_p)):
            lst.append(val)
        for lst, val in zip(sample_states, (pool_s, conv_s, short_s, v_s)):
            lst.append(val)

    ys = jnp.transpose(ys, (1, 0, 2))
    pool_p, conv_p, short_p = (jnp.stack(o) for o in prompt_states)
    pool_s, conv_s, short_s, v_s = (hist_major(jnp.stack(o)) for o in sample_states)
    return (yp, ys, pool_p, pool_s, conv_p, conv_s, short_p, short_s, v_s)
```

```python
import functools

import jax
import jax.numpy as jnp
import numpy as np
from jax import lax
from jax.experimental import pallas as pl
from jax.experimental.pallas import tpu as pltpu

D_MODEL = 1024
GROUP_WIDTH = 256
N_SUB = 4
SUB_DIM = 64
POOL_WINDOWS = (2, 4, 8, 16)
POOL_BUF = 15
CONV_WIDTH = 31
SHORT_WIDTH = 3
CHUNK = 128
D_FF = 4096
EPS = 1e-6
PAST_LEN = 16384
IN_WIDTH = 8 * GROUP_WIDTH

SUBLANES = 8
LANES = 128
N_SLABS = GROUP_WIDTH // LANES
POOL_HIST = 16
CONV_HIST = 32
SHORT_HIST = 8

ROW_BLOCK = CHUNK
MATMUL_ROWS = 512
PROMPT_TILE = 1024
SAMPLE_SEQS = 64
FFN_TILE = 1024
VMEM_LIMIT_BYTES = 56 * 1024 * 1024

F32 = jnp.float32
BF16 = jnp.bfloat16


def _rms_norm(x, g):
    ms = jnp.mean(x * x, axis=-1, keepdims=True)
    return x * lax.rsqrt(ms + EPS) * g


def _dot(a, b):
    return jnp.dot(a, b, preferred_element_type=F32)


def _head_mean(x, low_head):
    s_low = jnp.sum(jnp.where(low_head, x, 0.0), axis=-1, keepdims=True)
    s_high = jnp.sum(jnp.where(low_head, 0.0, x), axis=-1, keepdims=True)
    return jnp.where(low_head, s_low, s_high) * (1.0 / SUB_DIM)


def _head_layer_norm(x, g, b):
    low_head = lax.broadcasted_iota(jnp.int32, (x.shape[0], LANES), 1) < SUB_DIM
    out = []
    for s in range(N_SLABS):
        lanes = slice(s * LANES, (s + 1) * LANES)
        xs = x[:, lanes]
        xc = xs - _head_mean(xs, low_head)
        var = _head_mean(xc * xc, low_head)
        out.append(xc * lax.rsqrt(var + EPS) * g[:, lanes] + b[:, lanes])
    return jnp.concatenate(out, axis=1)


def _lane_group(shape):
    return jnp.right_shift(lax.broadcasted_iota(jnp.int32, shape, 1), SUB_DIM.bit_length() - 1)


def _pool_window(shape):
    grp = _lane_group(shape)
    w = jnp.full(shape, POOL_WINDOWS[0], jnp.int32)
    for gi in range(1, N_SUB):
        w = jnp.where(grp == gi, POOL_WINDOWS[gi], w)
    return grp, w


def _gating_rhs(vn):
    grp = _lane_group(vn.shape)
    return jnp.concatenate([jnp.where(grp == h, vn, 0.0).astype(BF16) for h in range(N_SUB)], axis=0)


def _cols(group, slab):
    lo = group * GROUP_WIDTH + slab * LANES
    return slice(lo, lo + LANES)


def _trailing_sum(ext_ref, slab, start, n_rows, width):
    acc = ext_ref[slab, start:start + n_rows, :]
    for k in range(1, width):
        acc = acc + ext_ref[slab, start - k:start - k + n_rows, :]
    return acc


def _causal_taps(ext_ref, slab, w_ref, hist, n_taps, r):
    first = hist + r - (n_taps - 1)
    lanes = slice(slab * LANES, (slab + 1) * LANES)
    acc = None
    for k in range(n_taps):
        term = ext_ref[slab, first + k:first + k + ROW_BLOCK, :] * w_ref[k:k + 1, lanes]
        acc = term if acc is None else acc + term
    return acc


def _gate_rows(r, z_ref, y_ref, vn_ref, swcat_ref, sbias_ref):
    rows = slice(r, r + ROW_BLOCK)
    gw = GROUP_WIDTH
    sg = _dot(swcat_ref[...], _gating_rhs(vn_ref[rows, :])) + sbias_ref[...]
    y_ref[rows, 2 * gw:3 * gw] = (z_ref[rows, 3 * gw:4 * gw] * sg).astype(BF16)


def _mix_rows(r, seq_start, z_ref, y_ref, vn_ref, pool_ext, conv_ext, short_ext,
              pinv_first_ref, pinv_rest_ref, convw_ref, convb_ref, clng_ref, clnb_ref,
              slng_ref, slnb_ref, shortw_ref):
    rows = slice(r, r + ROW_BLOCK)
    gw = GROUP_WIDTH
    low_head = lax.broadcasted_iota(jnp.int32, (ROW_BLOCK, LANES), 1) < SUB_DIM

    d = []
    for s in range(N_SLABS):
        lanes = slice(s * LANES, (s + 1) * LANES)
        a = z_ref[rows, _cols(0, s)]
        base = POOL_HIST + r
        pool_ext[s, base:base + ROW_BLOCK, :] = a
        small, large = POOL_WINDOWS[2 * s], POOL_WINDOWS[2 * s + 1]
        assert large == 2 * small
        if small % SUBLANES == 0:
            run = _trailing_sum(pool_ext, s, base - small, ROW_BLOCK + small, small)
            s_small = run[small:]
            s_large = s_small + run[:ROW_BLOCK]
        else:
            s_small = _trailing_sum(pool_ext, s, base, ROW_BLOCK, small)
            s_large = s_small + _trailing_sum(pool_ext, s, base - small, ROW_BLOCK, small)
        win = jnp.where(low_head, s_small, s_large)
        inv = pinv_rest_ref[:, lanes]
        if seq_start is not None:
            inv = jnp.where(seq_start, pinv_first_ref[:, lanes], inv)
        d.append(win * inv - a)
    d = jnp.concatenate(d, axis=1).astype(BF16)
    y_ref[rows, 0:gw] = d

    c = []
    for s in range(N_SLABS):
        g = z_ref[rows, _cols(1, s)] * jax.nn.sigmoid(z_ref[rows, _cols(2, s)])
        conv_ext[s, CONV_HIST + r:CONV_HIST + r + ROW_BLOCK, :] = g
        c.append(_causal_taps(conv_ext, s, convw_ref, CONV_HIST, CONV_WIDTH, r))
    c = jnp.concatenate(c, axis=1) + convb_ref[...]
    c = _head_layer_norm(c, clng_ref[...], clnb_ref[...])
    y_ref[rows, gw:2 * gw] = (c * jax.nn.sigmoid(c)).astype(BF16)

    vn_ref[rows, :] = _head_layer_norm(z_ref[rows, 4 * gw:5 * gw], slng_ref[...], slnb_ref[...])

    for s in range(N_SLABS):
        ch = z_ref[rows, _cols(6, s)] * z_ref[rows, _cols(7, s)]
        short_ext[s, SHORT_HIST + r:SHORT_HIST + r + ROW_BLOCK, :] = ch
        sc = _causal_taps(short_ext, s, shortw_ref, SHORT_HIST, SHORT_WIDTH, r)
        y_ref[rows, _cols(3, s)] = (z_ref[rows, _cols(5, s)] * sc).astype(BF16)


def _layer_views(layer, *refs):
    return [r.at[pl.ds(layer, 1)] if len(r.shape) == 2 else r.at[layer] for r in refs]


def _prompt_mixer_kernel(layer, x_ref, win_ref, wout_ref, pinv_first_ref, pinv_rest_ref,
                         gpre_ref, gpost_ref, wpool_ref, pscale_ref, convw_ref, convb_ref, clng_ref,
                         clnb_ref, slng_ref, slnb_ref, sguw_ref, sbias_ref, shortw_ref,
                         wup_f32_ref, wdown_f32_ref,
                         out_ref, pool_out_ref, conv_out_ref, short_out_ref, wup_ref, wdown_ref,
                         z_ref, y_ref, vn_ref, swcat_ref, pool_ext, conv_ext, short_ext):
    (gpre_ref, gpost_ref, wpool_ref, pscale_ref, convw_ref, convb_ref, clng_ref, clnb_ref, slng_ref,
     slnb_ref, sguw_ref, sbias_ref, shortw_ref) = _layer_views(
         layer, gpre_ref, gpost_ref, wpool_ref, pscale_ref, convw_ref, convb_ref, clng_ref, clnb_ref,
         slng_ref, slnb_ref, sguw_ref, sbias_ref, shortw_ref)
    j = pl.program_id(1)
    tile = x_ref.shape[0]
    wup_ref[...] = wup_f32_ref[...].astype(BF16)
    wdown_ref[...] = wdown_f32_ref[...].astype(BF16)
    causal = (lax.broadcasted_iota(jnp.int32, (CHUNK, CHUNK), 0)
              >= lax.broadcasted_iota(jnp.int32, (CHUNK, CHUNK), 1))
    for hd in range(N_SUB):
        swcat_ref[:, hd * CHUNK:(hd + 1) * CHUNK] = jnp.where(causal, sguw_ref[hd], 0.0).astype(BF16)

    @pl.when(j == 0)
    def _():
        pool_ext[:, 0:POOL_HIST, :] = jnp.zeros((N_SLABS, POOL_HIST, LANES), F32)
        conv_ext[:, 0:CONV_HIST, :] = jnp.zeros((N_SLABS, CONV_HIST, LANES), F32)
        short_ext[:, 0:SHORT_HIST, :] = jnp.zeros((N_SLABS, SHORT_HIST, LANES), F32)

    def project_out(r0):
        rows = slice(r0, r0 + MATMUL_ROWS)
        pooled = _dot(y_ref[rows, 0:GROUP_WIDTH], wpool_ref[...]) * pscale_ref[...]
        y_ref[rows, 0:GROUP_WIDTH] = pooled.astype(BF16)
        o = _dot(y_ref[rows, :], wout_ref[...])
        out_ref[rows, :] = x_ref[rows, :] + _rms_norm(o, gpost_ref[...])

    for r0 in range(0, tile, MATMUL_ROWS):
        rows = slice(r0, r0 + MATMUL_ROWS)
        h = _rms_norm(x_ref[rows, :], gpre_ref[...]).astype(BF16)
        z_ref[rows, :] = _dot(h, win_ref[...])
        for r in range(r0, r0 + MATMUL_ROWS, ROW_BLOCK):
            _mix_rows(r, (j == 0) if r == 0 else None, z_ref, y_ref, vn_ref, pool_ext, conv_ext,
                      short_ext, pinv_first_ref, pinv_rest_ref, convw_ref, convb_ref, clng_ref,
                      clnb_ref, slng_ref, slnb_ref, shortw_ref)
            if r > 0:
                _gate_rows(r - ROW_BLOCK, z_ref, y_ref, vn_ref, swcat_ref, sbias_ref)
            if r == r0 and r0 > 0:
                project_out(r0 - MATMUL_ROWS)
    _gate_rows(tile - ROW_BLOCK, z_ref, y_ref, vn_ref, swcat_ref, sbias_ref)
    project_out(tile - MATMUL_ROWS)

    pool_ext[:, 0:POOL_HIST, :] = pool_ext[:, tile:tile + POOL_HIST, :]
    conv_ext[:, 0:CONV_HIST, :] = conv_ext[:, tile:tile + CONV_HIST, :]
    short_ext[:, 0:SHORT_HIST, :] = short_ext[:, tile:tile + SHORT_HIST, :]

    @pl.when(j == pl.num_programs(1) - 1)
    def _():
        for s in range(N_SLABS):
            lanes = slice(s * LANES, (s + 1) * LANES)
            pool_out_ref[:, lanes] = pool_ext[s, POOL_HIST - POOL_BUF:POOL_HIST, :]
            conv_out_ref[:, lanes] = conv_ext[s, CONV_HIST - (CONV_WIDTH - 1):CONV_HIST, :]
            short_out_ref[:, lanes] = short_ext[s, SHORT_HIST - (SHORT_WIDTH - 1):SHORT_HIST, :]


def _sample_mixer_kernel(layer, x_ref, pool_in_ref, conv_in_ref, short_in_ref, win_ref, wout_ref,
                         gpre_ref, gpost_ref, wpool_ref, pscale_ref, convw_ref, convb_ref, clng_ref,
                         clnb_ref, slng_ref, slnb_ref, sgw_ref, sgb_ref, shortw_ref,
                         out_ref, pool_out_ref, conv_out_ref, short_out_ref, v_out_ref,
                         z_ref, y_ref):
    (gpre_ref, gpost_ref, wpool_ref, pscale_ref, convw_ref, convb_ref, clng_ref, clnb_ref, slng_ref,
     slnb_ref, sgw_ref, sgb_ref, shortw_ref) = _layer_views(
         layer, gpre_ref, gpost_ref, wpool_ref, pscale_ref, convw_ref, convb_ref, clng_ref, clnb_ref,
         slng_ref, slnb_ref, sgw_ref, sgb_ref, shortw_ref)
    n_steps, n_seq = v_out_ref.shape[0], v_out_ref.shape[1]
    gw = GROUP_WIDTH
    x = x_ref[...].reshape(n_steps * n_seq, x_ref.shape[2])
    h = _rms_norm(x, gpre_ref[...]).astype(BF16)
    z_ref[...] = _dot(h, win_ref[...])
    grp, w = _pool_window((n_seq, gw))
    cnt = jnp.minimum(w, PAST_LEN + 1).astype(F32)

    def slab(t):
        return slice(t * n_seq, (t + 1) * n_seq)

    a_new = [z_ref[slab(t), 0:gw] for t in range(n_steps)]
    g_new = [z_ref[slab(t), gw:2 * gw] * jax.nn.sigmoid(z_ref[slab(t), 2 * gw:3 * gw])
             for t in range(n_steps)]
    ch_new = [z_ref[slab(t), 6 * gw:7 * gw] * z_ref[slab(t), 7 * gw:8 * gw]
              for t in range(n_steps)]

    def pool_row(i):
        return pool_in_ref[i] if i < POOL_BUF else a_new[i - POOL_BUF]

    def conv_row(i):
        return conv_in_ref[i] if i < CONV_WIDTH - 1 else g_new[i - (CONV_WIDTH - 1)]

    def short_row(i):
        return short_in_ref[i] if i < SHORT_WIDTH - 1 else ch_new[i - (SHORT_WIDTH - 1)]

    vn = []
    for t in range(n_steps):
        rows = slab(t)
        end = POOL_BUF + t
        acc = pool_row(end)
        sums = []
        for k in range(1, max(POOL_WINDOWS)):
            acc = acc + pool_row(end - k)
            if k + 1 in POOL_WINDOWS:
                sums.append(acc)
        win = sums[0]
        for gi in range(1, N_SUB):
            win = jnp.where(grp == gi, sums[gi], win)
        d = win / cnt - a_new[t]
        y_ref[rows, 0:gw] = d.astype(BF16)

        c = None
        for k in range(CONV_WIDTH):
            term = conv_row(t + k) * convw_ref[k:k + 1, :]
            c = term if c is None else c + term
        c = _head_layer_norm(c + convb_ref[...], clng_ref[...], clnb_ref[...])
        y_ref[rows, gw:2 * gw] = (c * jax.nn.sigmoid(c)).astype(BF16)

        vn.append(_head_layer_norm(z_ref[rows, 4 * gw:5 * gw], slng_ref[...], slnb_ref[...]))
        v_out_ref[t] = vn[t]
        s = sgb_ref[t:t + 1, :]
        for u in range(t + 1):
            s = s + sgw_ref[t * n_steps + u:t * n_steps + u + 1, :] * vn[u]
        y_ref[rows, 2 * gw:3 * gw] = (z_ref[rows, 3 * gw:4 * gw] * s).astype(BF16)

        sc = None
        for k in range(SHORT_WIDTH):
            term = short_row(t + k) * shortw_ref[k:k + 1, :]
            sc = term if sc is None else sc + term
        y_ref[rows, 3 * gw:4 * gw] = (z_ref[rows, 5 * gw:6 * gw] * sc).astype(BF16)

    pooled = _dot(y_ref[:, 0:gw], wpool_ref[...]) * pscale_ref[...]
    y_ref[:, 0:gw] = pooled.astype(BF16)
    o = _dot(y_ref[...], wout_ref[...])
    out_ref[...] = (x + _rms_norm(o, gpost_ref[...])).reshape(out_ref.shape)

    for i in range(POOL_BUF):
        pool_out_ref[i] = pool_row(i + n_steps)
    for i in range(CONV_WIDTH - 1):
        conv_out_ref[i] = conv_row(i + n_steps)
    for i in range(SHORT_WIDTH - 1):
        short_out_ref[i] = short_row(i + n_steps)


def _mlp(x_ref, out_ref, gpre_ref, gpost_ref, wup_ref, wdown_ref):
    x = x_ref[...]
    f = _rms_norm(x, gpre_ref[...]).astype(BF16)
    u = jnp.maximum(_dot(f, wup_ref[...]), 0.0)
    o = _dot((u * u).astype(BF16), wdown_ref[...])
    out_ref[...] = x + _rms_norm(o, gpost_ref[...])


def _ffn_kernel(layer, n_cast, x_ref, xs_ref, gpre_ref, gpost_ref, wup_ref, wdown_ref, *rest):
    cast_in, out_ref, outs_ref = rest[:n_cast], rest[n_cast], rest[n_cast + 1]
    cast_out = rest[n_cast + 2:]
    gpre_ref, gpost_ref = _layer_views(layer, gpre_ref, gpost_ref)
    i = pl.program_id(0)
    last = pl.num_programs(0) - 1

    @pl.when(i < last)
    def _():
        for src, dst in zip(cast_in, cast_out):
            dst[...] = src[...].astype(BF16)
        _mlp(x_ref, out_ref, gpre_ref, gpost_ref, wup_ref, wdown_ref)

    @pl.when(i == last)
    def _():
        _mlp(xs_ref, outs_ref, gpre_ref, gpost_ref, wup_ref, wdown_ref)


def _const_spec(shape):
    nd = len(shape)
    return pl.BlockSpec(shape, lambda *_: (0,) * nd, pipeline_mode=pl.Buffered(1))


def _whole_spec(shape):
    nd = len(shape)
    return pl.BlockSpec(shape, lambda *_: (0,) * nd)


def _compiler_params(semantics):
    return pltpu.CompilerParams(dimension_semantics=semantics, vmem_limit_bytes=VMEM_LIMIT_BYTES)


def _prompt_mixer(x, p, w_in, w_out, w_ffn_up, w_ffn_down, layer):
    b, s, d = x.shape
    tile = PROMPT_TILE
    consts = (w_in, w_out, POOL_INV_FIRST, POOL_INV_REST,
              p["gpre"], p["gpost"], p["wpool"], p["pscale"], p["conv_w"], p["conv_b"], p["cln_g"],
              p["cln_b"], p["sln_g"], p["sln_b"], p["sgu_w"], p["sgu_bias"], p["short_w"])
    gw = GROUP_WIDTH
    tiles_per_seq = s // tile
    steps = b * tiles_per_seq
    d_ff = w_ffn_up.shape[2]
    ff_slice = d_ff // steps
    assert ff_slice * steps == d_ff and ff_slice % LANES == 0
    step = lambda bi, j: bi * tiles_per_seq + j
    state_spec = lambda n: pl.BlockSpec((None, n, gw), lambda bi, j: (bi, 0, 0))
    return pl.pallas_call(
        functools.partial(_prompt_mixer_kernel, layer),
        grid=(b, tiles_per_seq),
        in_specs=[pl.BlockSpec((None, tile, d), lambda bi, j: (bi, j, 0))]
        + [_const_spec(c.shape) for c in consts]
        + [pl.BlockSpec((None, d, ff_slice), lambda bi, j: (layer, 0, step(bi, j))),
           pl.BlockSpec((None, ff_slice, d), lambda bi, j: (layer, step(bi, j), 0))],
        out_specs=[pl.BlockSpec((None, tile, d), lambda bi, j: (bi, j, 0)),
                   state_spec(POOL_BUF), state_spec(CONV_WIDTH - 1), state_spec(SHORT_WIDTH - 1),
                   pl.BlockSpec((d, ff_slice), lambda bi, j: (0, step(bi, j))),
                   pl.BlockSpec((ff_slice, d), lambda bi, j: (step(bi, j), 0))],
        out_shape=[jax.ShapeDtypeStruct((b, s, d), F32),
                   jax.ShapeDtypeStruct((b, POOL_BUF, gw), F32),
                   jax.ShapeDtypeStruct((b, CONV_WIDTH - 1, gw), F32),
                   jax.ShapeDtypeStruct((b, SHORT_WIDTH - 1, gw), F32),
                   jax.ShapeDtypeStruct((d, d_ff), BF16),
                   jax.ShapeDtypeStruct((d_ff, d), BF16)],
        scratch_shapes=[pltpu.VMEM((tile, IN_WIDTH), F32),
                        pltpu.VMEM((tile, d), BF16),
                        pltpu.VMEM((tile, gw), F32),
                        pltpu.VMEM((CHUNK, N_SUB * CHUNK), BF16),
                        pltpu.VMEM((N_SLABS, POOL_HIST + tile, LANES), F32),
                        pltpu.VMEM((N_SLABS, CONV_HIST + tile, LANES), F32),
                        pltpu.VMEM((N_SLABS, SHORT_HIST + tile, LANES), F32)],
        compiler_params=_compiler_params(("arbitrary", "arbitrary")),
        name="prompt_mixer",
    )(x, *consts, w_ffn_up, w_ffn_down)


def _sample_mixer(x, pool_st, conv_st, short_st, p, w_in, w_out, layer):
    n_steps, n_seq, d = x.shape
    group = SAMPLE_SEQS
    assert n_seq % group == 0 and group % SUBLANES == 0
    gw = GROUP_WIDTH
    states = (pool_st, conv_st, short_st)
    consts = (w_in, w_out, p["gpre"], p["gpost"], p["wpool"], p["pscale"], p["conv_w"], p["conv_b"],
              p["cln_g"], p["cln_b"], p["sln_g"], p["sln_b"], p["sgu_w4"], p["sgu_bias"], p["short_w"])
    group_spec = lambda a: pl.BlockSpec((a.shape[0], group, a.shape[2]), lambda i: (0, i, 0))
    state_spec = lambda st: pl.BlockSpec((None, st.shape[1], group, gw), lambda i: (layer, 0, i, 0))
    out_shape = ([jax.ShapeDtypeStruct((n_steps, n_seq, d), F32)]
                 + [jax.ShapeDtypeStruct(st.shape[1:], F32) for st in states]
                 + [jax.ShapeDtypeStruct((n_steps, n_seq, gw), F32)])
    rows = n_steps * group
    return pl.pallas_call(
        functools.partial(_sample_mixer_kernel, layer),
        grid=(n_seq // group,),
        in_specs=[group_spec(x)] + [state_spec(st) for st in states]
        + [_const_spec(c.shape) for c in consts],
        out_specs=[group_spec(o) for o in out_shape],
        out_shape=out_shape,
        scratch_shapes=[pltpu.VMEM((rows, IN_WIDTH), F32), pltpu.VMEM((rows, d), BF16)],
        compiler_params=_compiler_params(("arbitrary",)),
        name="sample_mixer",
    )(x, *states, *consts)


def _ffn(x, xs, p, w_up, w_down, layer, cast=()):
    n, d = x.shape
    ns = xs.shape[0]
    tile = FFN_TILE
    steps = n // tile
    assert steps * tile == n and ns <= tile
    consts = (p["fpre"], p["fpost"], w_up, w_down)
    prompt_tile = lambda i: (jnp.minimum(i, steps - 1), 0)
    cast_in_specs, cast_out_specs, cast_shapes = [], [], []
    for w, w_layer in cast:
        rows = w.shape[1] // steps
        assert rows * steps == w.shape[1] and rows % (2 * SUBLANES) == 0
        cast_in_specs.append(pl.BlockSpec(
            (None, rows, w.shape[2]), lambda i, w_layer=w_layer: (w_layer,) + prompt_tile(i)))
        cast_out_specs.append(pl.BlockSpec((rows, w.shape[2]), prompt_tile))
        cast_shapes.append(jax.ShapeDtypeStruct(w.shape[1:], BF16))
    return pl.pallas_call(
        functools.partial(_ffn_kernel, layer, len(cast)),
        grid=(steps + 1,),
        in_specs=[pl.BlockSpec((tile, d), prompt_tile), _whole_spec(xs.shape)]
        + [_const_spec(c.shape) for c in consts] + cast_in_specs,
        out_specs=[pl.BlockSpec((tile, d), prompt_tile), _whole_spec(xs.shape)] + cast_out_specs,
        out_shape=[jax.ShapeDtypeStruct((n, d), F32), jax.ShapeDtypeStruct((ns, d), F32)] + cast_shapes,
        compiler_params=_compiler_params(("arbitrary",)),
        name="ffn",
    )(x, xs, *consts, *[w for w, _ in cast])


def _pool_inverse_counts():
    window = np.repeat(np.asarray(POOL_WINDOWS, np.float32), SUB_DIM)[None, :]
    first = 1.0 / np.minimum(window, np.arange(1, ROW_BLOCK + 1, dtype=np.float32)[:, None])
    return first.astype(np.float32), (1.0 / window).astype(np.float32)


POOL_INV_FIRST, POOL_INV_REST = _pool_inverse_counts()


def _stacked_params(n_steps, norm_mix_pre, norm_mix_post, norm_ffn_pre, norm_ffn_post, w_pool,
                    pool_scale, conv_w, conv_b, conv_ln_g, conv_ln_b, sgu_ln_g, sgu_ln_b, sgu_w,
                    sgu_b, short_w):
    depth = w_pool.shape[0]
    gw = GROUP_WIDTH
    head_of_lane = np.arange(gw) // SUB_DIM
    same_head = head_of_lane[:, None] == head_of_lane[None, :]
    wp = jnp.tile(w_pool.reshape(depth, gw, SUB_DIM), (1, 1, N_SUB))
    wpool = jnp.where(same_head[None], wp, 0.0).astype(BF16)
    bias = jnp.repeat(jnp.swapaxes(sgu_b, 1, 2), SUB_DIM, axis=2)
    causal = np.tril(np.ones((n_steps, n_steps), dtype=bool))
    w4 = jnp.where(causal[None, None], sgu_w[:, :, :n_steps, :n_steps], 0.0)
    w4 = jnp.repeat(jnp.transpose(w4, (0, 2, 3, 1)), SUB_DIM, axis=3)
    w4 = w4.reshape(depth, n_steps * n_steps, gw)
    return dict(gpre=norm_mix_pre, gpost=norm_mix_post, fpre=norm_ffn_pre, fpost=norm_ffn_post,
                wpool=wpool, pscale=pool_scale, conv_w=conv_w, conv_b=conv_b, cln_g=conv_ln_g,
                cln_b=conv_ln_b, sln_g=sgu_ln_g, sln_b=sgu_ln_b, sgu_w=sgu_w, sgu_bias=bias,
                sgu_w4=w4, short_w=short_w)


def kernel(x_prompt, x_sample, state_pool, state_conv, state_short, norm_mix_pre, norm_mix_post, norm_ffn_pre, norm_ffn_post, w_in, w_out, w_pool, pool_scale, conv_w, conv_b, conv_ln_g, conv_ln_b, sgu_ln_g, sgu_ln_b, sgu_w, sgu_b, short_w, w_ffn_up, w_ffn_down):
    depth = w_in.shape[0]
    bp, seq, d = x_prompt.shape
    n_seq, n_steps, _ = x_sample.shape
    assert seq % PROMPT_TILE == 0 and PROMPT_TILE % MATMUL_ROWS == 0 and MATMUL_ROWS % ROW_BLOCK == 0
    assert ROW_BLOCK >= max(POOL_WINDOWS) and n_steps <= CHUNK and PAST_LEN % CHUNK == 0

    p = _stacked_params(n_steps, norm_mix_pre, norm_mix_post, norm_ffn_pre, norm_ffn_post, w_pool,
                        pool_scale, conv_w, conv_b, conv_ln_g, conv_ln_b, sgu_ln_g, sgu_ln_b, sgu_w,
                        sgu_b, short_w)
    yp = x_prompt
    ys = jnp.transpose(x_sample, (1, 0, 2))
    hist_major = lambda a: jnp.transpose(a, (0, 2, 1, 3))
    pool_in, conv_in, short_in = hist_major(state_pool), hist_major(state_conv), hist_major(state_short)
    prompt_states = [[] for _ in range(3)]
    sample_states = [[] for _ in range(4)]
    w_in_b, w_out_b = w_in[0].astype(BF16), w_out[0].astype(BF16)
    for l in range(depth):
        yp, pool_p, conv_p, short_p, w_up_b, w_down_b = _prompt_mixer(
            yp, p, w_in_b, w_out_b, w_ffn_up, w_ffn_down, l)
        ys, pool_s, conv_s, short_s, v_s = _sample_mixer(
            ys, pool_in, conv_in, short_in, p, w_in_b, w_out_b, l)
        cast = ((w_in, l + 1), (w_out, l + 1)) if l + 1 < depth else ()
        yp, ys, *w_next = _ffn(yp.reshape(bp * seq, d), ys.reshape(n_steps * n_seq, d), p, w_up_b,
                               w_down_b, l, cast)
        yp, ys = yp.reshape(bp, seq, d), ys.reshape(n_steps, n_seq, d)
        if w_next:
            w_in_b, w_out_b = w_next

        for lst, val in zip(prompt_states, (pool_p, conv_p, short_p)):
            lst.append(val)
        for lst, val in zip(sample_states, (pool_s, conv_s, short_s, v_s)):
            lst.append(val)

    ys = jnp.transpose(ys, (1, 0, 2))
    pool_p, conv_p, short_p = (jnp.stack(o) for o in prompt_states)
    pool_s, conv_s, short_s, v_s = (hist_major(jnp.stack(o)) for o in sample_states)
    return (yp, ys, pool_p, pool_s, conv_p, conv_s, short_p, short_s, v_s)
```

```python
import functools

import jax
import jax.numpy as jnp
import numpy as np
from jax import lax
from jax.experimental import pallas as pl
from jax.experimental.pallas import tpu as pltpu

D_MODEL = 1024
GROUP_WIDTH = 256
N_SUB = 4
SUB_DIM = 64
POOL_WINDOWS = (2, 4, 8, 16)
POOL_BUF = 15
CONV_WIDTH = 31
SHORT_WIDTH = 3
CHUNK = 128
D_FF = 4096
EPS = 1e-6
PAST_LEN = 16384
IN_WIDTH = 8 * GROUP_WIDTH

SUBLANES = 8
LANES = 128
N_SLABS = GROUP_WIDTH // LANES
POOL_HIST = 16
CONV_HIST = 32
SHORT_HIST = 8

ROW_BLOCK = CHUNK
MATMUL_ROWS = 512
PROMPT_TILE = 1024
SAMPLE_SEQS = 64
FFN_TILE = 1024
VMEM_LIMIT_BYTES = 56 * 1024 * 1024

F32 = jnp.float32
BF16 = jnp.bfloat16


def _rms_norm(x, g):
    ms = jnp.mean(x * x, axis=-1, keepdims=True)
    return x * lax.rsqrt(ms + EPS) * g


def _dot(a, b):
    return jnp.dot(a, b, preferred_element_type=F32)


def _head_mean(x, low_head):
    s_low = jnp.sum(jnp.where(low_head, x, 0.0), axis=-1, keepdims=True)
    s_high = jnp.sum(jnp.where(low_head, 0.0, x), axis=-1, keepdims=True)
    return jnp.where(low_head, s_low, s_high) * (1.0 / SUB_DIM)


def _head_layer_norm(x, g, b):
    low_head = lax.broadcasted_iota(jnp.int32, (x.shape[0], LANES), 1) < SUB_DIM
    out = []
    for s in range(N_SLABS):
        lanes = slice(s * LANES, (s + 1) * LANES)
        xs = x[:, lanes]
        xc = xs - _head_mean(xs, low_head)
        var = _head_mean(xc * xc, low_head)
        out.append(xc * lax.rsqrt(var + EPS) * g[:, lanes] + b[:, lanes])
    return jnp.concatenate(out, axis=1)


def _lane_group(shape):
    return jnp.right_shift(lax.broadcasted_iota(jnp.int32, shape, 1), SUB_DIM.bit_length() - 1)


def _pool_window(shape):
    grp = _lane_group(shape)
    w = jnp.full(shape, POOL_WINDOWS[0], jnp.int32)
    for gi in range(1, N_SUB):
        w = jnp.where(grp == gi, POOL_WINDOWS[gi], w)
    return grp, w


def _cols(group, slab):
    lo = group * GROUP_WIDTH + slab * LANES
    return slice(lo, lo + LANES)


def _trailing_sum(ext_ref, slab, start, n_rows, width):
    acc = ext_ref[slab, start:start + n_rows, :]
    for k in range(1, width):
        acc = acc + ext_ref[slab, start - k:start - k + n_rows, :]
    return acc


def _causal_taps(ext_ref, slab, w_ref, hist, n_taps, r):
    first = hist + r - (n_taps - 1)
    lanes = slice(slab * LANES, (slab + 1) * LANES)
    acc = None
    for k in range(n_taps):
        term = ext_ref[slab, first + k:first + k + ROW_BLOCK, :] * w_ref[k:k + 1, lanes]
        acc = term if acc is None else acc + term
    return acc


def _gate_block(r0, z_ref, y_ref, vn_ref, swt_ref, sbias_ref):
    gw = GROUP_WIDTH
    chunks = range(r0, r0 + MATMUL_ROWS, ROW_BLOCK)
    vt = [vn_ref[r:r + ROW_BLOCK, :].T.astype(BF16) for r in chunks]
    mixed = []
    for h in range(N_SUB):
        head = slice(h * SUB_DIM, (h + 1) * SUB_DIM)
        lhs = jnp.concatenate([v[head, :] for v in vt], axis=0)
        mixed.append(_dot(lhs, swt_ref[h]))
    for ci, r in enumerate(chunks):
        rows = slice(r, r + ROW_BLOCK)
        part = slice(ci * SUB_DIM, (ci + 1) * SUB_DIM)
        sg = jnp.concatenate([m[part, :] for m in mixed], axis=0).T + sbias_ref[...]
        y_ref[rows, 2 * gw:3 * gw] = (z_ref[rows, 3 * gw:4 * gw] * sg).astype(BF16)


def _mix_rows(r, seq_start, z_ref, y_ref, vn_ref, pool_ext, conv_ext, short_ext,
              pinv_first_ref, pinv_rest_ref, convw_ref, convb_ref, clng_ref, clnb_ref,
              slng_ref, slnb_ref, shortw_ref):
    rows = slice(r, r + ROW_BLOCK)
    gw = GROUP_WIDTH
    low_head = lax.broadcasted_iota(jnp.int32, (ROW_BLOCK, LANES), 1) < SUB_DIM

    d = []
    for s in range(N_SLABS):
        lanes = slice(s * LANES, (s + 1) * LANES)
        a = z_ref[rows, _cols(0, s)]
        base = POOL_HIST + r
        pool_ext[s, base:base + ROW_BLOCK, :] = a
        small, large = POOL_WINDOWS[2 * s], POOL_WINDOWS[2 * s + 1]
        assert large == 2 * small
        if small % SUBLANES == 0:
            run = _trailing_sum(pool_ext, s, base - small, ROW_BLOCK + small, small)
            s_small = run[small:]
            s_large = s_small + run[:ROW_BLOCK]
        else:
            s_small = _trailing_sum(pool_ext, s, base, ROW_BLOCK, small)
            s_large = s_small + _trailing_sum(pool_ext, s, base - small, ROW_BLOCK, small)
        win = jnp.where(low_head, s_small, s_large)
        inv = pinv_rest_ref[:, lanes]
        if seq_start is not None:
            inv = jnp.where(seq_start, pinv_first_ref[:, lanes], inv)
        d.append(win * inv - a)
    d = jnp.concatenate(d, axis=1).astype(BF16)
    y_ref[rows, 0:gw] = d

    c = []
    for s in range(N_SLABS):
        g = z_ref[rows, _cols(1, s)] * jax.nn.sigmoid(z_ref[rows, _cols(2, s)])
        conv_ext[s, CONV_HIST + r:CONV_HIST + r + ROW_BLOCK, :] = g
        c.append(_causal_taps(conv_ext, s, convw_ref, CONV_HIST, CONV_WIDTH, r))
    c = jnp.concatenate(c, axis=1) + convb_ref[...]
    c = _head_layer_norm(c, clng_ref[...], clnb_ref[...])
    y_ref[rows, gw:2 * gw] = (c * jax.nn.sigmoid(c)).astype(BF16)

    vn_ref[rows, :] = _head_layer_norm(z_ref[rows, 4 * gw:5 * gw], slng_ref[...], slnb_ref[...])

    for s in range(N_SLABS):
        ch = z_ref[rows, _cols(6, s)] * z_ref[rows, _cols(7, s)]
        short_ext[s, SHORT_HIST + r:SHORT_HIST + r + ROW_BLOCK, :] = ch
        sc = _causal_taps(short_ext, s, shortw_ref, SHORT_HIST, SHORT_WIDTH, r)
        y_ref[rows, _cols(3, s)] = (z_ref[rows, _cols(5, s)] * sc).astype(BF16)


def _layer_views(layer, *refs):
    return [r.at[pl.ds(layer, 1)] if len(r.shape) == 2 else r.at[layer] for r in refs]


def _prompt_mixer_kernel(layer, x_ref, win_ref, wout_ref, pinv_first_ref, pinv_rest_ref,
                         gpre_ref, gpost_ref, wpool_ref, pscale_ref, convw_ref, convb_ref, clng_ref,
                         clnb_ref, slng_ref, slnb_ref, sguw_ref, sbias_ref, shortw_ref,
                         wup_f32_ref, wdown_f32_ref,
                         out_ref, pool_out_ref, conv_out_ref, short_out_ref, wup_ref, wdown_ref,
                         z_ref, y_ref, vn_ref, swt_ref, pool_ext, conv_ext, short_ext):
    (gpre_ref, gpost_ref, wpool_ref, pscale_ref, convw_ref, convb_ref, clng_ref, clnb_ref, slng_ref,
     slnb_ref, sguw_ref, sbias_ref, shortw_ref) = _layer_views(
         layer, gpre_ref, gpost_ref, wpool_ref, pscale_ref, convw_ref, convb_ref, clng_ref, clnb_ref,
         slng_ref, slnb_ref, sguw_ref, sbias_ref, shortw_ref)
    j = pl.program_id(1)
    tile = x_ref.shape[0]
    wup_ref[...] = wup_f32_ref[...].astype(BF16)
    wdown_ref[...] = wdown_f32_ref[...].astype(BF16)
    causal = (lax.broadcasted_iota(jnp.int32, (CHUNK, CHUNK), 0)
              >= lax.broadcasted_iota(jnp.int32, (CHUNK, CHUNK), 1))
    for hd in range(N_SUB):
        swt_ref[hd] = jnp.where(causal, sguw_ref[hd], 0.0).T.astype(BF16)

    @pl.when(j == 0)
    def _():
        pool_ext[:, 0:POOL_HIST, :] = jnp.zeros((N_SLABS, POOL_HIST, LANES), F32)
        conv_ext[:, 0:CONV_HIST, :] = jnp.zeros((N_SLABS, CONV_HIST, LANES), F32)
        short_ext[:, 0:SHORT_HIST, :] = jnp.zeros((N_SLABS, SHORT_HIST, LANES), F32)

    def project_out(r0):
        rows = slice(r0, r0 + MATMUL_ROWS)
        pooled = _dot(y_ref[rows, 0:GROUP_WIDTH], wpool_ref[...]) * pscale_ref[...]
        y_ref[rows, 0:GROUP_WIDTH] = pooled.astype(BF16)
        o = _dot(y_ref[rows, :], wout_ref[...])
        out_ref[rows, :] = x_ref[rows, :] + _rms_norm(o, gpost_ref[...])

    for r0 in range(0, tile, MATMUL_ROWS):
        rows = slice(r0, r0 + MATMUL_ROWS)
        h = _rms_norm(x_ref[rows, :], gpre_ref[...]).astype(BF16)
        z_ref[rows, :] = _dot(h, win_ref[...])
        for r in range(r0, r0 + MATMUL_ROWS, ROW_BLOCK):
            _mix_rows(r, (j == 0) if r == 0 else None, z_ref, y_ref, vn_ref, pool_ext, conv_ext,
                      short_ext, pinv_first_ref, pinv_rest_ref, convw_ref, convb_ref, clng_ref,
                      clnb_ref, slng_ref, slnb_ref, shortw_ref)
            if r == r0 and r0 > 0:
                _gate_block(r0 - MATMUL_ROWS, z_ref, y_ref, vn_ref, swt_ref, sbias_ref)
                project_out(r0 - MATMUL_ROWS)
    _gate_block(tile - MATMUL_ROWS, z_ref, y_ref, vn_ref, swt_ref, sbias_ref)
    project_out(tile - MATMUL_ROWS)

    pool_ext[:, 0:POOL_HIST, :] = pool_ext[:, tile:tile + POOL_HIST, :]
    conv_ext[:, 0:CONV_HIST, :] = conv_ext[:, tile:tile + CONV_HIST, :]
    short_ext[:, 0:SHORT_HIST, :] = short_ext[:, tile:tile + SHORT_HIST, :]

    @pl.when(j == pl.num_programs(1) - 1)
    def _():
        for s in range(N_SLABS):
            lanes = slice(s * LANES, (s + 1) * LANES)
            pool_out_ref[:, lanes] = pool_ext[s, POOL_HIST - POOL_BUF:POOL_HIST, :]
            conv_out_ref[:, lanes] = conv_ext[s, CONV_HIST - (CONV_WIDTH - 1):CONV_HIST, :]
            short_out_ref[:, lanes] = short_ext[s, SHORT_HIST - (SHORT_WIDTH - 1):SHORT_HIST, :]


def _sample_mixer_kernel(layer, x_ref, pool_in_ref, conv_in_ref, short_in_ref, win_ref, wout_ref,
                         gpre_ref, gpost_ref, wpool_ref, pscale_ref, convw_ref, convb_ref, clng_ref,
                         clnb_ref, slng_ref, slnb_ref, sgw_ref, sgb_ref, shortw_ref,
                         out_ref, pool_out_ref, conv_out_ref, short_out_ref, v_out_ref,
                         z_ref, y_ref):
    (gpre_ref, gpost_ref, wpool_ref, pscale_ref, convw_ref, convb_ref, clng_ref, clnb_ref, slng_ref,
     slnb_ref, sgw_ref, sgb_ref, shortw_ref) = _layer_views(
         layer, gpre_ref, gpost_ref, wpool_ref, pscale_ref, convw_ref, convb_ref, clng_ref, clnb_ref,
         slng_ref, slnb_ref, sgw_ref, sgb_ref, shortw_ref)
    n_steps, n_seq = v_out_ref.shape[0], v_out_ref.shape[1]
    gw = GROUP_WIDTH
    x = x_ref[...].reshape(n_steps * n_seq, x_ref.shape[2])
    h = _rms_norm(x, gpre_ref[...]).astype(BF16)
    z_ref[...] = _dot(h, win_ref[...])
    grp, w = _pool_window((n_seq, gw))
    cnt = jnp.minimum(w, PAST_LEN + 1).astype(F32)

    def slab(t):
        return slice(t * n_seq, (t + 1) * n_seq)

    a_new = [z_ref[slab(t), 0:gw] for t in range(n_steps)]
    g_new = [z_ref[slab(t), gw:2 * gw] * jax.nn.sigmoid(z_ref[slab(t), 2 * gw:3 * gw])
             for t in range(n_steps)]
    ch_new = [z_ref[slab(t), 6 * gw:7 * gw] * z_ref[slab(t), 7 * gw:8 * gw]
              for t in range(n_steps)]

    def pool_row(i):
        return pool_in_ref[i] if i < POOL_BUF else a_new[i - POOL_BUF]

    def conv_row(i):
        return conv_in_ref[i] if i < CONV_WIDTH - 1 else g_new[i - (CONV_WIDTH - 1)]

    def short_row(i):
        return short_in_ref[i] if i < SHORT_WIDTH - 1 else ch_new[i - (SHORT_WIDTH - 1)]

    vn = []
    for t in range(n_steps):
        rows = slab(t)
        end = POOL_BUF + t
        acc = pool_row(end)
        sums = []
        for k in range(1, max(POOL_WINDOWS)):
            acc = acc + pool_row(end - k)
            if k + 1 in POOL_WINDOWS:
                sums.append(acc)
        win = sums[0]
        for gi in range(1, N_SUB):
            win = jnp.where(grp == gi, sums[gi], win)
        d = win / cnt - a_new[t]
        y_ref[rows, 0:gw] = d.astype(BF16)

        c = None
        for k in range(CONV_WIDTH):
            term = conv_row(t + k) * convw_ref[k:k + 1, :]
            c = term if c is None else c + term
        c = _head_layer_norm(c + convb_ref[...], clng_ref[...], clnb_ref[...])
        y_ref[rows, gw:2 * gw] = (c * jax.nn.sigmoid(c)).astype(BF16)

        vn.append(_head_layer_norm(z_ref[rows, 4 * gw:5 * gw], slng_ref[...], slnb_ref[...]))
        v_out_ref[t] = vn[t]
        s = sgb_ref[t:t + 1, :]
        for u in range(t + 1):
            s = s + sgw_ref[t * n_steps + u:t * n_steps + u + 1, :] * vn[u]
        y_ref[rows, 2 * gw:3 * gw] = (z_ref[rows, 3 * gw:4 * gw] * s).astype(BF16)

        sc = None
        for k in range(SHORT_WIDTH):
            term = short_row(t + k) * shortw_ref[k:k + 1, :]
            sc = term if sc is None else sc + term
        y_ref[rows, 3 * gw:4 * gw] = (z_ref[rows, 5 * gw:6 * gw] * sc).astype(BF16)

    pooled = _dot(y_ref[:, 0:gw], wpool_ref[...]) * pscale_ref[...]
    y_ref[:, 0:gw] = pooled.astype(BF16)
    o = _dot(y_ref[...], wout_ref[...])
    out_ref[...] = (x + _rms_norm(o, gpost_ref[...])).reshape(out_ref.shape)

    for i in range(POOL_BUF):
        pool_out_ref[i] = pool_row(i + n_steps)
    for i in range(CONV_WIDTH - 1):
        conv_out_ref[i] = conv_row(i + n_steps)
    for i in range(SHORT_WIDTH - 1):
        short_out_ref[i] = short_row(i + n_steps)


def _mlp(x_ref, out_ref, gpre_ref, gpost_ref, wup_ref, wdown_ref):
    x = x_ref[...]
    f = _rms_norm(x, gpre_ref[...]).astype(BF16)
    u = jnp.maximum(_dot(f, wup_ref[...]), 0.0)
    o = _dot((u * u).astype(BF16), wdown_ref[...])
    out_ref[...] = x + _rms_norm(o, gpost_ref[...])


def _ffn_kernel(layer, n_cast, x_ref, xs_ref, gpre_ref, gpost_ref, wup_ref, wdown_ref, *rest):
    cast_in, out_ref, outs_ref = rest[:n_cast], rest[n_cast], rest[n_cast + 1]
    cast_out = rest[n_cast + 2:]
    gpre_ref, gpost_ref = _layer_views(layer, gpre_ref, gpost_ref)
    i = pl.program_id(0)
    last = pl.num_programs(0) - 1

    @pl.when(i < last)
    def _():
        for src, dst in zip(cast_in, cast_out):
            dst[...] = src[...].astype(BF16)
        _mlp(x_ref, out_ref, gpre_ref, gpost_ref, wup_ref, wdown_ref)

    @pl.when(i == last)
    def _():
        _mlp(xs_ref, outs_ref, gpre_ref, gpost_ref, wup_ref, wdown_ref)


def _const_spec(shape):
    nd = len(shape)
    return pl.BlockSpec(shape, lambda *_: (0,) * nd, pipeline_mode=pl.Buffered(1))


def _whole_spec(shape):
    nd = len(shape)
    return pl.BlockSpec(shape, lambda *_: (0,) * nd)


def _compiler_params(semantics):
    return pltpu.CompilerParams(dimension_semantics=semantics, vmem_limit_bytes=VMEM_LIMIT_BYTES)


def _prompt_mixer(x, p, w_in, w_out, w_ffn_up, w_ffn_down, layer):
    b, s, d = x.shape
    tile = PROMPT_TILE
    consts = (w_in, w_out, POOL_INV_FIRST, POOL_INV_REST,
              p["gpre"], p["gpost"], p["wpool"], p["pscale"], p["conv_w"], p["conv_b"], p["cln_g"],
              p["cln_b"], p["sln_g"], p["sln_b"], p["sgu_w"], p["sgu_bias"], p["short_w"])
    gw = GROUP_WIDTH
    tiles_per_seq = s // tile
    steps = b * tiles_per_seq
    d_ff = w_ffn_up.shape[2]
    ff_slice = d_ff // steps
    assert ff_slice * steps == d_ff and ff_slice % LANES == 0
    step = lambda bi, j: bi * tiles_per_seq + j
    state_spec = lambda n: pl.BlockSpec((None, n, gw), lambda bi, j: (bi, 0, 0))
    return pl.pallas_call(
        functools.partial(_prompt_mixer_kernel, layer),
        grid=(b, tiles_per_seq),
        in_specs=[pl.BlockSpec((None, tile, d), lambda bi, j: (bi, j, 0))]
        + [_const_spec(c.shape) for c in consts]
        + [pl.BlockSpec((None, d, ff_slice), lambda bi, j: (layer, 0, step(bi, j))),
           pl.BlockSpec((None, ff_slice, d), lambda bi, j: (layer, step(bi, j), 0))],
        out_specs=[pl.BlockSpec((None, tile, d), lambda bi, j: (bi, j, 0)),
                   state_spec(POOL_BUF), state_spec(CONV_WIDTH - 1), state_spec(SHORT_WIDTH - 1),
                   pl.BlockSpec((d, ff_slice), lambda bi, j: (0, step(bi, j))),
                   pl.BlockSpec((ff_slice, d), lambda bi, j: (step(bi, j), 0))],
        out_shape=[jax.ShapeDtypeStruct((b, s, d), F32),
                   jax.ShapeDtypeStruct((b, POOL_BUF, gw), F32),
                   jax.ShapeDtypeStruct((b, CONV_WIDTH - 1, gw), F32),
                   jax.ShapeDtypeStruct((b, SHORT_WIDTH - 1, gw), F32),
                   jax.ShapeDtypeStruct((d, d_ff), BF16),
                   jax.ShapeDtypeStruct((d_ff, d), BF16)],
        scratch_shapes=[pltpu.VMEM((tile, IN_WIDTH), F32),
                        pltpu.VMEM((tile, d), BF16),
                        pltpu.VMEM((tile, gw), F32),
                        pltpu.VMEM((N_SUB, CHUNK, CHUNK), BF16),
                        pltpu.VMEM((N_SLABS, POOL_HIST + tile, LANES), F32),
                        pltpu.VMEM((N_SLABS, CONV_HIST + tile, LANES), F32),
                        pltpu.VMEM((N_SLABS, SHORT_HIST + tile, LANES), F32)],
        compiler_params=_compiler_params(("arbitrary", "arbitrary")),
        name="prompt_mixer",
    )(x, *consts, w_ffn_up, w_ffn_down)


def _sample_mixer(x, pool_st, conv_st, short_st, p, w_in, w_out, layer):
    n_steps, n_seq, d = x.shape
    group = SAMPLE_SEQS
    assert n_seq % group == 0 and group % SUBLANES == 0
    gw = GROUP_WIDTH
    states = (pool_st, conv_st, short_st)
    consts = (w_in, w_out, p["gpre"], p["gpost"], p["wpool"], p["pscale"], p["conv_w"], p["conv_b"],
              p["cln_g"], p["cln_b"], p["sln_g"], p["sln_b"], p["sgu_w4"], p["sgu_bias"], p["short_w"])
    group_spec = lambda a: pl.BlockSpec((a.shape[0], group, a.shape[2]), lambda i: (0, i, 0))
    state_spec = lambda st: pl.BlockSpec((None, st.shape[1], group, gw), lambda i: (layer, 0, i, 0))
    out_shape = ([jax.ShapeDtypeStruct((n_steps, n_seq, d), F32)]
                 + [jax.ShapeDtypeStruct(st.shape[1:], F32) for st in states]
                 + [jax.ShapeDtypeStruct((n_steps, n_seq, gw), F32)])
    rows = n_steps * group
    return pl.pallas_call(
        functools.partial(_sample_mixer_kernel, layer),
        grid=(n_seq // group,),
        in_specs=[group_spec(x)] + [state_spec(st) for st in states]
        + [_const_spec(c.shape) for c in consts],
        out_specs=[group_spec(o) for o in out_shape],
        out_shape=out_shape,
        scratch_shapes=[pltpu.VMEM((rows, IN_WIDTH), F32), pltpu.VMEM((rows, d), BF16)],
        compiler_params=_compiler_params(("arbitrary",)),
        name="sample_mixer",
    )(x, *states, *consts)


def _ffn(x, xs, p, w_up, w_down, layer, cast=()):
    n, d = x.shape
    ns = xs.shape[0]
    tile = FFN_TILE
    steps = n // tile
    assert steps * tile == n and ns <= tile
    consts = (p["fpre"], p["fpost"], w_up, w_down)
    prompt_tile = lambda i: (jnp.minimum(i, steps - 1), 0)
    cast_in_specs, cast_out_specs, cast_shapes = [], [], []
    for w, w_layer in cast:
        rows = w.shape[1] // steps
        assert rows * steps == w.shape[1] and rows % (2 * SUBLANES) == 0
        cast_in_specs.append(pl.BlockSpec(
            (None, rows, w.shape[2]), lambda i, w_layer=w_layer: (w_layer,) + prompt_tile(i)))
        cast_out_specs.append(pl.BlockSpec((rows, w.shape[2]), prompt_tile))
        cast_shapes.append(jax.ShapeDtypeStruct(w.shape[1:], BF16))
    return pl.pallas_call(
        functools.partial(_ffn_kernel, layer, len(cast)),
        grid=(steps + 1,),
        in_specs=[pl.BlockSpec((tile, d), prompt_tile), _whole_spec(xs.shape)]
        + [_const_spec(c.shape) for c in consts] + cast_in_specs,
        out_specs=[pl.BlockSpec((tile, d), prompt_tile), _whole_spec(xs.shape)] + cast_out_specs,
        out_shape=[jax.ShapeDtypeStruct((n, d), F32), jax.ShapeDtypeStruct((ns, d), F32)] + cast_shapes,
        compiler_params=_compiler_params(("arbitrary",)),
        name="ffn",
    )(x, xs, *consts, *[w for w, _ in cast])


def _pool_inverse_counts():
    window = np.repeat(np.asarray(POOL_WINDOWS, np.float32), SUB_DIM)[None, :]
    first = 1.0 / np.minimum(window, np.arange(1, ROW_BLOCK + 1, dtype=np.float32)[:, None])
    return first.astype(np.float32), (1.0 / window).astype(np.float32)


POOL_INV_FIRST, POOL_INV_REST = _pool_inverse_counts()


def _stacked_params(n_steps, norm_mix_pre, norm_mix_post, norm_ffn_pre, norm_ffn_post, w_pool,
                    pool_scale, conv_w, conv_b, conv_ln_g, conv_ln_b, sgu_ln_g, sgu_ln_b, sgu_w,
                    sgu_b, short_w):
    depth = w_pool.shape[0]
    gw = GROUP_WIDTH
    head_of_lane = np.arange(gw) // SUB_DIM
    same_head = head_of_lane[:, None] == head_of_lane[None, :]
    wp = jnp.tile(w_pool.reshape(depth, gw, SUB_DIM), (1, 1, N_SUB))
    wpool = jnp.where(same_head[None], wp, 0.0).astype(BF16)
    bias = jnp.repeat(jnp.swapaxes(sgu_b, 1, 2), SUB_DIM, axis=2)
    causal = np.tril(np.ones((n_steps, n_steps), dtype=bool))
    w4 = jnp.where(causal[None, None], sgu_w[:, :, :n_steps, :n_steps], 0.0)
    w4 = jnp.repeat(jnp.transpose(w4, (0, 2, 3, 1)), SUB_DIM, axis=3)
    w4 = w4.reshape(depth, n_steps * n_steps, gw)
    return dict(gpre=norm_mix_pre, gpost=norm_mix_post, fpre=norm_ffn_pre, fpost=norm_ffn_post,
                wpool=wpool, pscale=pool_scale, conv_w=conv_w, conv_b=conv_b, cln_g=conv_ln_g,
                cln_b=conv_ln_b, sln_g=sgu_ln_g, sln_b=sgu_ln_b, sgu_w=sgu_w, sgu_bias=bias,
                sgu_w4=w4, short_w=short_w)


def kernel(x_prompt, x_sample, state_pool, state_conv, state_short, norm_mix_pre, norm_mix_post, norm_ffn_pre, norm_ffn_post, w_in, w_out, w_pool, pool_scale, conv_w, conv_b, conv_ln_g, conv_ln_b, sgu_ln_g, sgu_ln_b, sgu_w, sgu_b, short_w, w_ffn_up, w_ffn_down):
    depth = w_in.shape[0]
    bp, seq, d = x_prompt.shape
    n_seq, n_steps, _ = x_sample.shape
    assert seq % PROMPT_TILE == 0 and PROMPT_TILE % MATMUL_ROWS == 0 and MATMUL_ROWS % ROW_BLOCK == 0
    assert ROW_BLOCK >= max(POOL_WINDOWS) and n_steps <= CHUNK and PAST_LEN % CHUNK == 0

    p = _stacked_params(n_steps, norm_mix_pre, norm_mix_post, norm_ffn_pre, norm_ffn_post, w_pool,
                        pool_scale, conv_w, conv_b, conv_ln_g, conv_ln_b, sgu_ln_g, sgu_ln_b, sgu_w,
                        sgu_b, short_w)
    yp = x_prompt
    ys = jnp.transpose(x_sample, (1, 0, 2))
    hist_major = lambda a: jnp.transpose(a, (0, 2, 1, 3))
    pool_in, conv_in, short_in = hist_major(state_pool), hist_major(state_conv), hist_major(state_short)
    prompt_states = [[] for _ in range(3)]
    sample_states = [[] for _ in range(4)]
    w_in_b, w_out_b = w_in[0].astype(BF16), w_out[0].astype(BF16)
    for l in range(depth):
        yp, pool_p, conv_p, short_p, w_up_b, w_down_b = _prompt_mixer(
            yp, p, w_in_b, w_out_b, w_ffn_up, w_ffn_down, l)
        ys, pool_s, conv_s, short_s, v_s = _sample_mixer(
            ys, pool_in, conv_in, short_in, p, w_in_b, w_out_b, l)
        cast = ((w_in, l + 1), (w_out, l + 1)) if l + 1 < depth else ()
        yp, ys, *w_next = _ffn(yp.reshape(bp * seq, d), ys.reshape(n_steps * n_seq, d), p, w_up_b,
                               w_down_b, l, cast)
        yp, ys = yp.reshape(bp, seq, d), ys.reshape(n_steps, n_seq, d)
        if w_next:
            w_in_b, w_out_b = w_next

        for lst, val in zip(prompt_states, (pool_p, conv_p, short_p)):
            lst.append(val)
        for lst, val in zip(sample_states, (pool_s, conv_s, short_s, v_s)):
            lst.append(val)

    ys = jnp.transpose(ys, (1, 0, 2))
    pool_p, conv_p, short_p = (jnp.stack(o) for o in prompt_states)
    pool_s, conv_s, short_s, v_s = (hist_major(jnp.stack(o)) for o in sample_states)
    return (yp, ys, pool_p, pool_s, conv_p, conv_s, short_p, short_s, v_s)
```

```python
import functools

import jax
import jax.numpy as jnp
import numpy as np
from jax import lax
from jax.experimental import pallas as pl
from jax.experimental.pallas import tpu as pltpu

D_MODEL = 1024
GROUP_WIDTH = 256
N_SUB = 4
SUB_DIM = 64
POOL_WINDOWS = (2, 4, 8, 16)
POOL_BUF = 15
CONV_WIDTH = 31
SHORT_WIDTH = 3
CHUNK = 128
D_FF = 4096
EPS = 1e-6
PAST_LEN = 16384
IN_WIDTH = 8 * GROUP_WIDTH

SUBLANES = 8
LANES = 128
N_SLABS = GROUP_WIDTH // LANES
POOL_HIST = 16
CONV_HIST = 32
SHORT_HIST = 8

ROW_BLOCK = CHUNK
MATMUL_ROWS = 512
PROMPT_TILE = 1024
SAMPLE_SEQS = 64
FFN_TILE = 1024
VMEM_LIMIT_BYTES = 56 * 1024 * 1024

F32 = jnp.float32
BF16 = jnp.bfloat16


def _rms_norm(x, g):
    ms = jnp.mean(x * x, axis=-1, keepdims=True)
    return x * lax.rsqrt(ms + EPS) * g


def _dot(a, b):
    return jnp.dot(a, b, preferred_element_type=F32)


def _head_mean(x, low_head):
    s_low = jnp.sum(jnp.where(low_head, x, 0.0), axis=-1, keepdims=True)
    s_high = jnp.sum(jnp.where(low_head, 0.0, x), axis=-1, keepdims=True)
    return jnp.where(low_head, s_low, s_high) * (1.0 / SUB_DIM)


def _head_layer_norm(x, g, b):
    low_head = lax.broadcasted_iota(jnp.int32, (x.shape[0], LANES), 1) < SUB_DIM
    out = []
    for s in range(N_SLABS):
        lanes = slice(s * LANES, (s + 1) * LANES)
        xs = x[:, lanes]
        xc = xs - _head_mean(xs, low_head)
        var = _head_mean(xc * xc, low_head)
        out.append(xc * lax.rsqrt(var + EPS) * g[:, lanes] + b[:, lanes])
    return jnp.concatenate(out, axis=1)


def _lane_group(shape):
    return jnp.right_shift(lax.broadcasted_iota(jnp.int32, shape, 1), SUB_DIM.bit_length() - 1)


def _pool_window(shape):
    grp = _lane_group(shape)
    w = jnp.full(shape, POOL_WINDOWS[0], jnp.int32)
    for gi in range(1, N_SUB):
        w = jnp.where(grp == gi, POOL_WINDOWS[gi], w)
    return grp, w


def _cols(group, slab):
    lo = group * GROUP_WIDTH + slab * LANES
    return slice(lo, lo + LANES)


def _trailing_sum(ext_ref, slab, start, n_rows, width):
    acc = ext_ref[slab, start:start + n_rows, :]
    for k in range(1, width):
        acc = acc + ext_ref[slab, start - k:start - k + n_rows, :]
    return acc


def _causal_taps(ext_ref, slab, w_ref, hist, n_taps, r):
    first = hist + r - (n_taps - 1)
    lanes = slice(slab * LANES, (slab + 1) * LANES)
    acc = None
    for k in range(n_taps):
        term = ext_ref[slab, first + k:first + k + ROW_BLOCK, :] * w_ref[k:k + 1, lanes]
        acc = term if acc is None else acc + term
    return acc


def _gate_block(r0, z_ref, y_ref, vn_ref, swt_ref, sbias_ref):
    gw = GROUP_WIDTH
    chunks = range(r0, r0 + MATMUL_ROWS, ROW_BLOCK)
    vt = [vn_ref[r:r + ROW_BLOCK, :].T for r in chunks]
    mixed = []
    for h in range(N_SUB):
        head = slice(h * SUB_DIM, (h + 1) * SUB_DIM)
        lhs = jnp.concatenate([v[head, :] for v in vt], axis=0)
        mixed.append(_dot(lhs, swt_ref[h]))
    for ci, r in enumerate(chunks):
        rows = slice(r, r + ROW_BLOCK)
        part = slice(ci * SUB_DIM, (ci + 1) * SUB_DIM)
        sg = jnp.concatenate([m[part, :] for m in mixed], axis=0).T + sbias_ref[...]
        y_ref[rows, 2 * gw:3 * gw] = (z_ref[rows, 3 * gw:4 * gw] * sg).astype(BF16)


def _mix_rows(r, seq_start, z_ref, y_ref, vn_ref, pool_ext, conv_ext, short_ext,
              pinv_first_ref, pinv_rest_ref, convw_ref, convb_ref, clng_ref, clnb_ref,
              slng_ref, slnb_ref, shortw_ref):
    rows = slice(r, r + ROW_BLOCK)
    gw = GROUP_WIDTH
    low_head = lax.broadcasted_iota(jnp.int32, (ROW_BLOCK, LANES), 1) < SUB_DIM

    d = []
    for s in range(N_SLABS):
        lanes = slice(s * LANES, (s + 1) * LANES)
        a = z_ref[rows, _cols(0, s)]
        base = POOL_HIST + r
        pool_ext[s, base:base + ROW_BLOCK, :] = a
        small, large = POOL_WINDOWS[2 * s], POOL_WINDOWS[2 * s + 1]
        assert large == 2 * small
        if small % SUBLANES == 0:
            run = _trailing_sum(pool_ext, s, base - small, ROW_BLOCK + small, small)
            s_small = run[small:]
            s_large = s_small + run[:ROW_BLOCK]
        else:
            s_small = _trailing_sum(pool_ext, s, base, ROW_BLOCK, small)
            s_large = s_small + _trailing_sum(pool_ext, s, base - small, ROW_BLOCK, small)
        win = jnp.where(low_head, s_small, s_large)
        inv = pinv_rest_ref[:, lanes]
        if seq_start is not None:
            inv = jnp.where(seq_start, pinv_first_ref[:, lanes], inv)
        d.append(win * inv - a)
    d = jnp.concatenate(d, axis=1).astype(BF16)
    y_ref[rows, 0:gw] = d

    c = []
    for s in range(N_SLABS):
        g = z_ref[rows, _cols(1, s)] * jax.nn.sigmoid(z_ref[rows, _cols(2, s)])
        conv_ext[s, CONV_HIST + r:CONV_HIST + r + ROW_BLOCK, :] = g
        c.append(_causal_taps(conv_ext, s, convw_ref, CONV_HIST, CONV_WIDTH, r))
    c = jnp.concatenate(c, axis=1) + convb_ref[...]
    c = _head_layer_norm(c, clng_ref[...], clnb_ref[...])
    y_ref[rows, gw:2 * gw] = (c * jax.nn.sigmoid(c)).astype(BF16)

    vn = _head_layer_norm(z_ref[rows, 4 * gw:5 * gw], slng_ref[...], slnb_ref[...])
    vn_ref[rows, :] = vn.astype(BF16)

    for s in range(N_SLABS):
        ch = z_ref[rows, _cols(6, s)] * z_ref[rows, _cols(7, s)]
        short_ext[s, SHORT_HIST + r:SHORT_HIST + r + ROW_BLOCK, :] = ch
        sc = _causal_taps(short_ext, s, shortw_ref, SHORT_HIST, SHORT_WIDTH, r)
        y_ref[rows, _cols(3, s)] = (z_ref[rows, _cols(5, s)] * sc).astype(BF16)


def _layer_views(layer, *refs):
    return [r.at[pl.ds(layer, 1)] if len(r.shape) == 2 else r.at[layer] for r in refs]


def _prompt_mixer_kernel(layer, x_ref, win_ref, wout_ref, pinv_first_ref, pinv_rest_ref,
                         gpre_ref, gpost_ref, wpool_ref, pscale_ref, convw_ref, convb_ref, clng_ref,
                         clnb_ref, slng_ref, slnb_ref, sguw_ref, sbias_ref, shortw_ref,
                         wup_f32_ref, wdown_f32_ref,
                         out_ref, pool_out_ref, conv_out_ref, short_out_ref, wup_ref, wdown_ref,
                         z_ref, y_ref, vn_ref, swt_ref, pool_ext, conv_ext, short_ext):
    (gpre_ref, gpost_ref, wpool_ref, pscale_ref, convw_ref, convb_ref, clng_ref, clnb_ref, slng_ref,
     slnb_ref, sguw_ref, sbias_ref, shortw_ref) = _layer_views(
         layer, gpre_ref, gpost_ref, wpool_ref, pscale_ref, convw_ref, convb_ref, clng_ref, clnb_ref,
         slng_ref, slnb_ref, sguw_ref, sbias_ref, shortw_ref)
    j = pl.program_id(1)
    tile = x_ref.shape[0]
    wup_ref[...] = wup_f32_ref[...].astype(BF16)
    wdown_ref[...] = wdown_f32_ref[...].astype(BF16)
    causal = (lax.broadcasted_iota(jnp.int32, (CHUNK, CHUNK), 0)
              >= lax.broadcasted_iota(jnp.int32, (CHUNK, CHUNK), 1))
    for hd in range(N_SUB):
        swt_ref[hd] = jnp.where(causal, sguw_ref[hd], 0.0).T.astype(BF16)

    @pl.when(j == 0)
    def _():
        pool_ext[:, 0:POOL_HIST, :] = jnp.zeros((N_SLABS, POOL_HIST, LANES), F32)
        conv_ext[:, 0:CONV_HIST, :] = jnp.zeros((N_SLABS, CONV_HIST, LANES), F32)
        short_ext[:, 0:SHORT_HIST, :] = jnp.zeros((N_SLABS, SHORT_HIST, LANES), F32)

    def project_out(r0):
        rows = slice(r0, r0 + MATMUL_ROWS)
        pooled = _dot(y_ref[rows, 0:GROUP_WIDTH], wpool_ref[...]) * pscale_ref[...]
        y_ref[rows, 0:GROUP_WIDTH] = pooled.astype(BF16)
        o = _dot(y_ref[rows, :], wout_ref[...])
        out_ref[rows, :] = x_ref[rows, :] + _rms_norm(o, gpost_ref[...])

    for r0 in range(0, tile, MATMUL_ROWS):
        rows = slice(r0, r0 + MATMUL_ROWS)
        h = _rms_norm(x_ref[rows, :], gpre_ref[...]).astype(BF16)
        z_ref[rows, :] = _dot(h, win_ref[...])
        for r in range(r0, r0 + MATMUL_ROWS, ROW_BLOCK):
            _mix_rows(r, (j == 0) if r == 0 else None, z_ref, y_ref, vn_ref, pool_ext, conv_ext,
                      short_ext, pinv_first_ref, pinv_rest_ref, convw_ref, convb_ref, clng_ref,
                      clnb_ref, slng_ref, slnb_ref, shortw_ref)
            if r == r0 and r0 > 0:
                _gate_block(r0 - MATMUL_ROWS, z_ref, y_ref, vn_ref, swt_ref, sbias_ref)
                project_out(r0 - MATMUL_ROWS)
    _gate_block(tile - MATMUL_ROWS, z_ref, y_ref, vn_ref, swt_ref, sbias_ref)
    project_out(tile - MATMUL_ROWS)

    pool_ext[:, 0:POOL_HIST, :] = pool_ext[:, tile:tile + POOL_HIST, :]
    conv_ext[:, 0:CONV_HIST, :] = conv_ext[:, tile:tile + CONV_HIST, :]
    short_ext[:, 0:SHORT_HIST, :] = short_ext[:, tile:tile + SHORT_HIST, :]

    @pl.when(j == pl.num_programs(1) - 1)
    def _():
        for s in range(N_SLABS):
            lanes = slice(s * LANES, (s + 1) * LANES)
            pool_out_ref[:, lanes] = pool_ext[s, POOL_HIST - POOL_BUF:POOL_HIST, :]
            conv_out_ref[:, lanes] = conv_ext[s, CONV_HIST - (CONV_WIDTH - 1):CONV_HIST, :]
            short_out_ref[:, lanes] = short_ext[s, SHORT_HIST - (SHORT_WIDTH - 1):SHORT_HIST, :]


def _sample_mixer_kernel(layer, x_ref, pool_in_ref, conv_in_ref, short_in_ref, win_ref, wout_ref,
                         gpre_ref, gpost_ref, wpool_ref, pscale_ref, convw_ref, convb_ref, clng_ref,
                         clnb_ref, slng_ref, slnb_ref, sgw_ref, sgb_ref, shortw_ref,
                         out_ref, pool_out_ref, conv_out_ref, short_out_ref, v_out_ref,
                         z_ref, y_ref):
    (gpre_ref, gpost_ref, wpool_ref, pscale_ref, convw_ref, convb_ref, clng_ref, clnb_ref, slng_ref,
     slnb_ref, sgw_ref, sgb_ref, shortw_ref) = _layer_views(
         layer, gpre_ref, gpost_ref, wpool_ref, pscale_ref, convw_ref, convb_ref, clng_ref, clnb_ref,
         slng_ref, slnb_ref, sgw_ref, sgb_ref, shortw_ref)
    n_steps, n_seq = v_out_ref.shape[0], v_out_ref.shape[1]
    gw = GROUP_WIDTH
    x = x_ref[...].reshape(n_steps * n_seq, x_ref.shape[2])
    h = _rms_norm(x, gpre_ref[...]).astype(BF16)
    z_ref[...] = _dot(h, win_ref[...])
    grp, w = _pool_window((n_seq, gw))
    cnt = jnp.minimum(w, PAST_LEN + 1).astype(F32)

    def slab(t):
        return slice(t * n_seq, (t + 1) * n_seq)

    a_new = [z_ref[slab(t), 0:gw] for t in range(n_steps)]
    g_new = [z_ref[slab(t), gw:2 * gw] * jax.nn.sigmoid(z_ref[slab(t), 2 * gw:3 * gw])
             for t in range(n_steps)]
    ch_new = [z_ref[slab(t), 6 * gw:7 * gw] * z_ref[slab(t), 7 * gw:8 * gw]
              for t in range(n_steps)]

    def pool_row(i):
        return pool_in_ref[i] if i < POOL_BUF else a_new[i - POOL_BUF]

    def conv_row(i):
        return conv_in_ref[i] if i < CONV_WIDTH - 1 else g_new[i - (CONV_WIDTH - 1)]

    def short_row(i):
        return short_in_ref[i] if i < SHORT_WIDTH - 1 else ch_new[i - (SHORT_WIDTH - 1)]

    vn = []
    for t in range(n_steps):
        rows = slab(t)
        end = POOL_BUF + t
        acc = pool_row(end)
        sums = []
        for k in range(1, max(POOL_WINDOWS)):
            acc = acc + pool_row(end - k)
            if k + 1 in POOL_WINDOWS:
                sums.append(acc)
        win = sums[0]
        for gi in range(1, N_SUB):
            win = jnp.where(grp == gi, sums[gi], win)
        d = win / cnt - a_new[t]
        y_ref[rows, 0:gw] = d.astype(BF16)

        c = None
        for k in range(CONV_WIDTH):
            term = conv_row(t + k) * convw_ref[k:k + 1, :]
            c = term if c is None else c + term
        c = _head_layer_norm(c + convb_ref[...], clng_ref[...], clnb_ref[...])
        y_ref[rows, gw:2 * gw] = (c * jax.nn.sigmoid(c)).astype(BF16)

        vn.append(_head_layer_norm(z_ref[rows, 4 * gw:5 * gw], slng_ref[...], slnb_ref[...]))
        v_out_ref[t] = vn[t]
        s = sgb_ref[t:t + 1, :]
        for u in range(t + 1):
            s = s + sgw_ref[t * n_steps + u:t * n_steps + u + 1, :] * vn[u]
        y_ref[rows, 2 * gw:3 * gw] = (z_ref[rows, 3 * gw:4 * gw] * s).astype(BF16)

        sc = None
        for k in range(SHORT_WIDTH):
            term = short_row(t + k) * shortw_ref[k:k + 1, :]
            sc = term if sc is None else sc + term
        y_ref[rows, 3 * gw:4 * gw] = (z_ref[rows, 5 * gw:6 * gw] * sc).astype(BF16)

    pooled = _dot(y_ref[:, 0:gw], wpool_ref[...]) * pscale_ref[...]
    y_ref[:, 0:gw] = pooled.astype(BF16)
    o = _dot(y_ref[...], wout_ref[...])
    out_ref[...] = (x + _rms_norm(o, gpost_ref[...])).reshape(out_ref.shape)

    for i in range(POOL_BUF):
        pool_out_ref[i] = pool_row(i + n_steps)
    for i in range(CONV_WIDTH - 1):
        conv_out_ref[i] = conv_row(i + n_steps)
    for i in range(SHORT_WIDTH - 1):
        short_out_ref[i] = short_row(i + n_steps)


def _mlp(x_ref, out_ref, gpre_ref, gpost_ref, wup_ref, wdown_ref):
    x = x_ref[...]
    f = _rms_norm(x, gpre_ref[...]).astype(BF16)
    u = jnp.maximum(_dot(f, wup_ref[...]), 0.0)
    o = _dot((u * u).astype(BF16), wdown_ref[...])
    out_ref[...] = x + _rms_norm(o, gpost_ref[...])


def _ffn_kernel(layer, n_cast, x_ref, xs_ref, gpre_ref, gpost_ref, wup_ref, wdown_ref, *rest):
    cast_in, out_ref, outs_ref = rest[:n_cast], rest[n_cast], rest[n_cast + 1]
    cast_out = rest[n_cast + 2:]
    gpre_ref, gpost_ref = _layer_views(layer, gpre_ref, gpost_ref)
    i = pl.program_id(0)
    last = pl.num_programs(0) - 1

    @pl.when(i < last)
    def _():
        for src, dst in zip(cast_in, cast_out):
            dst[...] = src[...].astype(BF16)
        _mlp(x_ref, out_ref, gpre_ref, gpost_ref, wup_ref, wdown_ref)

    @pl.when(i == last)
    def _():
        _mlp(xs_ref, outs_ref, gpre_ref, gpost_ref, wup_ref, wdown_ref)


def _const_spec(shape):
    nd = len(shape)
    return pl.BlockSpec(shape, lambda *_: (0,) * nd, pipeline_mode=pl.Buffered(1))


def _whole_spec(shape):
    nd = len(shape)
    return pl.BlockSpec(shape, lambda *_: (0,) * nd)


def _compiler_params(semantics):
    return pltpu.CompilerParams(dimension_semantics=semantics, vmem_limit_bytes=VMEM_LIMIT_BYTES)


def _prompt_mixer(x, p, w_in, w_out, w_ffn_up, w_ffn_down, layer):
    b, s, d = x.shape
    tile = PROMPT_TILE
    consts = (w_in, w_out, POOL_INV_FIRST, POOL_INV_REST,
              p["gpre"], p["gpost"], p["wpool"], p["pscale"], p["conv_w"], p["conv_b"], p["cln_g"],
              p["cln_b"], p["sln_g"], p["sln_b"], p["sgu_w"], p["sgu_bias"], p["short_w"])
    gw = GROUP_WIDTH
    tiles_per_seq = s // tile
    steps = b * tiles_per_seq
    d_ff = w_ffn_up.shape[2]
    ff_slice = d_ff // steps
    assert ff_slice * steps == d_ff and ff_slice % LANES == 0
    step = lambda bi, j: bi * tiles_per_seq + j
    state_spec = lambda n: pl.BlockSpec((None, n, gw), lambda bi, j: (bi, 0, 0))
    return pl.pallas_call(
        functools.partial(_prompt_mixer_kernel, layer),
        grid=(b, tiles_per_seq),
        in_specs=[pl.BlockSpec((None, tile, d), lambda bi, j: (bi, j, 0))]
        + [_const_spec(c.shape) for c in consts]
        + [pl.BlockSpec((None, d, ff_slice), lambda bi, j: (layer, 0, step(bi, j))),
           pl.BlockSpec((None, ff_slice, d), lambda bi, j: (layer, step(bi, j), 0))],
        out_specs=[pl.BlockSpec((None, tile, d), lambda bi, j: (bi, j, 0)),
                   state_spec(POOL_BUF), state_spec(CONV_WIDTH - 1), state_spec(SHORT_WIDTH - 1),
                   pl.BlockSpec((d, ff_slice), lambda bi, j: (0, step(bi, j))),
                   pl.BlockSpec((ff_slice, d), lambda bi, j: (step(bi, j), 0))],
        out_shape=[jax.ShapeDtypeStruct((b, s, d), F32),
                   jax.ShapeDtypeStruct((b, POOL_BUF, gw), F32),
                   jax.ShapeDtypeStruct((b, CONV_WIDTH - 1, gw), F32),
                   jax.ShapeDtypeStruct((b, SHORT_WIDTH - 1, gw), F32),
                   jax.ShapeDtypeStruct((d, d_ff), BF16),
                   jax.ShapeDtypeStruct((d_ff, d), BF16)],
        scratch_shapes=[pltpu.VMEM((tile, IN_WIDTH), F32),
                        pltpu.VMEM((tile, d), BF16),
                        pltpu.VMEM((tile, gw), BF16),
                        pltpu.VMEM((N_SUB, CHUNK, CHUNK), BF16),
                        pltpu.VMEM((N_SLABS, POOL_HIST + tile, LANES), F32),
                        pltpu.VMEM((N_SLABS, CONV_HIST + tile, LANES), F32),
                        pltpu.VMEM((N_SLABS, SHORT_HIST + tile, LANES), F32)],
        compiler_params=_compiler_params(("arbitrary", "arbitrary")),
        name="prompt_mixer",
    )(x, *consts, w_ffn_up, w_ffn_down)


def _sample_mixer(x, pool_st, conv_st, short_st, p, w_in, w_out, layer):
    n_steps, n_seq, d = x.shape
    group = SAMPLE_SEQS
    assert n_seq % group == 0 and group % SUBLANES == 0
    gw = GROUP_WIDTH
    states = (pool_st, conv_st, short_st)
    consts = (w_in, w_out, p["gpre"], p["gpost"], p["wpool"], p["pscale"], p["conv_w"], p["conv_b"],
              p["cln_g"], p["cln_b"], p["sln_g"], p["sln_b"], p["sgu_w4"], p["sgu_bias"], p["short_w"])
    group_spec = lambda a: pl.BlockSpec((a.shape[0], group, a.shape[2]), lambda i: (0, i, 0))
    state_spec = lambda st: pl.BlockSpec((None, st.shape[1], group, gw), lambda i: (layer, 0, i, 0))
    out_shape = ([jax.ShapeDtypeStruct((n_steps, n_seq, d), F32)]
                 + [jax.ShapeDtypeStruct(st.shape[1:], F32) for st in states]
                 + [jax.ShapeDtypeStruct((n_steps, n_seq, gw), F32)])
    rows = n_steps * group
    return pl.pallas_call(
        functools.partial(_sample_mixer_kernel, layer),
        grid=(n_seq // group,),
        in_specs=[group_spec(x)] + [state_spec(st) for st in states]
        + [_const_spec(c.shape) for c in consts],
        out_specs=[group_spec(o) for o in out_shape],
        out_shape=out_shape,
        scratch_shapes=[pltpu.VMEM((rows, IN_WIDTH), F32), pltpu.VMEM((rows, d), BF16)],
        compiler_params=_compiler_params(("arbitrary",)),
        name="sample_mixer",
    )(x, *states, *consts)


def _ffn(x, xs, p, w_up, w_down, layer, cast=()):
    n, d = x.shape
    ns = xs.shape[0]
    tile = FFN_TILE
    steps = n // tile
    assert steps * tile == n and ns <= tile
    consts = (p["fpre"], p["fpost"], w_up, w_down)
    prompt_tile = lambda i: (jnp.minimum(i, steps - 1), 0)
    cast_in_specs, cast_out_specs, cast_shapes = [], [], []
    for w, w_layer in cast:
        rows = w.shape[1] // steps
        assert rows * steps == w.shape[1] and rows % (2 * SUBLANES) == 0
        cast_in_specs.append(pl.BlockSpec(
            (None, rows, w.shape[2]), lambda i, w_layer=w_layer: (w_layer,) + prompt_tile(i)))
        cast_out_specs.append(pl.BlockSpec((rows, w.shape[2]), prompt_tile))
        cast_shapes.append(jax.ShapeDtypeStruct(w.shape[1:], BF16))
    return pl.pallas_call(
        functools.partial(_ffn_kernel, layer, len(cast)),
        grid=(steps + 1,),
        in_specs=[pl.BlockSpec((tile, d), prompt_tile), _whole_spec(xs.shape)]
        + [_const_spec(c.shape) for c in consts] + cast_in_specs,
        out_specs=[pl.BlockSpec((tile, d), prompt_tile), _whole_spec(xs.shape)] + cast_out_specs,
        out_shape=[jax.ShapeDtypeStruct((n, d), F32), jax.ShapeDtypeStruct((ns, d), F32)] + cast_shapes,
        compiler_params=_compiler_params(("arbitrary",)),
        name="ffn",
    )(x, xs, *consts, *[w for w, _ in cast])


def _pool_inverse_counts():
    window = np.repeat(np.asarray(POOL_WINDOWS, np.float32), SUB_DIM)[None, :]
    first = 1.0 / np.minimum(window, np.arange(1, ROW_BLOCK + 1, dtype=np.float32)[:, None])
    return first.astype(np.float32), (1.0 / window).astype(np.float32)


POOL_INV_FIRST, POOL_INV_REST = _pool_inverse_counts()


def _stacked_params(n_steps, norm_mix_pre, norm_mix_post, norm_ffn_pre, norm_ffn_post, w_pool,
                    pool_scale, conv_w, conv_b, conv_ln_g, conv_ln_b, sgu_ln_g, sgu_ln_b, sgu_w,
                    sgu_b, short_w):
    depth = w_pool.shape[0]
    gw = GROUP_WIDTH
    head_of_lane = np.arange(gw) // SUB_DIM
    same_head = head_of_lane[:, None] == head_of_lane[None, :]
    wp = jnp.tile(w_pool.reshape(depth, gw, SUB_DIM), (1, 1, N_SUB))
    wpool = jnp.where(same_head[None], wp, 0.0).astype(BF16)
    bias = jnp.repeat(jnp.swapaxes(sgu_b, 1, 2), SUB_DIM, axis=2)
    causal = np.tril(np.ones((n_steps, n_steps), dtype=bool))
    w4 = jnp.where(causal[None, None], sgu_w[:, :, :n_steps, :n_steps], 0.0)
    w4 = jnp.repeat(jnp.transpose(w4, (0, 2, 3, 1)), SUB_DIM, axis=3)
    w4 = w4.reshape(depth, n_steps * n_steps, gw)
    return dict(gpre=norm_mix_pre, gpost=norm_mix_post, fpre=norm_ffn_pre, fpost=norm_ffn_post,
                wpool=wpool, pscale=pool_scale, conv_w=conv_w, conv_b=conv_b, cln_g=conv_ln_g,
                cln_b=conv_ln_b, sln_g=sgu_ln_g, sln_b=sgu_ln_b, sgu_w=sgu_w, sgu_bias=bias,
                sgu_w4=w4, short_w=short_w)


def kernel(x_prompt, x_sample, state_pool, state_conv, state_short, norm_mix_pre, norm_mix_post, norm_ffn_pre, norm_ffn_post, w_in, w_out, w_pool, pool_scale, conv_w, conv_b, conv_ln_g, conv_ln_b, sgu_ln_g, sgu_ln_b, sgu_w, sgu_b, short_w, w_ffn_up, w_ffn_down):
    depth = w_in.shape[0]
    bp, seq, d = x_prompt.shape
    n_seq, n_steps, _ = x_sample.shape
    assert seq % PROMPT_TILE == 0 and PROMPT_TILE % MATMUL_ROWS == 0 and MATMUL_ROWS % ROW_BLOCK == 0
    assert ROW_BLOCK >= max(POOL_WINDOWS) and n_steps <= CHUNK and PAST_LEN % CHUNK == 0

    p = _stacked_params(n_steps, norm_mix_pre, norm_mix_post, norm_ffn_pre, norm_ffn_post, w_pool,
                        pool_scale, conv_w, conv_b, conv_ln_g, conv_ln_b, sgu_ln_g, sgu_ln_b, sgu_w,
                        sgu_b, short_w)
    yp = x_prompt
    ys = jnp.transpose(x_sample, (1, 0, 2))
    hist_major = lambda a: jnp.transpose(a, (0, 2, 1, 3))
    pool_in, conv_in, short_in = hist_major(state_pool), hist_major(state_conv), hist_major(state_short)
    prompt_states = [[] for _ in range(3)]
    sample_states = [[] for _ in range(4)]
    w_in_b, w_out_b = w_in[0].astype(BF16), w_out[0].astype(BF16)
    for l in range(depth):
        yp, pool_p, conv_p, short_p, w_up_b, w_down_b = _prompt_mixer(
            yp, p, w_in_b, w_out_b, w_ffn_up, w_ffn_down, l)
        ys, pool_s, conv_s, short_s, v_s = _sample_mixer(
            ys, pool_in, conv_in, short_in, p, w_in_b, w_out_b, l)
        cast = ((w_in, l + 1), (w_out, l + 1)) if l + 1 < depth else ()
        yp, ys, *w_next = _ffn(yp.reshape(bp * seq, d), ys.reshape(n_steps * n_seq, d), p, w_up_b,
                               w_down_b, l, cast)
        yp, ys = yp.reshape(bp, seq, d), ys.reshape(n_steps, n_seq, d)
        if w_next:
            w_in_b, w_out_b = w_next

        for lst, val in zip(prompt_states, (pool_p, conv_p, short_p)):
            lst.append(val)
        for lst, val in zip(sample_states, (pool_s, conv_s, short_s, v_s)):
            lst.append(val)

    ys = jnp.transpose(ys, (1, 0, 2))
    pool_p, conv_p, short_p = (jnp.stack(o) for o in prompt_states)
    pool_s, conv_s, short_s, v_s = (hist_major(jnp.stack(o)) for o in sample_states)
    return (yp, ys, pool_p, pool_s, conv_p, conv_s, short_p, short_s, v_s)
```

```python
import functools

import jax
import jax.numpy as jnp
import numpy as np
from jax import lax
from jax.experimental import pallas as pl
from jax.experimental.pallas import tpu as pltpu

D_MODEL = 1024
GROUP_WIDTH = 256
N_SUB = 4
SUB_DIM = 64
POOL_WINDOWS = (2, 4, 8, 16)
POOL_BUF = 15
CONV_WIDTH = 31
SHORT_WIDTH = 3
CHUNK = 128
D_FF = 4096
EPS = 1e-6
PAST_LEN = 16384
IN_WIDTH = 8 * GROUP_WIDTH

SUBLANES = 8
LANES = 128
N_SLABS = GROUP_WIDTH // LANES
POOL_HIST = 16
CONV_HIST = 32
SHORT_HIST = 8

ROW_BLOCK = CHUNK
MATMUL_ROWS = 512
PROMPT_TILE = 1024
SAMPLE_SEQS = 64
FFN_TILE = 1024
VMEM_LIMIT_BYTES = 56 * 1024 * 1024

F32 = jnp.float32
BF16 = jnp.bfloat16


def _rms_norm(x, g):
    ms = jnp.mean(x * x, axis=-1, keepdims=True)
    return x * lax.rsqrt(ms + EPS) * g


def _dot(a, b):
    return jnp.dot(a, b, preferred_element_type=F32)


def _head_mean(x, low_head):
    s_low = jnp.sum(jnp.where(low_head, x, 0.0), axis=-1, keepdims=True)
    s_high = jnp.sum(jnp.where(low_head, 0.0, x), axis=-1, keepdims=True)
    return jnp.where(low_head, s_low, s_high) * (1.0 / SUB_DIM)


def _head_layer_norm(x, g, b):
    low_head = lax.broadcasted_iota(jnp.int32, (x.shape[0], LANES), 1) < SUB_DIM
    out = []
    for s in range(N_SLABS):
        lanes = slice(s * LANES, (s + 1) * LANES)
        xs = x[:, lanes]
        xc = xs - _head_mean(xs, low_head)
        var = _head_mean(xc * xc, low_head)
        out.append(xc * lax.rsqrt(var + EPS) * g[:, lanes] + b[:, lanes])
    return jnp.concatenate(out, axis=1)


def _lane_group(shape):
    return jnp.right_shift(lax.broadcasted_iota(jnp.int32, shape, 1), SUB_DIM.bit_length() - 1)


def _pool_window(shape):
    grp = _lane_group(shape)
    w = jnp.full(shape, POOL_WINDOWS[0], jnp.int32)
    for gi in range(1, N_SUB):
        w = jnp.where(grp == gi, POOL_WINDOWS[gi], w)
    return grp, w


def _cols(group, slab):
    lo = group * GROUP_WIDTH + slab * LANES
    return slice(lo, lo + LANES)


def _trailing_sum(ext_ref, slab, start, n_rows, width):
    acc = ext_ref[slab, start:start + n_rows, :]
    for k in range(1, width):
        acc = acc + ext_ref[slab, start - k:start - k + n_rows, :]
    return acc


def _causal_taps(ext_ref, slab, w_ref, hist, n_taps, r):
    first = hist + r - (n_taps - 1)
    lanes = slice(slab * LANES, (slab + 1) * LANES)
    acc = None
    for k in range(n_taps):
        term = ext_ref[slab, first + k:first + k + ROW_BLOCK, :] * w_ref[k:k + 1, lanes]
        acc = term if acc is None else acc + term
    return acc


def _gate_block(r0, z_ref, y_ref, vn_ref, swt_ref, sbias_ref):
    gw = GROUP_WIDTH
    chunks = range(r0, r0 + MATMUL_ROWS, ROW_BLOCK)
    vt = [vn_ref[r:r + ROW_BLOCK, :].T for r in chunks]
    mixed = []
    for h in range(N_SUB):
        head = slice(h * SUB_DIM, (h + 1) * SUB_DIM)
        lhs = jnp.concatenate([v[head, :] for v in vt], axis=0)
        mixed.append(_dot(lhs, swt_ref[h]))
    for ci, r in enumerate(chunks):
        rows = slice(r, r + ROW_BLOCK)
        part = slice(ci * SUB_DIM, (ci + 1) * SUB_DIM)
        sg = jnp.concatenate([m[part, :] for m in mixed], axis=0).T + sbias_ref[...]
        y_ref[rows, 2 * gw:3 * gw] = (z_ref[rows, 3 * gw:4 * gw] * sg).astype(BF16)


def _mix_rows(r, seq_start, z_ref, y_ref, vn_ref, pool_ext, conv_ext, short_ext,
              pinv_first_ref, pinv_rest_ref, convw_ref, convb_ref, clng_ref, clnb_ref,
              slng_ref, slnb_ref, shortw_ref):
    rows = slice(r, r + ROW_BLOCK)
    gw = GROUP_WIDTH
    low_head = lax.broadcasted_iota(jnp.int32, (ROW_BLOCK, LANES), 1) < SUB_DIM

    d = []
    for s in range(N_SLABS):
        lanes = slice(s * LANES, (s + 1) * LANES)
        a = z_ref[rows, _cols(0, s)]
        base = POOL_HIST + r
        pool_ext[s, base:base + ROW_BLOCK, :] = a
        small, large = POOL_WINDOWS[2 * s], POOL_WINDOWS[2 * s + 1]
        assert large == 2 * small
        if small % SUBLANES == 0:
            run = _trailing_sum(pool_ext, s, base - small, ROW_BLOCK + small, small)
            s_small = run[small:]
            s_large = s_small + run[:ROW_BLOCK]
        else:
            s_small = _trailing_sum(pool_ext, s, base, ROW_BLOCK, small)
            s_large = s_small + _trailing_sum(pool_ext, s, base - small, ROW_BLOCK, small)
        win = jnp.where(low_head, s_small, s_large)
        inv = pinv_rest_ref[:, lanes]
        if seq_start is not None:
            inv = jnp.where(seq_start, pinv_first_ref[:, lanes], inv)
        d.append(win * inv - a)
    d = jnp.concatenate(d, axis=1).astype(BF16)
    y_ref[rows, 0:gw] = d

    c = []
    for s in range(N_SLABS):
        g = z_ref[rows, _cols(1, s)] * jax.nn.sigmoid(z_ref[rows, _cols(2, s)])
        conv_ext[s, CONV_HIST + r:CONV_HIST + r + ROW_BLOCK, :] = g
        c.append(_causal_taps(conv_ext, s, convw_ref, CONV_HIST, CONV_WIDTH, r))
    c = jnp.concatenate(c, axis=1) + convb_ref[...]
    c = _head_layer_norm(c, clng_ref[...], clnb_ref[...])
    y_ref[rows, gw:2 * gw] = (c * jax.nn.sigmoid(c)).astype(BF16)

    vn = _head_layer_norm(z_ref[rows, 4 * gw:5 * gw], slng_ref[...], slnb_ref[...])
    vn_ref[rows, :] = vn.astype(BF16)

    for s in range(N_SLABS):
        ch = z_ref[rows, _cols(6, s)] * z_ref[rows, _cols(7, s)]
        short_ext[s, SHORT_HIST + r:SHORT_HIST + r + ROW_BLOCK, :] = ch
        sc = _causal_taps(short_ext, s, shortw_ref, SHORT_HIST, SHORT_WIDTH, r)
        y_ref[rows, _cols(3, s)] = (z_ref[rows, _cols(5, s)] * sc).astype(BF16)


def _layer_views(layer, *refs):
    return [r.at[pl.ds(layer, 1)] if len(r.shape) == 2 else r.at[layer] for r in refs]


def _prompt_mixer_kernel(layer, x_ref, win_ref, wout_ref, pinv_first_ref, pinv_rest_ref,
                         gpre_ref, gpost_ref, wpool_ref, pscale_ref, convw_ref, convb_ref, clng_ref,
                         clnb_ref, slng_ref, slnb_ref, sguw_ref, sbias_ref, shortw_ref,
                         wup_f32_ref, wdown_f32_ref,
                         out_ref, pool_out_ref, conv_out_ref, short_out_ref, wup_ref, wdown_ref,
                         z_ref, y_ref, vn_ref, swt_ref, pool_ext, conv_ext, short_ext):
    (gpre_ref, gpost_ref, wpool_ref, pscale_ref, convw_ref, convb_ref, clng_ref, clnb_ref, slng_ref,
     slnb_ref, sguw_ref, sbias_ref, shortw_ref) = _layer_views(
         layer, gpre_ref, gpost_ref, wpool_ref, pscale_ref, convw_ref, convb_ref, clng_ref, clnb_ref,
         slng_ref, slnb_ref, sguw_ref, sbias_ref, shortw_ref)
    j = pl.program_id(1)
    tile = x_ref.shape[0]
    wup_ref[...] = wup_f32_ref[...].astype(BF16)
    wdown_ref[...] = wdown_f32_ref[...].astype(BF16)
    @pl.when((pl.program_id(0) == 0) & (j == 0))
    def _():
        causal = (lax.broadcasted_iota(jnp.int32, (CHUNK, CHUNK), 0)
                  >= lax.broadcasted_iota(jnp.int32, (CHUNK, CHUNK), 1))
        for hd in range(N_SUB):
            swt_ref[hd] = jnp.where(causal, sguw_ref[hd], 0.0).T.astype(BF16)

    @pl.when(j == 0)
    def _():
        pool_ext[:, 0:POOL_HIST, :] = jnp.zeros((N_SLABS, POOL_HIST, LANES), F32)
        conv_ext[:, 0:CONV_HIST, :] = jnp.zeros((N_SLABS, CONV_HIST, LANES), F32)
        short_ext[:, 0:SHORT_HIST, :] = jnp.zeros((N_SLABS, SHORT_HIST, LANES), F32)

    def project_out(r0):
        rows = slice(r0, r0 + MATMUL_ROWS)
        pooled = _dot(y_ref[rows, 0:GROUP_WIDTH], wpool_ref[...]) * pscale_ref[...]
        y_ref[rows, 0:GROUP_WIDTH] = pooled.astype(BF16)
        o = _dot(y_ref[rows, :], wout_ref[...])
        out_ref[rows, :] = x_ref[rows, :] + _rms_norm(o, gpost_ref[...])

    for r0 in range(0, tile, MATMUL_ROWS):
        rows = slice(r0, r0 + MATMUL_ROWS)
        h = _rms_norm(x_ref[rows, :], gpre_ref[...]).astype(BF16)
        z_ref[rows, :] = _dot(h, win_ref[...])
        for r in range(r0, r0 + MATMUL_ROWS, ROW_BLOCK):
            _mix_rows(r, (j == 0) if r == 0 else None, z_ref, y_ref, vn_ref, pool_ext, conv_ext,
                      short_ext, pinv_first_ref, pinv_rest_ref, convw_ref, convb_ref, clng_ref,
                      clnb_ref, slng_ref, slnb_ref, shortw_ref)
            if r == r0 and r0 > 0:
                _gate_block(r0 - MATMUL_ROWS, z_ref, y_ref, vn_ref, swt_ref, sbias_ref)
                project_out(r0 - MATMUL_ROWS)
    _gate_block(tile - MATMUL_ROWS, z_ref, y_ref, vn_ref, swt_ref, sbias_ref)
    project_out(tile - MATMUL_ROWS)

    pool_ext[:, 0:POOL_HIST, :] = pool_ext[:, tile:tile + POOL_HIST, :]
    conv_ext[:, 0:CONV_HIST, :] = conv_ext[:, tile:tile + CONV_HIST, :]
    short_ext[:, 0:SHORT_HIST, :] = short_ext[:, tile:tile + SHORT_HIST, :]

    @pl.when(j == pl.num_programs(1) - 1)
    def _():
        for s in range(N_SLABS):
            lanes = slice(s * LANES, (s + 1) * LANES)
            pool_out_ref[:, lanes] = pool_ext[s, POOL_HIST - POOL_BUF:POOL_HIST, :]
            conv_out_ref[:, lanes] = conv_ext[s, CONV_HIST - (CONV_WIDTH - 1):CONV_HIST, :]
            short_out_ref[:, lanes] = short_ext[s, SHORT_HIST - (SHORT_WIDTH - 1):SHORT_HIST, :]


def _sample_mixer_kernel(layer, x_ref, pool_in_ref, conv_in_ref, short_in_ref, win_ref, wout_ref,
                         gpre_ref, gpost_ref, wpool_ref, pscale_ref, convw_ref, convb_ref, clng_ref,
                         clnb_ref, slng_ref, slnb_ref, sgw_ref, sgb_ref, shortw_ref,
                         out_ref, pool_out_ref, conv_out_ref, short_out_ref, v_out_ref,
                         z_ref, y_ref):
    (gpre_ref, gpost_ref, wpool_ref, pscale_ref, convw_ref, convb_ref, clng_ref, clnb_ref, slng_ref,
     slnb_ref, sgw_ref, sgb_ref, shortw_ref) = _layer_views(
         layer, gpre_ref, gpost_ref, wpool_ref, pscale_ref, convw_ref, convb_ref, clng_ref, clnb_ref,
         slng_ref, slnb_ref, sgw_ref, sgb_ref, shortw_ref)
    n_steps, n_seq = v_out_ref.shape[0], v_out_ref.shape[1]
    gw = GROUP_WIDTH
    x = x_ref[...].reshape(n_steps * n_seq, x_ref.shape[2])
    h = _rms_norm(x, gpre_ref[...]).astype(BF16)
    z_ref[...] = _dot(h, win_ref[...])
    grp, w = _pool_window((n_seq, gw))
    cnt = jnp.minimum(w, PAST_LEN + 1).astype(F32)

    def slab(t):
        return slice(t * n_seq, (t + 1) * n_seq)

    a_new = [z_ref[slab(t), 0:gw] for t in range(n_steps)]
    g_new = [z_ref[slab(t), gw:2 * gw] * jax.nn.sigmoid(z_ref[slab(t), 2 * gw:3 * gw])
             for t in range(n_steps)]
    ch_new = [z_ref[slab(t), 6 * gw:7 * gw] * z_ref[slab(t), 7 * gw:8 * gw]
              for t in range(n_steps)]

    def pool_row(i):
        return pool_in_ref[i] if i < POOL_BUF else a_new[i - POOL_BUF]

    def conv_row(i):
        return conv_in_ref[i] if i < CONV_WIDTH - 1 else g_new[i - (CONV_WIDTH - 1)]

    def short_row(i):
        return short_in_ref[i] if i < SHORT_WIDTH - 1 else ch_new[i - (SHORT_WIDTH - 1)]

    vn = []
    for t in range(n_steps):
        rows = slab(t)
        end = POOL_BUF + t
        acc = pool_row(end)
        sums = []
        for k in range(1, max(POOL_WINDOWS)):
            acc = acc + pool_row(end - k)
            if k + 1 in POOL_WINDOWS:
                sums.append(acc)
        win = sums[0]
        for gi in range(1, N_SUB):
            win = jnp.where(grp == gi, sums[gi], win)
        d = win / cnt - a_new[t]
        y_ref[rows, 0:gw] = d.astype(BF16)

        c = None
        for k in range(CONV_WIDTH):
            term = conv_row(t + k) * convw_ref[k:k + 1, :]
            c = term if c is None else c + term
        c = _head_layer_norm(c + convb_ref[...], clng_ref[...], clnb_ref[...])
        y_ref[rows, gw:2 * gw] = (c * jax.nn.sigmoid(c)).astype(BF16)

        vn.append(_head_layer_norm(z_ref[rows, 4 * gw:5 * gw], slng_ref[...], slnb_ref[...]))
        v_out_ref[t] = vn[t]
        s = sgb_ref[t:t + 1, :]
        for u in range(t + 1):
            s = s + sgw_ref[t * n_steps + u:t * n_steps + u + 1, :] * vn[u]
        y_ref[rows, 2 * gw:3 * gw] = (z_ref[rows, 3 * gw:4 * gw] * s).astype(BF16)

        sc = None
        for k in range(SHORT_WIDTH):
            term = short_row(t + k) * shortw_ref[k:k + 1, :]
            sc = term if sc is None else sc + term
        y_ref[rows, 3 * gw:4 * gw] = (z_ref[rows, 5 * gw:6 * gw] * sc).astype(BF16)

    pooled = _dot(y_ref[:, 0:gw], wpool_ref[...]) * pscale_ref[...]
    y_ref[:, 0:gw] = pooled.astype(BF16)
    o = _dot(y_ref[...], wout_ref[...])
    out_ref[...] = (x + _rms_norm(o, gpost_ref[...])).reshape(out_ref.shape)

    for i in range(POOL_BUF):
        pool_out_ref[i] = pool_row(i + n_steps)
    for i in range(CONV_WIDTH - 1):
        conv_out_ref[i] = conv_row(i + n_steps)
    for i in range(SHORT_WIDTH - 1):
        short_out_ref[i] = short_row(i + n_steps)


def _mlp(x_ref, out_ref, gpre_ref, gpost_ref, wup_ref, wdown_ref):
    x = x_ref[...]
    f = _rms_norm(x, gpre_ref[...]).astype(BF16)
    u = jnp.maximum(_dot(f, wup_ref[...]), 0.0)
    o = _dot((u * u).astype(BF16), wdown_ref[...])
    out_ref[...] = x + _rms_norm(o, gpost_ref[...])


def _ffn_kernel(layer, n_cast, x_ref, xs_ref, gpre_ref, gpost_ref, wup_ref, wdown_ref, *rest):
    cast_in, out_ref, outs_ref = rest[:n_cast], rest[n_cast], rest[n_cast + 1]
    cast_out = rest[n_cast + 2:]
    gpre_ref, gpost_ref = _layer_views(layer, gpre_ref, gpost_ref)
    i = pl.program_id(0)
    last = pl.num_programs(0) - 1

    @pl.when(i < last)
    def _():
        for src, dst in zip(cast_in, cast_out):
            dst[...] = src[...].astype(BF16)
        _mlp(x_ref, out_ref, gpre_ref, gpost_ref, wup_ref, wdown_ref)

    @pl.when(i == last)
    def _():
        _mlp(xs_ref, outs_ref, gpre_ref, gpost_ref, wup_ref, wdown_ref)


def _const_spec(shape):
    nd = len(shape)
    return pl.BlockSpec(shape, lambda *_: (0,) * nd, pipeline_mode=pl.Buffered(1))


def _whole_spec(shape):
    nd = len(shape)
    return pl.BlockSpec(shape, lambda *_: (0,) * nd)


def _compiler_params(semantics):
    return pltpu.CompilerParams(dimension_semantics=semantics, vmem_limit_bytes=VMEM_LIMIT_BYTES)


def _prompt_mixer(x, p, w_in, w_out, w_ffn_up, w_ffn_down, layer):
    b, s, d = x.shape
    tile = PROMPT_TILE
    consts = (w_in, w_out, POOL_INV_FIRST, POOL_INV_REST,
              p["gpre"], p["gpost"], p["wpool"], p["pscale"], p["conv_w"], p["conv_b"], p["cln_g"],
              p["cln_b"], p["sln_g"], p["sln_b"], p["sgu_w"], p["sgu_bias"], p["short_w"])
    gw = GROUP_WIDTH
    tiles_per_seq = s // tile
    steps = b * tiles_per_seq
    d_ff = w_ffn_up.shape[2]
    ff_slice = d_ff // steps
    assert ff_slice * steps == d_ff and ff_slice % LANES == 0
    step = lambda bi, j: bi * tiles_per_seq + j
    state_spec = lambda n: pl.BlockSpec((None, n, gw), lambda bi, j: (bi, 0, 0))
    return pl.pallas_call(
        functools.partial(_prompt_mixer_kernel, layer),
        grid=(b, tiles_per_seq),
        in_specs=[pl.BlockSpec((None, tile, d), lambda bi, j: (bi, j, 0))]
        + [_const_spec(c.shape) for c in consts]
        + [pl.BlockSpec((None, d, ff_slice), lambda bi, j: (layer, 0, step(bi, j))),
           pl.BlockSpec((None, ff_slice, d), lambda bi, j: (layer, step(bi, j), 0))],
        out_specs=[pl.BlockSpec((None, tile, d), lambda bi, j: (bi, j, 0)),
                   state_spec(POOL_BUF), state_spec(CONV_WIDTH - 1), state_spec(SHORT_WIDTH - 1),
                   pl.BlockSpec((d, ff_slice), lambda bi, j: (0, step(bi, j))),
                   pl.BlockSpec((ff_slice, d), lambda bi, j: (step(bi, j), 0))],
        out_shape=[jax.ShapeDtypeStruct((b, s, d), F32),
                   jax.ShapeDtypeStruct((b, POOL_BUF, gw), F32),
                   jax.ShapeDtypeStruct((b, CONV_WIDTH - 1, gw), F32),
                   jax.ShapeDtypeStruct((b, SHORT_WIDTH - 1, gw), F32),
                   jax.ShapeDtypeStruct((d, d_ff), BF16),
                   jax.ShapeDtypeStruct((d_ff, d), BF16)],
        scratch_shapes=[pltpu.VMEM((tile, IN_WIDTH), F32),
                        pltpu.VMEM((tile, d), BF16),
                        pltpu.VMEM((tile, gw), BF16),
                        pltpu.VMEM((N_SUB, CHUNK, CHUNK), BF16),
                        pltpu.VMEM((N_SLABS, POOL_HIST + tile, LANES), F32),
                        pltpu.VMEM((N_SLABS, CONV_HIST + tile, LANES), F32),
                        pltpu.VMEM((N_SLABS, SHORT_HIST + tile, LANES), F32)],
        compiler_params=_compiler_params(("arbitrary", "arbitrary")),
        name="prompt_mixer",
    )(x, *consts, w_ffn_up, w_ffn_down)


def _sample_mixer(x, pool_st, conv_st, short_st, p, w_in, w_out, layer):
    n_steps, n_seq, d = x.shape
    group = SAMPLE_SEQS
    assert n_seq % group == 0 and group % SUBLANES == 0
    gw = GROUP_WIDTH
    states = (pool_st, conv_st, short_st)
    consts = (w_in, w_out, p["gpre"], p["gpost"], p["wpool"], p["pscale"], p["conv_w"], p["conv_b"],
              p["cln_g"], p["cln_b"], p["sln_g"], p["sln_b"], p["sgu_w4"], p["sgu_bias"], p["short_w"])
    group_spec = lambda a: pl.BlockSpec((a.shape[0], group, a.shape[2]), lambda i: (0, i, 0))
    state_spec = lambda st: pl.BlockSpec((None, st.shape[1], group, gw), lambda i: (layer, 0, i, 0))
    out_shape = ([jax.ShapeDtypeStruct((n_steps, n_seq, d), F32)]
                 + [jax.ShapeDtypeStruct(st.shape[1:], F32) for st in states]
                 + [jax.ShapeDtypeStruct((n_steps, n_seq, gw), F32)])
    rows = n_steps * group
    return pl.pallas_call(
        functools.partial(_sample_mixer_kernel, layer),
        grid=(n_seq // group,),
        in_specs=[group_spec(x)] + [state_spec(st) for st in states]
        + [_const_spec(c.shape) for c in consts],
        out_specs=[group_spec(o) for o in out_shape],
        out_shape=out_shape,
        scratch_shapes=[pltpu.VMEM((rows, IN_WIDTH), F32), pltpu.VMEM((rows, d), BF16)],
        compiler_params=_compiler_params(("arbitrary",)),
        name="sample_mixer",
    )(x, *states, *consts)


def _ffn(x, xs, p, w_up, w_down, layer, cast=()):
    n, d = x.shape
    ns = xs.shape[0]
    tile = FFN_TILE
    steps = n // tile
    assert steps * tile == n and ns <= tile
    consts = (p["fpre"], p["fpost"], w_up, w_down)
    prompt_tile = lambda i: (jnp.minimum(i, steps - 1), 0)
    cast_in_specs, cast_out_specs, cast_shapes = [], [], []
    for w, w_layer in cast:
        rows = w.shape[1] // steps
        assert rows * steps == w.shape[1] and rows % (2 * SUBLANES) == 0
        cast_in_specs.append(pl.BlockSpec(
            (None, rows, w.shape[2]), lambda i, w_layer=w_layer: (w_layer,) + prompt_tile(i)))
        cast_out_specs.append(pl.BlockSpec((rows, w.shape[2]), prompt_tile))
        cast_shapes.append(jax.ShapeDtypeStruct(w.shape[1:], BF16))
    return pl.pallas_call(
        functools.partial(_ffn_kernel, layer, len(cast)),
        grid=(steps + 1,),
        in_specs=[pl.BlockSpec((tile, d), prompt_tile), _whole_spec(xs.shape)]
        + [_const_spec(c.shape) for c in consts] + cast_in_specs,
        out_specs=[pl.BlockSpec((tile, d), prompt_tile), _whole_spec(xs.shape)] + cast_out_specs,
        out_shape=[jax.ShapeDtypeStruct((n, d), F32), jax.ShapeDtypeStruct((ns, d), F32)] + cast_shapes,
        compiler_params=_compiler_params(("arbitrary",)),
        name="ffn",
    )(x, xs, *consts, *[w for w, _ in cast])


def _pool_inverse_counts():
    window = np.repeat(np.asarray(POOL_WINDOWS, np.float32), SUB_DIM)[None, :]
    first = 1.0 / np.minimum(window, np.arange(1, ROW_BLOCK + 1, dtype=np.float32)[:, None])
    return first.astype(np.float32), (1.0 / window).astype(np.float32)


POOL_INV_FIRST, POOL_INV_REST = _pool_inverse_counts()


def _stacked_params(n_steps, norm_mix_pre, norm_mix_post, norm_ffn_pre, norm_ffn_post, w_pool,
                    pool_scale, conv_w, conv_b, conv_ln_g, conv_ln_b, sgu_ln_g, sgu_ln_b, sgu_w,
                    sgu_b, short_w):
    depth = w_pool.shape[0]
    gw = GROUP_WIDTH
    head_of_lane = np.arange(gw) // SUB_DIM
    same_head = head_of_lane[:, None] == head_of_lane[None, :]
    wp = jnp.tile(w_pool.reshape(depth, gw, SUB_DIM), (1, 1, N_SUB))
    wpool = jnp.where(same_head[None], wp, 0.0).astype(BF16)
    bias = jnp.repeat(jnp.swapaxes(sgu_b, 1, 2), SUB_DIM, axis=2)
    causal = np.tril(np.ones((n_steps, n_steps), dtype=bool))
    w4 = jnp.where(causal[None, None], sgu_w[:, :, :n_steps, :n_steps], 0.0)
    w4 = jnp.repeat(jnp.transpose(w4, (0, 2, 3, 1)), SUB_DIM, axis=3)
    w4 = w4.reshape(depth, n_steps * n_steps, gw)
    return dict(gpre=norm_mix_pre, gpost=norm_mix_post, fpre=norm_ffn_pre, fpost=norm_ffn_post,
                wpool=wpool, pscale=pool_scale, conv_w=conv_w, conv_b=conv_b, cln_g=conv_ln_g,
                cln_b=conv_ln_b, sln_g=sgu_ln_g, sln_b=sgu_ln_b, sgu_w=sgu_w, sgu_bias=bias,
                sgu_w4=w4, short_w=short_w)


def kernel(x_prompt, x_sample, state_pool, state_conv, state_short, norm_mix_pre, norm_mix_post, norm_ffn_pre, norm_ffn_post, w_in, w_out, w_pool, pool_scale, conv_w, conv_b, conv_ln_g, conv_ln_b, sgu_ln_g, sgu_ln_b, sgu_w, sgu_b, short_w, w_ffn_up, w_ffn_down):
    depth = w_in.shape[0]
    bp, seq, d = x_prompt.shape
    n_seq, n_steps, _ = x_sample.shape
    assert seq % PROMPT_TILE == 0 and PROMPT_TILE % MATMUL_ROWS == 0 and MATMUL_ROWS % ROW_BLOCK == 0
    assert ROW_BLOCK >= max(POOL_WINDOWS) and n_steps <= CHUNK and PAST_LEN % CHUNK == 0

    p = _stacked_params(n_steps, norm_mix_pre, norm_mix_post, norm_ffn_pre, norm_ffn_post, w_pool,
                        pool_scale, conv_w, conv_b, conv_ln_g, conv_ln_b, sgu_ln_g, sgu_ln_b, sgu_w,
                        sgu_b, short_w)
    yp = x_prompt
    ys = jnp.transpose(x_sample, (1, 0, 2))
    hist_major = lambda a: jnp.transpose(a, (0, 2, 1, 3))
    pool_in, conv_in, short_in = hist_major(state_pool), hist_major(state_conv), hist_major(state_short)
    prompt_states = [[] for _ in range(3)]
    sample_states = [[] for _ in range(4)]
    w_in_b, w_out_b = w_in[0].astype(BF16), w_out[0].astype(BF16)
    for l in range(depth):
        yp, pool_p, conv_p, short_p, w_up_b, w_down_b = _prompt_mixer(
            yp, p, w_in_b, w_out_b, w_ffn_up, w_ffn_down, l)
        ys, pool_s, conv_s, short_s, v_s = _sample_mixer(
            ys, pool_in, conv_in, short_in, p, w_in_b, w_out_b, l)
        cast = ((w_in, l + 1), (w_out, l + 1)) if l + 1 < depth else ()
        yp, ys, *w_next = _ffn(yp.reshape(bp * seq, d), ys.reshape(n_steps * n_seq, d), p, w_up_b,
                               w_down_b, l, cast)
        yp, ys = yp.reshape(bp, seq, d), ys.reshape(n_steps, n_seq, d)
        if w_next:
            w_in_b, w_out_b = w_next

        for lst, val in zip(prompt_states, (pool_p, conv_p, short_p)):
            lst.append(val)
        for lst, val in zip(sample_states, (pool_s, conv_s, short_s, v_s)):
            lst.append(val)

    ys = jnp.transpose(ys, (1, 0, 2))
    pool_p, conv_p, short_p = (jnp.stack(o) for o in prompt_states)
    pool_s, conv_s, short_s, v_s = (hist_major(jnp.stack(o)) for o in sample_states)
    return (yp, ys, pool_p, pool_s, conv_p, conv_s, short_p, short_s, v_s)
```

```python
import functools

import jax
import jax.numpy as jnp
import numpy as np
from jax import lax
from jax.experimental import pallas as pl
from jax.experimental.pallas import tpu as pltpu

D_MODEL = 1024
GROUP_WIDTH = 256
N_SUB = 4
SUB_DIM = 64
POOL_WINDOWS = (2, 4, 8, 16)
POOL_BUF = 15
CONV_WIDTH = 31
SHORT_WIDTH = 3
CHUNK = 128
D_FF = 4096
EPS = 1e-6
PAST_LEN = 16384
IN_WIDTH = 8 * GROUP_WIDTH

SUBLANES = 8
LANES = 128
N_SLABS = GROUP_WIDTH // LANES
POOL_HIST = 16
CONV_HIST = 32
SHORT_HIST = 8

ROW_BLOCK = CHUNK
MATMUL_ROWS = 512
PROMPT_TILE = 1024
SAMPLE_SEQS = 64
FFN_TILE = 1024
VMEM_LIMIT_BYTES = 56 * 1024 * 1024

F32 = jnp.float32
BF16 = jnp.bfloat16


def _rms_norm(x, g):
    ms = jnp.mean(x * x, axis=-1, keepdims=True)
    return x * lax.rsqrt(ms + EPS) * g


def _dot(a, b):
    return jnp.dot(a, b, preferred_element_type=F32)


def _head_mean(x, low_head):
    s_low = jnp.sum(jnp.where(low_head, x, 0.0), axis=-1, keepdims=True)
    s_high = jnp.sum(jnp.where(low_head, 0.0, x), axis=-1, keepdims=True)
    return jnp.where(low_head, s_low, s_high) * (1.0 / SUB_DIM)


def _head_layer_norm(x, g, b):
    low_head = lax.broadcasted_iota(jnp.int32, (x.shape[0], LANES), 1) < SUB_DIM
    out = []
    for s in range(N_SLABS):
        lanes = slice(s * LANES, (s + 1) * LANES)
        xs = x[:, lanes]
        xc = xs - _head_mean(xs, low_head)
        var = _head_mean(xc * xc, low_head)
        out.append(xc * lax.rsqrt(var + EPS) * g[:, lanes] + b[:, lanes])
    return jnp.concatenate(out, axis=1)


def _lane_group(shape):
    return jnp.right_shift(lax.broadcasted_iota(jnp.int32, shape, 1), SUB_DIM.bit_length() - 1)


def _pool_window(shape):
    grp = _lane_group(shape)
    w = jnp.full(shape, POOL_WINDOWS[0], jnp.int32)
    for gi in range(1, N_SUB):
        w = jnp.where(grp == gi, POOL_WINDOWS[gi], w)
    return grp, w


def _cols(group, slab):
    lo = group * GROUP_WIDTH + slab * LANES
    return slice(lo, lo + LANES)


def _trailing_sum(ext_ref, slab, start, n_rows, width):
    acc = ext_ref[slab, start:start + n_rows, :]
    for k in range(1, width):
        acc = acc + ext_ref[slab, start - k:start - k + n_rows, :]
    return acc


def _causal_taps(ext_ref, slab, w_ref, hist, n_taps, r):
    first = hist + r - (n_taps - 1)
    lanes = slice(slab * LANES, (slab + 1) * LANES)
    acc = None
    for k in range(n_taps):
        term = ext_ref[slab, first + k:first + k + ROW_BLOCK, :] * w_ref[k:k + 1, lanes]
        acc = term if acc is None else acc + term
    return acc


def _gate_block(r0, z_ref, y_ref, vn_ref, swt_ref, sbias_ref):
    gw = GROUP_WIDTH
    chunks = range(r0, r0 + MATMUL_ROWS, ROW_BLOCK)
    vt = [vn_ref[r:r + ROW_BLOCK, :].T for r in chunks]
    mixed = []
    for h in range(N_SUB):
        head = slice(h * SUB_DIM, (h + 1) * SUB_DIM)
        lhs = jnp.concatenate([v[head, :] for v in vt], axis=0)
        mixed.append(_dot(lhs, swt_ref[h]))
    for ci, r in enumerate(chunks):
        rows = slice(r, r + ROW_BLOCK)
        part = slice(ci * SUB_DIM, (ci + 1) * SUB_DIM)
        sg = jnp.concatenate([m[part, :] for m in mixed], axis=0).T + sbias_ref[...]
        y_ref[rows, 2 * gw:3 * gw] = (z_ref[rows, 3 * gw:4 * gw] * sg).astype(BF16)


def _mix_rows(r, seq_start, z_ref, y_ref, vn_ref, pool_ext, conv_ext, short_ext,
              pinv_first_ref, pinv_rest_ref, convw_ref, convb_ref, clng_ref, clnb_ref,
              slng_ref, slnb_ref, shortw_ref):
    rows = slice(r, r + ROW_BLOCK)
    gw = GROUP_WIDTH
    low_head = lax.broadcasted_iota(jnp.int32, (ROW_BLOCK, LANES), 1) < SUB_DIM

    d = []
    for s in range(N_SLABS):
        lanes = slice(s * LANES, (s + 1) * LANES)
        a = z_ref[rows, _cols(0, s)]
        base = POOL_HIST + r
        pool_ext[s, base:base + ROW_BLOCK, :] = a
        small, large = POOL_WINDOWS[2 * s], POOL_WINDOWS[2 * s + 1]
        assert large == 2 * small
        if small % SUBLANES == 0:
            run = _trailing_sum(pool_ext, s, base - small, ROW_BLOCK + small, small)
            s_small = run[small:]
            s_large = s_small + run[:ROW_BLOCK]
        else:
            s_small = _trailing_sum(pool_ext, s, base, ROW_BLOCK, small)
            s_large = s_small + _trailing_sum(pool_ext, s, base - small, ROW_BLOCK, small)
        win = jnp.where(low_head, s_small, s_large)
        inv = pinv_rest_ref[:, lanes]
        if seq_start is not None:
            inv = jnp.where(seq_start, pinv_first_ref[:, lanes], inv)
        d.append(win * inv - a)
    d = jnp.concatenate(d, axis=1).astype(BF16)
    y_ref[rows, 0:gw] = d

    c = []
    for s in range(N_SLABS):
        g = z_ref[rows, _cols(1, s)] * jax.nn.sigmoid(z_ref[rows, _cols(2, s)])
        conv_ext[s, CONV_HIST + r:CONV_HIST + r + ROW_BLOCK, :] = g
        c.append(_causal_taps(conv_ext, s, convw_ref, CONV_HIST, CONV_WIDTH, r))
    c = jnp.concatenate(c, axis=1) + convb_ref[...]
    c = _head_layer_norm(c, clng_ref[...], clnb_ref[...])
    y_ref[rows, gw:2 * gw] = (c * jax.nn.sigmoid(c)).astype(BF16)

    vn = _head_layer_norm(z_ref[rows, 4 * gw:5 * gw], slng_ref[...], slnb_ref[...])
    vn_ref[rows, :] = vn.astype(BF16)

    for s in range(N_SLABS):
        ch = z_ref[rows, _cols(6, s)] * z_ref[rows, _cols(7, s)]
        short_ext[s, SHORT_HIST + r:SHORT_HIST + r + ROW_BLOCK, :] = ch
        sc = _causal_taps(short_ext, s, shortw_ref, SHORT_HIST, SHORT_WIDTH, r)
        y_ref[rows, _cols(3, s)] = (z_ref[rows, _cols(5, s)] * sc).astype(BF16)


def _layer_views(layer, *refs):
    return [r.at[pl.ds(layer, 1)] if len(r.shape) == 2 else r.at[layer] for r in refs]


def _prompt_mixer_kernel(layer, x_ref, win_ref, wout_ref, pinv_first_ref, pinv_rest_ref,
                         gpre_ref, gpost_ref, wpool_ref, pscale_ref, convw_ref, convb_ref, clng_ref,
                         clnb_ref, slng_ref, slnb_ref, sguw_ref, sbias_ref, shortw_ref,
                         wup_f32_ref, wdown_f32_ref,
                         out_ref, pool_out_ref, conv_out_ref, short_out_ref, wup_ref, wdown_ref,
                         z_ref, y_ref, vn_ref, swt_ref, pool_ext, conv_ext, short_ext):
    (gpre_ref, gpost_ref, wpool_ref, pscale_ref, convw_ref, convb_ref, clng_ref, clnb_ref, slng_ref,
     slnb_ref, sguw_ref, sbias_ref, shortw_ref) = _layer_views(
         layer, gpre_ref, gpost_ref, wpool_ref, pscale_ref, convw_ref, convb_ref, clng_ref, clnb_ref,
         slng_ref, slnb_ref, sguw_ref, sbias_ref, shortw_ref)
    j = pl.program_id(1)
    tile = x_ref.shape[0]
    @pl.when((pl.program_id(0) == 0) & (j == 0))
    def _():
        causal = (lax.broadcasted_iota(jnp.int32, (CHUNK, CHUNK), 0)
                  >= lax.broadcasted_iota(jnp.int32, (CHUNK, CHUNK), 1))
        for hd in range(N_SUB):
            swt_ref[hd] = jnp.where(causal, sguw_ref[hd], 0.0).T.astype(BF16)

    @pl.when(j == 0)
    def _():
        pool_ext[:, 0:POOL_HIST, :] = jnp.zeros((N_SLABS, POOL_HIST, LANES), F32)
        conv_ext[:, 0:CONV_HIST, :] = jnp.zeros((N_SLABS, CONV_HIST, LANES), F32)
        short_ext[:, 0:SHORT_HIST, :] = jnp.zeros((N_SLABS, SHORT_HIST, LANES), F32)

    def project_out(r0):
        rows = slice(r0, r0 + MATMUL_ROWS)
        pooled = _dot(y_ref[rows, 0:GROUP_WIDTH], wpool_ref[...]) * pscale_ref[...]
        y_ref[rows, 0:GROUP_WIDTH] = pooled.astype(BF16)
        o = _dot(y_ref[rows, :], wout_ref[...])
        out_ref[rows, :] = x_ref[rows, :] + _rms_norm(o, gpost_ref[...])

    for r0 in range(0, tile, MATMUL_ROWS):
        rows = slice(r0, r0 + MATMUL_ROWS)
        h = _rms_norm(x_ref[rows, :], gpre_ref[...]).astype(BF16)
        z_ref[rows, :] = _dot(h, win_ref[...])
        if r0 == 0:
            wup_ref[...] = wup_f32_ref[...].astype(BF16)
            wdown_ref[...] = wdown_f32_ref[...].astype(BF16)
        for r in range(r0, r0 + MATMUL_ROWS, ROW_BLOCK):
            _mix_rows(r, (j == 0) if r == 0 else None, z_ref, y_ref, vn_ref, pool_ext, conv_ext,
                      short_ext, pinv_first_ref, pinv_rest_ref, convw_ref, convb_ref, clng_ref,
                      clnb_ref, slng_ref, slnb_ref, shortw_ref)
            if r == r0 and r0 > 0:
                _gate_block(r0 - MATMUL_ROWS, z_ref, y_ref, vn_ref, swt_ref, sbias_ref)
                project_out(r0 - MATMUL_ROWS)
    _gate_block(tile - MATMUL_ROWS, z_ref, y_ref, vn_ref, swt_ref, sbias_ref)
    project_out(tile - MATMUL_ROWS)

    pool_ext[:, 0:POOL_HIST, :] = pool_ext[:, tile:tile + POOL_HIST, :]
    conv_ext[:, 0:CONV_HIST, :] = conv_ext[:, tile:tile + CONV_HIST, :]
    short_ext[:, 0:SHORT_HIST, :] = short_ext[:, tile:tile + SHORT_HIST, :]

    @pl.when(j == pl.num_programs(1) - 1)
    def _():
        for s in range(N_SLABS):
            lanes = slice(s * LANES, (s + 1) * LANES)
            pool_out_ref[:, lanes] = pool_ext[s, POOL_HIST - POOL_BUF:POOL_HIST, :]
            conv_out_ref[:, lanes] = conv_ext[s, CONV_HIST - (CONV_WIDTH - 1):CONV_HIST, :]
            short_out_ref[:, lanes] = short_ext[s, SHORT_HIST - (SHORT_WIDTH - 1):SHORT_HIST, :]


def _sample_mixer_kernel(layer, x_ref, pool_in_ref, conv_in_ref, short_in_ref, win_ref, wout_ref,
                         gpre_ref, gpost_ref, wpool_ref, pscale_ref, convw_ref, convb_ref, clng_ref,
                         clnb_ref, slng_ref, slnb_ref, sgw_ref, sgb_ref, shortw_ref,
                         out_ref, pool_out_ref, conv_out_ref, short_out_ref, v_out_ref,
                         z_ref, y_ref):
    (gpre_ref, gpost_ref, wpool_ref, pscale_ref, convw_ref, convb_ref, clng_ref, clnb_ref, slng_ref,
     slnb_ref, sgw_ref, sgb_ref, shortw_ref) = _layer_views(
         layer, gpre_ref, gpost_ref, wpool_ref, pscale_ref, convw_ref, convb_ref, clng_ref, clnb_ref,
         slng_ref, slnb_ref, sgw_ref, sgb_ref, shortw_ref)
    n_steps, n_seq = v_out_ref.shape[0], v_out_ref.shape[1]
    gw = GROUP_WIDTH
    x = x_ref[...].reshape(n_steps * n_seq, x_ref.shape[2])
    h = _rms_norm(x, gpre_ref[...]).astype(BF16)
    z_ref[...] = _dot(h, win_ref[...])
    grp, w = _pool_window((n_seq, gw))
    cnt = jnp.minimum(w, PAST_LEN + 1).astype(F32)

    def slab(t):
        return slice(t * n_seq, (t + 1) * n_seq)

    a_new = [z_ref[slab(t), 0:gw] for t in range(n_steps)]
    g_new = [z_ref[slab(t), gw:2 * gw] * jax.nn.sigmoid(z_ref[slab(t), 2 * gw:3 * gw])
             for t in range(n_steps)]
    ch_new = [z_ref[slab(t), 6 * gw:7 * gw] * z_ref[slab(t), 7 * gw:8 * gw]
              for t in range(n_steps)]

    def pool_row(i):
        return pool_in_ref[i] if i < POOL_BUF else a_new[i - POOL_BUF]

    def conv_row(i):
        return conv_in_ref[i] if i < CONV_WIDTH - 1 else g_new[i - (CONV_WIDTH - 1)]

    def short_row(i):
        return short_in_ref[i] if i < SHORT_WIDTH - 1 else ch_new[i - (SHORT_WIDTH - 1)]

    vn = []
    for t in range(n_steps):
        rows = slab(t)
        end = POOL_BUF + t
        acc = pool_row(end)
        sums = []
        for k in range(1, max(POOL_WINDOWS)):
            acc = acc + pool_row(end - k)
            if k + 1 in POOL_WINDOWS:
                sums.append(acc)
        win = sums[0]
        for gi in range(1, N_SUB):
            win = jnp.where(grp == gi, sums[gi], win)
        d = win / cnt - a_new[t]
        y_ref[rows, 0:gw] = d.astype(BF16)

        c = None
        for k in range(CONV_WIDTH):
            term = conv_row(t + k) * convw_ref[k:k + 1, :]
            c = term if c is None else c + term
        c = _head_layer_norm(c + convb_ref[...], clng_ref[...], clnb_ref[...])
        y_ref[rows, gw:2 * gw] = (c * jax.nn.sigmoid(c)).astype(BF16)

        vn.append(_head_layer_norm(z_ref[rows, 4 * gw:5 * gw], slng_ref[...], slnb_ref[...]))
        v_out_ref[t] = vn[t]
        s = sgb_ref[t:t + 1, :]
        for u in range(t + 1):
            s = s + sgw_ref[t * n_steps + u:t * n_steps + u + 1, :] * vn[u]
        y_ref[rows, 2 * gw:3 * gw] = (z_ref[rows, 3 * gw:4 * gw] * s).astype(BF16)

        sc = None
        for k in range(SHORT_WIDTH):
            term = short_row(t + k) * shortw_ref[k:k + 1, :]
            sc = term if sc is None else sc + term
        y_ref[rows, 3 * gw:4 * gw] = (z_ref[rows, 5 * gw:6 * gw] * sc).astype(BF16)

    pooled = _dot(y_ref[:, 0:gw], wpool_ref[...]) * pscale_ref[...]
    y_ref[:, 0:gw] = pooled.astype(BF16)
    o = _dot(y_ref[...], wout_ref[...])
    out_ref[...] = (x + _rms_norm(o, gpost_ref[...])).reshape(out_ref.shape)

    for i in range(POOL_BUF):
        pool_out_ref[i] = pool_row(i + n_steps)
    for i in range(CONV_WIDTH - 1):
        conv_out_ref[i] = conv_row(i + n_steps)
    for i in range(SHORT_WIDTH - 1):
        short_out_ref[i] = short_row(i + n_steps)


def _mlp(x_ref, out_ref, gpre_ref, gpost_ref, wup_ref, wdown_ref):
    x = x_ref[...]
    f = _rms_norm(x, gpre_ref[...]).astype(BF16)
    u = jnp.maximum(_dot(f, wup_ref[...]), 0.0)
    o = _dot((u * u).astype(BF16), wdown_ref[...])
    out_ref[...] = x + _rms_norm(o, gpost_ref[...])


def _ffn_kernel(layer, n_cast, x_ref, xs_ref, gpre_ref, gpost_ref, wup_ref, wdown_ref, *rest):
    cast_in, out_ref, outs_ref = rest[:n_cast], rest[n_cast], rest[n_cast + 1]
    cast_out = rest[n_cast + 2:]
    gpre_ref, gpost_ref = _layer_views(layer, gpre_ref, gpost_ref)
    i = pl.program_id(0)
    last = pl.num_programs(0) - 1

    @pl.when(i < last)
    def _():
        for src, dst in zip(cast_in, cast_out):
            dst[...] = src[...].astype(BF16)
        _mlp(x_ref, out_ref, gpre_ref, gpost_ref, wup_ref, wdown_ref)

    @pl.when(i == last)
    def _():
        _mlp(xs_ref, outs_ref, gpre_ref, gpost_ref, wup_ref, wdown_ref)


def _const_spec(shape):
    nd = len(shape)
    return pl.BlockSpec(shape, lambda *_: (0,) * nd, pipeline_mode=pl.Buffered(1))


def _whole_spec(shape):
    nd = len(shape)
    return pl.BlockSpec(shape, lambda *_: (0,) * nd)


def _compiler_params(semantics):
    return pltpu.CompilerParams(dimension_semantics=semantics, vmem_limit_bytes=VMEM_LIMIT_BYTES)


def _prompt_mixer(x, p, w_in, w_out, w_ffn_up, w_ffn_down, layer):
    b, s, d = x.shape
    tile = PROMPT_TILE
    consts = (w_in, w_out, POOL_INV_FIRST, POOL_INV_REST,
              p["gpre"], p["gpost"], p["wpool"], p["pscale"], p["conv_w"], p["conv_b"], p["cln_g"],
              p["cln_b"], p["sln_g"], p["sln_b"], p["sgu_w"], p["sgu_bias"], p["short_w"])
    gw = GROUP_WIDTH
    tiles_per_seq = s // tile
    steps = b * tiles_per_seq
    d_ff = w_ffn_up.shape[2]
    ff_slice = d_ff // steps
    assert ff_slice * steps == d_ff and ff_slice % LANES == 0
    step = lambda bi, j: bi * tiles_per_seq + j
    state_spec = lambda n: pl.BlockSpec((None, n, gw), lambda bi, j: (bi, 0, 0))
    return pl.pallas_call(
        functools.partial(_prompt_mixer_kernel, layer),
        grid=(b, tiles_per_seq),
        in_specs=[pl.BlockSpec((None, tile, d), lambda bi, j: (bi, j, 0))]
        + [_const_spec(c.shape) for c in consts]
        + [pl.BlockSpec((None, d, ff_slice), lambda bi, j: (layer, 0, step(bi, j))),
           pl.BlockSpec((None, ff_slice, d), lambda bi, j: (layer, step(bi, j), 0))],
        out_specs=[pl.BlockSpec((None, tile, d), lambda bi, j: (bi, j, 0)),
                   state_spec(POOL_BUF), state_spec(CONV_WIDTH - 1), state_spec(SHORT_WIDTH - 1),
                   pl.BlockSpec((d, ff_slice), lambda bi, j: (0, step(bi, j))),
                   pl.BlockSpec((ff_slice, d), lambda bi, j: (step(bi, j), 0))],
        out_shape=[jax.ShapeDtypeStruct((b, s, d), F32),
                   jax.ShapeDtypeStruct((b, POOL_BUF, gw), F32),
                   jax.ShapeDtypeStruct((b, CONV_WIDTH - 1, gw), F32),
                   jax.ShapeDtypeStruct((b, SHORT_WIDTH - 1, gw), F32),
                   jax.ShapeDtypeStruct((d, d_ff), BF16),
                   jax.ShapeDtypeStruct((d_ff, d), BF16)],
        scratch_shapes=[pltpu.VMEM((tile, IN_WIDTH), F32),
                        pltpu.VMEM((tile, d), BF16),
                        pltpu.VMEM((tile, gw), BF16),
                        pltpu.VMEM((N_SUB, CHUNK, CHUNK), BF16),
                        pltpu.VMEM((N_SLABS, POOL_HIST + tile, LANES), F32),
                        pltpu.VMEM((N_SLABS, CONV_HIST + tile, LANES), F32),
                        pltpu.VMEM((N_SLABS, SHORT_HIST + tile, LANES), F32)],
        compiler_params=_compiler_params(("arbitrary", "arbitrary")),
        name="prompt_mixer",
    )(x, *consts, w_ffn_up, w_ffn_down)


def _sample_mixer(x, pool_st, conv_st, short_st, p, w_in, w_out, layer):
    n_steps, n_seq, d = x.shape
    group = SAMPLE_SEQS
    assert n_seq % group == 0 and group % SUBLANES == 0
    gw = GROUP_WIDTH
    states = (pool_st, conv_st, short_st)
    consts = (w_in, w_out, p["gpre"], p["gpost"], p["wpool"], p["pscale"], p["conv_w"], p["conv_b"],
              p["cln_g"], p["cln_b"], p["sln_g"], p["sln_b"], p["sgu_w4"], p["sgu_bias"], p["short_w"])
    group_spec = lambda a: pl.BlockSpec((a.shape[0], group, a.shape[2]), lambda i: (0, i, 0))
    state_spec = lambda st: pl.BlockSpec((None, st.shape[1], group, gw), lambda i: (layer, 0, i, 0))
    out_shape = ([jax.ShapeDtypeStruct((n_steps, n_seq, d), F32)]
                 + [jax.ShapeDtypeStruct(st.shape[1:], F32) for st in states]
                 + [jax.ShapeDtypeStruct((n_steps, n_seq, gw), F32)])
    rows = n_steps * group
    return pl.pallas_call(
        functools.partial(_sample_mixer_kernel, layer),
        grid=(n_seq // group,),
        in_specs=[group_spec(x)] + [state_spec(st) for st in states]
        + [_const_spec(c.shape) for c in consts],
        out_specs=[group_spec(o) for o in out_shape],
        out_shape=out_shape,
        scratch_shapes=[pltpu.VMEM((rows, IN_WIDTH), F32), pltpu.VMEM((rows, d), BF16)],
        compiler_params=_compiler_params(("arbitrary",)),
        name="sample_mixer",
    )(x, *states, *consts)


def _ffn(x, xs, p, w_up, w_down, layer, cast=()):
    n, d = x.shape
    ns = xs.shape[0]
    tile = FFN_TILE
    steps = n // tile
    assert steps * tile == n and ns <= tile
    consts = (p["fpre"], p["fpost"], w_up, w_down)
    prompt_tile = lambda i: (jnp.minimum(i, steps - 1), 0)
    cast_in_specs, cast_out_specs, cast_shapes = [], [], []
    for w, w_layer in cast:
        rows = w.shape[1] // steps
        assert rows * steps == w.shape[1] and rows % (2 * SUBLANES) == 0
        cast_in_specs.append(pl.BlockSpec(
            (None, rows, w.shape[2]), lambda i, w_layer=w_layer: (w_layer,) + prompt_tile(i)))
        cast_out_specs.append(pl.BlockSpec((rows, w.shape[2]), prompt_tile))
        cast_shapes.append(jax.ShapeDtypeStruct(w.shape[1:], BF16))
    return pl.pallas_call(
        functools.partial(_ffn_kernel, layer, len(cast)),
        grid=(steps + 1,),
        in_specs=[pl.BlockSpec((tile, d), prompt_tile), _whole_spec(xs.shape)]
        + [_const_spec(c.shape) for c in consts] + cast_in_specs,
        out_specs=[pl.BlockSpec((tile, d), prompt_tile), _whole_spec(xs.shape)] + cast_out_specs,
        out_shape=[jax.ShapeDtypeStruct((n, d), F32), jax.ShapeDtypeStruct((ns, d), F32)] + cast_shapes,
        compiler_params=_compiler_params(("arbitrary",)),
        name="ffn",
    )(x, xs, *consts, *[w for w, _ in cast])


def _pool_inverse_counts():
    window = np.repeat(np.asarray(POOL_WINDOWS, np.float32), SUB_DIM)[None, :]
    first = 1.0 / np.minimum(window, np.arange(1, ROW_BLOCK + 1, dtype=np.float32)[:, None])
    return first.astype(np.float32), (1.0 / window).astype(np.float32)


POOL_INV_FIRST, POOL_INV_REST = _pool_inverse_counts()


def _stacked_params(n_steps, norm_mix_pre, norm_mix_post, norm_ffn_pre, norm_ffn_post, w_pool,
                    pool_scale, conv_w, conv_b, conv_ln_g, conv_ln_b, sgu_ln_g, sgu_ln_b, sgu_w,
                    sgu_b, short_w):
    depth = w_pool.shape[0]
    gw = GROUP_WIDTH
    head_of_lane = np.arange(gw) // SUB_DIM
    same_head = head_of_lane[:, None] == head_of_lane[None, :]
    wp = jnp.tile(w_pool.reshape(depth, gw, SUB_DIM), (1, 1, N_SUB))
    wpool = jnp.where(same_head[None], wp, 0.0).astype(BF16)
    bias = jnp.repeat(jnp.swapaxes(sgu_b, 1, 2), SUB_DIM, axis=2)
    causal = np.tril(np.ones((n_steps, n_steps), dtype=bool))
    w4 = jnp.where(causal[None, None], sgu_w[:, :, :n_steps, :n_steps], 0.0)
    w4 = jnp.repeat(jnp.transpose(w4, (0, 2, 3, 1)), SUB_DIM, axis=3)
    w4 = w4.reshape(depth, n_steps * n_steps, gw)
    return dict(gpre=norm_mix_pre, gpost=norm_mix_post, fpre=norm_ffn_pre, fpost=norm_ffn_post,
                wpool=wpool, pscale=pool_scale, conv_w=conv_w, conv_b=conv_b, cln_g=conv_ln_g,
                cln_b=conv_ln_b, sln_g=sgu_ln_g, sln_b=sgu_ln_b, sgu_w=sgu_w, sgu_bias=bias,
                sgu_w4=w4, short_w=short_w)


def kernel(x_prompt, x_sample, state_pool, state_conv, state_short, norm_mix_pre, norm_mix_post, norm_ffn_pre, norm_ffn_post, w_in, w_out, w_pool, pool_scale, conv_w, conv_b, conv_ln_g, conv_ln_b, sgu_ln_g, sgu_ln_b, sgu_w, sgu_b, short_w, w_ffn_up, w_ffn_down):
    depth = w_in.shape[0]
    bp, seq, d = x_prompt.shape
    n_seq, n_steps, _ = x_sample.shape
    assert seq % PROMPT_TILE == 0 and PROMPT_TILE % MATMUL_ROWS == 0 and MATMUL_ROWS % ROW_BLOCK == 0
    assert ROW_BLOCK >= max(POOL_WINDOWS) and n_steps <= CHUNK and PAST_LEN % CHUNK == 0

    p = _stacked_params(n_steps, norm_mix_pre, norm_mix_post, norm_ffn_pre, norm_ffn_post, w_pool,
                        pool_scale, conv_w, conv_b, conv_ln_g, conv_ln_b, sgu_ln_g, sgu_ln_b, sgu_w,
                        sgu_b, short_w)
    yp = x_prompt
    ys = jnp.transpose(x_sample, (1, 0, 2))
    hist_major = lambda a: jnp.transpose(a, (0, 2, 1, 3))
    pool_in, conv_in, short_in = hist_major(state_pool), hist_major(state_conv), hist_major(state_short)
    prompt_states = [[] for _ in range(3)]
    sample_states = [[] for _ in range(4)]
    w_in_b, w_out_b = w_in[0].astype(BF16), w_out[0].astype(BF16)
    for l in range(depth):
        yp, pool_p, conv_p, short_p, w_up_b, w_down_b = _prompt_mixer(
            yp, p, w_in_b, w_out_b, w_ffn_up, w_ffn_down, l)
        ys, pool_s, conv_s, short_s, v_s = _sample_mixer(
            ys, pool_in, conv_in, short_in, p, w_in_b, w_out_b, l)
        cast = ((w_in, l + 1), (w_out, l + 1)) if l + 1 < depth else ()
        yp, ys, *w_next = _ffn(yp.reshape(bp * seq, d), ys.reshape(n_steps * n_seq, d), p, w_up_b,
                               w_down_b, l, cast)
        yp, ys = yp.reshape(bp, seq, d), ys.reshape(n_steps, n_seq, d)
        if w_next:
            w_in_b, w_out_b = w_next

        for lst, val in zip(prompt_states, (pool_p, conv_p, short_p)):
            lst.append(val)
        for lst, val in zip(sample_states, (pool_s, conv_s, short_s, v_s)):
            lst.append(val)

    ys = jnp.transpose(ys, (1, 0, 2))
    pool_p, conv_p, short_p = (jnp.stack(o) for o in prompt_states)
    pool_s, conv_s, short_s, v_s = (hist_major(jnp.stack(o)) for o in sample_states)
    return (yp, ys, pool_p, pool_s, conv_p, conv_s, short_p, short_s, v_s)
```
